```python
import jax, jax.numpy as jnp
from jax import lax

D_MODEL = 1024
BATCH = 8
SEQ = 2048
DEPTH = 4

D_MIX = D_MODEL
FOUR_W = D_MIX // 4
FOUR_HEADS = 4
FOUR_HD = FOUR_W // FOUR_HEADS
GMLP_W = D_MIX // 4
GMLP_HEADS = 4
GMLP_HD = GMLP_W // GMLP_HEADS
GMLP_CHUNK = 128
MLSTM_W = D_MIX - FOUR_W - GMLP_W
MLSTM_HEADS = 4
MLSTM_DV = MLSTM_W // MLSTM_HEADS
MLSTM_DQK = MLSTM_DV // 2
MLSTM_QK_W = MLSTM_HEADS * MLSTM_DQK
MLSTM_CHUNK = 128
CONV_K = 3
N_GATES = 4 * MLSTM_HEADS
SPLIT_SIZES = (FOUR_W, GMLP_W, GMLP_W, 2 * MLSTM_QK_W, MLSTM_W, MLSTM_W, N_GATES)
D_IN = FOUR_W + 2 * GMLP_W + 2 * MLSTM_QK_W + 2 * MLSTM_W + N_GATES
N_EXPERTS = 128
TOP_K = 8
D_EXPERT = D_MODEL // 4
MOE_BLOCK = 128
PLE_DIM = 256
EPS = 1e-6
F_BIAS_INIT = 3.0

kernel_name = "hybrid_fourier_gmlp_mlstm_moe_encoder"


def rmsnorm(x, g):
    xf = x.astype(jnp.float32)
    y = xf * lax.rsqrt(jnp.mean(xf * xf, axis=-1, keepdims=True) + EPS)
    return (y * g.astype(jnp.float32)).astype(x.dtype)


def layernorm(x, g, b):
    xf = x.astype(jnp.float32)
    mu = jnp.mean(xf, axis=-1, keepdims=True)
    xc = xf - mu
    y = xc * lax.rsqrt(jnp.mean(xc * xc, axis=-1, keepdims=True) + EPS)
    return (y * g.astype(jnp.float32) + b.astype(jnp.float32)).astype(x.dtype)


def fourier_mixer(z, w):
    B, S, _ = z.shape
    zh = z.reshape(B, S, FOUR_HEADS, FOUR_HD).astype(jnp.float32)
    f = jnp.fft.fft2(zh, axes=(1, 3), norm="ortho").real
    y = jnp.einsum('bshd,hde->bshe', f.astype(z.dtype), w)
    return y.reshape(B, S, FOUR_W)


def gmlp_mixer(zu, zv, ln_g, ln_b, ws, bs):
    B, S, _ = zu.shape
    u = jax.nn.gelu(zu)
    v = layernorm(jax.nn.gelu(zv), ln_g, ln_b)
    nc = S // GMLP_CHUNK
    vh = v.reshape(B, nc, GMLP_CHUNK, GMLP_HEADS, GMLP_HD)
    s = jnp.einsum('hij,bcjhd->bcihd', ws, vh) + bs.T[None, None, :, :, None]
    return u * s.reshape(B, S, GMLP_W)


def short_conv(x, w):
    C = x.shape[-1]
    return lax.conv_general_dilated(
        x, w[:, None, :], window_strides=(1,),
        padding=[(CONV_K // 2, CONV_K // 2)],
        dimension_numbers=('NWC', 'WIO', 'NWC'), feature_group_count=C)


def mlstm_dir(q, k, v, li, lf):
    B, H, S, DK = q.shape
    DV = v.shape[-1]
    L = MLSTM_CHUNK
    nc = S // L
    q = q.reshape(B, H, nc, L, DK)
    k = k.reshape(B, H, nc, L, DK)
    v = v.reshape(B, H, nc, L, DV)
    li = li.reshape(B, H, nc, L)
    lf = lf.reshape(B, H, nc, L)
    b = jnp.cumsum(lf, axis=-1)
    g = b[..., -1]
    a = g[..., None] - b + li
    m_loc = jnp.max(a, axis=-1)
    w = jnp.exp(a - m_loc[..., None])
    c_loc = jnp.einsum('bhcl,bhcld,bhcle->bhcde', w, k, v)
    n_loc = jnp.einsum('bhcl,bhcld->bhcd', w, k)

    def step(carry, xs):
        c, n, m = carry
        cl, nl, ml, gl = xs
        m_new = jnp.maximum(gl + m, ml)
        a_old = jnp.exp(gl + m - m_new)
        a_new = jnp.exp(ml - m_new)
        c2 = a_old[..., None, None] * c + a_new[..., None, None] * cl
        n2 = a_old[..., None] * n + a_new[..., None] * nl
        return (c2, n2, m_new), (c, n, m)

    init = (jnp.zeros((B, H, DK, DV), jnp.float32),
            jnp.zeros((B, H, DK), jnp.float32),
            jnp.zeros((B, H), jnp.float32))
    mv = lambda t: jnp.moveaxis(t, 2, 0)
    _, (c_prev, n_prev, m_prev) = lax.scan(step, init, (mv(c_loc), mv(n_loc), mv(m_loc), mv(g)))
    c_prev = jnp.moveaxis(c_prev, 0, 2)
    n_prev = jnp.moveaxis(n_prev, 0, 2)
    m_prev = jnp.moveaxis(m_prev, 0, 2)
    lower = jnp.tril(jnp.ones((L, L), dtype=bool))
    dmat = jnp.where(lower, b[..., :, None] - b[..., None, :] + li[..., None, :], -jnp.inf)
    m_intra = jnp.max(dmat, axis=-1)
    m_inter = b + m_prev[..., None]
    m_tot = jnp.maximum(m_intra, m_inter)
    s_mat = jnp.exp(dmat - m_tot[..., None]) * jnp.einsum('bhcjd,bhcsd->bhcjs', q, k)
    inter = jnp.exp(m_inter - m_tot)
    num = (jnp.einsum('bhcjs,bhcse->bhcje', s_mat, v)
           + inter[..., None] * jnp.einsum('bhcjd,bhcde->bhcje', q, c_prev))
    den = jnp.sum(s_mat, axis=-1) + inter * jnp.einsum('bhcjd,bhcd->bhcj', q, n_prev)
    h = num / jnp.maximum(jnp.abs(den), jnp.exp(-m_tot))[..., None]
    return h.reshape(B, H, S, DV)


def mlstm_mixer(zqk, zv, zo, zg, conv_w, gate_b, norm_g):
    B, S, _ = zv.shape
    H = MLSTM_HEADS
    qk = jax.nn.silu(short_conv(zqk, conv_w)).astype(jnp.float32)
    q = qk[..., :MLSTM_QK_W].reshape(B, S, H, MLSTM_DQK).transpose(0, 2, 1, 3) * (MLSTM_DQK ** -0.5)
    k = qk[..., MLSTM_QK_W:].reshape(B, S, H, MLSTM_DQK).transpose(0, 2, 1, 3)
    v = zv.astype(jnp.float32).reshape(B, S, H, MLSTM_DV).transpose(0, 2, 1, 3)
    gts = (zg.astype(jnp.float32) + gate_b.astype(jnp.float32)).reshape(B, S, 4, H).transpose(2, 0, 3, 1)
    i_f, f_f, i_b, f_b = gts[0], gts[1], gts[2], gts[3]
    h_fwd = mlstm_dir(q, k, v, i_f, jax.nn.log_sigmoid(f_f))
    flip = lambda t: jnp.flip(t, axis=2)
    h_bwd = flip(mlstm_dir(flip(q), flip(k), flip(v), flip(i_b), flip(jax.nn.log_sigmoid(f_b))))
    o = jax.nn.sigmoid(zo.astype(jnp.float32)).reshape(B, S, H, MLSTM_DV).transpose(0, 2, 1, 3)
    h = o * (h_fwd + h_bwd)
    mu = jnp.mean(h, axis=-1, keepdims=True)
    hc = h - mu
    h = hc * lax.rsqrt(jnp.mean(hc * hc, axis=-1, keepdims=True) + EPS)
    h = h.transpose(0, 2, 1, 3) * norm_g.astype(jnp.float32).reshape(H, MLSTM_DV)
    return h.reshape(B, S, MLSTM_W).astype(zv.dtype)


def swiglu(x, wg, wu, wd):
    return (jax.nn.silu(x @ wg) * (x @ wu)) @ wd


def moe(xm, router_w, router_b, wg, wu, wd, sg, su, sd):
    B, S, D = xm.shape
    T = B * S
    xt = xm.reshape(T, D)
    scores = jax.nn.sigmoid((xt @ router_w).astype(jnp.float32))
    _, idx = lax.top_k(scores + router_b.astype(jnp.float32), TOP_K)
    sel = jnp.take_along_axis(scores, idx, axis=1)
    gates = sel / jnp.sum(sel, axis=-1, keepdims=True)
    N = T * TOP_K
    flat_e = idx.reshape(N)
    flat_tok = jnp.repeat(jnp.arange(T, dtype=jnp.int32), TOP_K)
    flat_g = gates.reshape(N)
    order = jnp.argsort(flat_e)
    se, st, sgate = flat_e[order], flat_tok[order], flat_g[order]
    counts = jnp.bincount(flat_e, length=N_EXPERTS).astype(jnp.int32)
    pcounts = ((counts + MOE_BLOCK - 1) // MOE_BLOCK) * MOE_BLOCK
    offs = jnp.cumsum(counts) - counts
    pend = jnp.cumsum(pcounts)
    poffs = pend - pcounts
    ppos = poffs[se] + (jnp.arange(N, dtype=jnp.int32) - offs[se])
    n_blk = -(-N // MOE_BLOCK) + N_EXPERTS
    P = n_blk * MOE_BLOCK
    pad_tok = jnp.zeros((P,), jnp.int32).at[ppos].set(st)
    pad_g = jnp.zeros((P,), jnp.float32).at[ppos].set(sgate)
    blk_start = jnp.arange(n_blk, dtype=jnp.int32) * MOE_BLOCK
    blk_e = jnp.minimum(jnp.searchsorted(pend, blk_start, side='right'), N_EXPERTS - 1)

    def run_block(args):
        e, tok = args
        return swiglu(xt[tok], wg[e], wu[e], wd[e])

    ys = lax.map(run_block, (blk_e, pad_tok.reshape(n_blk, MOE_BLOCK)))
    routed = jnp.zeros((T, D), xt.dtype).at[pad_tok].add(
        ys.reshape(P, D) * pad_g[:, None].astype(ys.dtype))
    out = routed + swiglu(xt, sg, su, sd)
    return out.reshape(B, S, D)


def setup_inputs(seed: int = 0) -> dict:
    key = jax.random.key(seed)
    ks = jax.random.split(key, 26)
    nrm = lambda k, shape, scale: jax.random.normal(k, shape, jnp.float32) * scale
    gain = lambda k, shape: 1.0 + nrm(k, shape, 0.02)
    H = MLSTM_HEADS
    gate_off = jnp.concatenate([jnp.zeros((H,), jnp.float32), jnp.full((H,), F_BIAS_INIT, jnp.float32),
                                jnp.zeros((H,), jnp.float32), jnp.full((H,), F_BIAS_INIT, jnp.float32)])
    return {
        "x": nrm(ks[0], (BATCH, SEQ, D_MODEL), 1.0),
        "p": nrm(ks[1], (DEPTH, BATCH, SEQ, PLE_DIM), 1.0),
        "norm1_g": gain(ks[2], (DEPTH, D_MODEL)),
        "w_in": nrm(ks[3], (DEPTH, D_MODEL, D_IN), D_MODEL ** -0.5),
        "four_w": nrm(ks[4], (DEPTH, FOUR_HEADS, FOUR_HD, FOUR_HD), FOUR_HD ** -0.5),
        "gmlp_ln_g": gain(ks[5], (DEPTH, GMLP_W)),
        "gmlp_ln_b": nrm(ks[6], (DEPTH, GMLP_W), 0.02),
        "gmlp_ws": nrm(ks[7], (DEPTH, GMLP_HEADS, GMLP_CHUNK, GMLP_CHUNK), GMLP_CHUNK ** -0.5),
        "gmlp_bs": 1.0 + nrm(ks[8], (DEPTH, GMLP_HEADS, GMLP_CHUNK), 0.1),
        "mlstm_conv_w": nrm(ks[9], (DEPTH, CONV_K, 2 * MLSTM_QK_W), CONV_K ** -0.5),
        "mlstm_gate_b": gate_off + nrm(ks[10], (DEPTH, N_GATES), 0.1),
        "mlstm_norm_g": gain(ks[11], (DEPTH, MLSTM_W)),
        "w_out": nrm(ks[12], (DEPTH, D_MIX, D_MODEL), D_MIX ** -0.5),
        "norm2_g": gain(ks[13], (DEPTH, D_MODEL)),
        "router_w": nrm(ks[14], (DEPTH, D_MODEL, N_EXPERTS), D_MODEL ** -0.5),
        "router_b": nrm(ks[15], (DEPTH, N_EXPERTS), 0.01),
        "exp_w_gate": nrm(ks[16], (DEPTH, N_EXPERTS, D_MODEL, D_EXPERT), D_MODEL ** -0.5),
        "exp_w_up": nrm(ks[17], (DEPTH, N_EXPERTS, D_MODEL, D_EXPERT), D_MODEL ** -0.5),
        "exp_w_down": nrm(ks[18], (DEPTH, N_EXPERTS, D_EXPERT, D_MODEL), D_EXPERT ** -0.5),
        "sh_w_gate": nrm(ks[19], (DEPTH, D_MODEL, D_EXPERT), D_MODEL ** -0.5),
        "sh_w_up": nrm(ks[20], (DEPTH, D_MODEL, D_EXPERT), D_MODEL ** -0.5),
        "sh_w_down": nrm(ks[21], (DEPTH, D_EXPERT, D_MODEL), D_EXPERT ** -0.5),
        "ple_w_in": nrm(ks[22], (DEPTH, PLE_DIM, D_MODEL), PLE_DIM ** -0.5),
        "ple_w_gate": nrm(ks[23], (DEPTH, D_MODEL, D_MODEL), D_MODEL ** -0.5),
        "ple_norm_g": gain(ks[24], (DEPTH, D_MODEL)),
        "final_norm_g": gain(ks[25], (D_MODEL,)),
    }


def reference(x, p, norm1_g, w_in, four_w, gmlp_ln_g, gmlp_ln_b, gmlp_ws, gmlp_bs,
              mlstm_conv_w, mlstm_gate_b, mlstm_norm_g, w_out, norm2_g, router_w, router_b,
              exp_w_gate, exp_w_up, exp_w_down, sh_w_gate, sh_w_up, sh_w_down,
              ple_w_in, ple_w_gate, ple_norm_g, final_norm_g):
    split_idx = []
    acc = 0
    for s in SPLIT_SIZES[:-1]:
        acc += s
        split_idx.append(acc)
    h = x
    for i in range(DEPTH):
        a = rmsnorm(h, norm1_g[i])
        z = a @ w_in[i]
        zf, zu, zv, zqk, zmv, zo, zg = jnp.split(z, split_idx, axis=-1)
        mix = jnp.concatenate([
            fourier_mixer(zf, four_w[i]),
            gmlp_mixer(zu, zv, gmlp_ln_g[i], gmlp_ln_b[i], gmlp_ws[i], gmlp_bs[i]),
            mlstm_mixer(zqk, zmv, zo, zg, mlstm_conv_w[i], mlstm_gate_b[i], mlstm_norm_g[i]),
        ], axis=-1)
        h = h + mix @ w_out[i]
        h = h + moe(rmsnorm(h, norm2_g[i]), router_w[i], router_b[i],
                    exp_w_gate[i], exp_w_up[i], exp_w_down[i],
                    sh_w_gate[i], sh_w_up[i], sh_w_down[i])
        e = p[i] @ ple_w_in[i]
        h = h + rmsnorm(jax.nn.sigmoid(h @ ple_w_gate[i]) * e, ple_norm_g[i])
    return rmsnorm(h, final_norm_g)
```

```python
import functools
import math

import numpy as np
import jax
import jax.numpy as jnp
from jax import lax
from jax.experimental import pallas as pl
from jax.experimental.pallas import tpu as pltpu

F32, BF16, I32 = jnp.float32, jnp.bfloat16, jnp.int32
HIGHEST = lax.Precision.HIGHEST

EPS = 1e-6
LANES = 128
ROW_SUB = 8
VMEM_LIMIT_BYTES = 48 * 1024 * 1024

FOUR_HEADS, FOUR_HD = 4, 64
GMLP_HEADS, GMLP_HD, GMLP_CHUNK = 4, 64, 128
MLSTM_HEADS, MLSTM_DV, MLSTM_DQK, MLSTM_CHUNK = 4, 128, 64, 128
CONV_PAD = 16
N_GATES = 16
N_EXPERTS, TOP_K = 128, 8
EXPERT_BLOCK = 256
TOKEN_TILE = 512
DISPATCH_TILE = 256
COMBINE_TILE = 128


def _params(*sem):
    return pltpu.CompilerParams(dimension_semantics=sem, vmem_limit_bytes=VMEM_LIMIT_BYTES)


def _rms(x, g):
    return x * lax.rsqrt(jnp.mean(x * x, axis=-1, keepdims=True) + EPS) * g


def _full(shape):
    nd = len(shape)
    return pl.BlockSpec(shape, lambda *_: (0,) * nd)


def _dot(a, b):
    return jnp.dot(a, b, preferred_element_type=F32)


def _in_proj_body(h_ref, g_ref, w_ref, wg_ref, wgt_ref,
                  zf_ref, zu_ref, zv_ref, zqk_ref, zmv_ref, zo_ref, zg_ref, zgt_ref):
    ab = _rms(h_ref[...], g_ref[...]).astype(BF16)
    off = 0
    for o_ref in (zf_ref, zu_ref, zv_ref, zqk_ref, zmv_ref, zo_ref):
        n = o_ref.shape[-1]
        o_ref[...] = _dot(ab, w_ref[:, off:off + n]).astype(o_ref.dtype)
        off += n
    zg_ref[...] = _dot(ab, wg_ref[...])
    zgt_ref[...] = lax.dot_general(wgt_ref[...], ab, (((1,), (1,)), ((), ())),
                                   preferred_element_type=F32)


def _in_proj(h, g, w_main, w_gate, w_gate_t):
    t, d = h.shape
    tm = min(TOKEN_TILE, t)
    widths = (256, 256, 256, 512, 512, 512)
    row = lambda n: pl.BlockSpec((tm, n), lambda i: (i, 0))
    out_shape = [jax.ShapeDtypeStruct((t, n), BF16) for n in widths]
    out_shape += [jax.ShapeDtypeStruct((t, LANES), F32), jax.ShapeDtypeStruct((N_GATES, t), F32)]
    out_specs = [row(n) for n in widths]
    out_specs += [row(LANES), pl.BlockSpec((N_GATES, tm), lambda i: (0, i))]
    return pl.pallas_call(
        _in_proj_body, grid=(t // tm,),
        in_specs=[row(d), _full((1, d)), _full(w_main.shape), _full(w_gate.shape), _full(w_gate_t.shape)],
        out_specs=out_specs, out_shape=out_shape,
        compiler_params=_params("parallel"), name="in_proj",
    )(h, g, w_main, w_gate, w_gate_t)


def _dft_tables(n):
    k = np.arange(n, dtype=np.int64)
    ang = 2.0 * np.pi * ((k[:, None] * k[None, :]) % n).astype(np.float64) / n
    return np.cos(ang), np.sin(ang)


def _block_diag(blocks):
    h, a, b = blocks.shape
    eye = jnp.eye(h, dtype=blocks.dtype)
    return (eye[:, None, :, None] * blocks[:, :, None, :]).reshape(h * a, h * b)


def _fourier_body(scale, row_tile, z_ref, cs_ref, ss_ref, cm_ref, sm_ref, w_ref, o_ref, p_scr, q_scr):
    z = z_ref[0]
    p_scr[...] = _dot(z, cm_ref[...]).astype(BF16)
    q_scr[...] = _dot(z, sm_ref[...]).astype(BF16)
    s = z.shape[0]
    for r in range(s // row_tile):
        rs = slice(r * row_tile, (r + 1) * row_tile)
        re = (_dot(cs_ref[rs, :], p_scr[...]) - _dot(ss_ref[rs, :], q_scr[...])) * scale
        o_ref[0, rs, :] = _dot(re.astype(BF16), w_ref[...]).astype(o_ref.dtype)


def _fourier(zf, cs, ss, cm, sm, wbd):
    b, s, w = zf.shape
    row_tile = min(512, s)
    scale = 1.0 / math.sqrt(s * FOUR_HD)
    blk = pl.BlockSpec((1, s, w), lambda i: (i, 0, 0))
    return pl.pallas_call(
        functools.partial(_fourier_body, scale, row_tile), grid=(b,),
        in_specs=[blk, _full(cs.shape), _full(ss.shape), _full(cm.shape), _full(sm.shape), _full(wbd.shape)],
        out_specs=blk, out_shape=jax.ShapeDtypeStruct((b, s, w), BF16),
        scratch_shapes=[pltpu.VMEM((s, w), BF16), pltpu.VMEM((s, w), BF16)],
        compiler_params=_params("parallel"), name="fourier",
    )(zf, cs, ss, cm, sm, wbd)


def _gmlp_body(zu_ref, zv_ref, lg_ref, lb_ref, ws_ref, bs_ref, o_ref):
    s = zu_ref.shape[1]
    w = zu_ref.shape[2]
    lane = lax.broadcasted_iota(I32, (GMLP_CHUNK, w), 1)

    def chunk(c, carry):
        r0 = pl.multiple_of(c * GMLP_CHUNK, GMLP_CHUNK)
        rows = pl.ds(r0, GMLP_CHUNK)
        v = jax.nn.gelu(zv_ref[0, rows, :].astype(F32))
        vc = v - jnp.mean(v, axis=-1, keepdims=True)
        vn = vc * lax.rsqrt(jnp.mean(vc * vc, axis=-1, keepdims=True) + EPS) * lg_ref[...] + lb_ref[...]
        acc = bs_ref[...]
        for h in range(GMLP_HEADS):
            vh = jnp.where((lane >= h * GMLP_HD) & (lane < (h + 1) * GMLP_HD), vn, 0.0).astype(BF16)
            acc = acc + _dot(ws_ref[h], vh)
        u = jax.nn.gelu(zu_ref[0, rows, :].astype(F32))
        o_ref[0, rows, :] = (u * acc).astype(o_ref.dtype)
        return carry

    lax.fori_loop(0, s // GMLP_CHUNK, chunk, 0)


def _gmlp(zu, zv, ln_g, ln_b, ws, bs_full):
    b, s, w = zu.shape
    blk = pl.BlockSpec((1, s, w), lambda i: (i, 0, 0))
    return pl.pallas_call(
        _gmlp_body, grid=(b,),
        in_specs=[blk, blk, _full((1, w)), _full((1, w)), _full(ws.shape), _full(bs_full.shape)],
        out_specs=blk, out_shape=jax.ShapeDtypeStruct((b, s, w), BF16),
        compiler_params=_params("parallel"), name="gmlp",
    )(zu, zv, ln_g, ln_b, ws, bs_full)


def _mlstm_body(qkp_ref, v_ref, zo_ref, g_ref, gt_ref, cw_ref, gb_ref, gbt_ref, ng_ref, o_ref,
                qm_scr, km_scr, hf_scr, hb_scr, c_scr, n_scr, m_scr):
    L = MLSTM_CHUNK
    H = MLSTM_HEADS
    s = v_ref.shape[1]
    nc = s // L
    qkw = H * MLSTM_DQK

    ext = L + 2 * CONV_PAD
    r_i = lax.broadcasted_iota(I32, (L, ext), 0)
    c_i = lax.broadcasted_iota(I32, (L, ext), 1)
    sh_m1 = jnp.where(c_i == r_i + CONV_PAD - 1, 1.0, 0.0).astype(BF16)
    sh_0 = jnp.where(c_i == r_i + CONV_PAD, 1.0, 0.0).astype(BF16)
    sh_p1 = jnp.where(c_i == r_i + CONV_PAD + 1, 1.0, 0.0).astype(BF16)
    lane = lax.broadcasted_iota(I32, (L, LANES), 1)
    low_half = lane < MLSTM_DQK

    def conv_chunk(c, carry):
        r0 = pl.multiple_of(c * L, L)
        xe = qkp_ref[0, pl.ds(r0, ext), :]
        y = (cw_ref[0:1, :] * _dot(sh_m1, xe) + cw_ref[1:2, :] * _dot(sh_0, xe)
             + cw_ref[2:3, :] * _dot(sh_p1, xe))
        qk = y * jax.nn.sigmoid(y)
        for h in range(H):
            keep = low_half if h % 2 == 0 else jnp.logical_not(low_half)
            t0 = (h // 2) * LANES
            q_t = qk[:, t0:t0 + LANES] * (MLSTM_DQK ** -0.5)
            k_t = qk[:, qkw + t0:qkw + t0 + LANES]
            qm_scr[h, pl.ds(r0, L), :] = jnp.where(keep, q_t, 0.0).astype(BF16)
            km_scr[h, pl.ds(r0, L), :] = jnp.where(keep, k_t, 0.0).astype(BF16)
        return carry

    lax.fori_loop(0, nc, conv_chunk, 0)

    c_scr[...] = jnp.zeros(c_scr.shape, F32)
    n_scr[...] = jnp.zeros(n_scr.shape, F32)
    m_scr[...] = jnp.zeros(m_scr.shape, F32)
    ri = lax.broadcasted_iota(I32, (L, L), 0)
    ci = lax.broadcasted_iota(I32, (L, L), 1)
    tri_l = jnp.where(ci <= ri, 1.0, 0.0)
    tri_u = jnp.where(ci >= ri, 1.0, 0.0)

    def direction(d, chunk):
        r0 = pl.multiple_of(chunk * L, L)
        rows = pl.ds(r0, L)
        gc = g_ref[0, rows, :] + gb_ref[...]
        gr = gt_ref[:, rows] + gbt_ref[:, 0:1]
        lf_c = jax.nn.log_sigmoid(gc)
        lf_r = jax.nn.log_sigmoid(gr)
        if d == 0:
            cum_c = jnp.dot(tri_l, lf_c, precision=HIGHEST, preferred_element_type=F32)
            cum_r = jnp.dot(lf_r, tri_u, precision=HIGHEST, preferred_element_type=F32)
            mask = ci <= ri
        else:
            cum_c = jnp.dot(tri_u, lf_c, precision=HIGHEST, preferred_element_type=F32)
            cum_r = jnp.dot(lf_r, tri_l, precision=HIGHEST, preferred_element_type=F32)
            mask = ci >= ri
        i_lane = 2 * d * H
        f_lane = (2 * d + 1) * H
        for h in range(H):
            u = d * H + h
            b_c = cum_c[:, f_lane + h:f_lane + h + 1]
            li_c = gc[:, i_lane + h:i_lane + h + 1]
            b_r = cum_r[f_lane + h:f_lane + h + 1, :]
            li_r = gr[i_lane + h:i_lane + h + 1, :]
            g_tot = b_c[L - 1:L, :] if d == 0 else b_c[0:1, :]
            qh = qm_scr[h, rows, :]
            kh = km_scr[h, rows, :]
            vh = v_ref[0, rows, h * MLSTM_DV:(h + 1) * MLSTM_DV]
            c_prev = c_scr[u]
            n_prev = n_scr[u]
            m_prev = m_scr[u][:, 0:1]
            dmat = jnp.where(mask, b_c - b_r + li_r, -jnp.inf)
            m_intra = jnp.max(dmat, axis=-1, keepdims=True)
            m_inter = b_c + m_prev
            m_tot = jnp.maximum(m_intra, m_inter)
            qk = lax.dot_general(qh, kh, (((1,), (1,)), ((), ())), preferred_element_type=F32)
            s_mat = jnp.exp(dmat - m_tot) * qk
            inter = jnp.exp(m_inter - m_tot)
            num = _dot(s_mat.astype(BF16), vh) + inter * _dot(qh, c_prev.astype(BF16))
            qn = jnp.sum(qh.astype(F32) * n_prev, axis=-1, keepdims=True)
            den = jnp.sum(s_mat, axis=-1, keepdims=True) + inter * qn
            hh = num / jnp.maximum(jnp.abs(den), jnp.exp(-m_tot))
            dst = hf_scr if d == 0 else hb_scr
            dst[rows, h * MLSTM_DV:(h + 1) * MLSTM_DV] = hh
            a_c = g_tot - b_c + li_c
            m_loc = jnp.max(a_c, axis=0, keepdims=True)
            w_c = jnp.exp(a_c - m_loc)
            c_loc = lax.dot_general(kh, (w_c * vh.astype(F32)).astype(BF16),
                                    (((0,), (0,)), ((), ())), preferred_element_type=F32)
            n_loc = jnp.sum(w_c * kh.astype(F32), axis=0, keepdims=True)
            m_new = jnp.maximum(g_tot + m_prev, m_loc)
            a_old = jnp.exp(g_tot + m_prev - m_new)
            a_new = jnp.exp(m_loc - m_new)
            c_scr[u] = a_old * c_prev + a_new * c_loc
            n_scr[u] = a_old * n_prev + a_new * n_loc
            m_scr[u] = jnp.broadcast_to(m_new, (1, LANES))

    def step(i, carry):
        direction(0, i)
        direction(1, nc - 1 - i)
        return carry

    lax.fori_loop(0, nc, step, 0)

    def finish(c, carry):
        r0 = pl.multiple_of(c * L, L)
        rows = pl.ds(r0, L)
        og = jax.nn.sigmoid(zo_ref[0, rows, :].astype(F32))
        hsum = og * (hf_scr[rows, :] + hb_scr[rows, :])
        for h in range(H):
            cols = slice(h * MLSTM_DV, (h + 1) * MLSTM_DV)
            x = hsum[:, cols]
            xc = x - jnp.mean(x, axis=-1, keepdims=True)
            y = xc * lax.rsqrt(jnp.mean(xc * xc, axis=-1, keepdims=True) + EPS) * ng_ref[:, cols]
            o_ref[0, rows, cols] = y.astype(o_ref.dtype)
        return carry

    lax.fori_loop(0, nc, finish, 0)


def _mlstm(zqk_pad, zmv, zo, zg, zgt, conv_w, gate_b, gate_b_t, norm_g):
    b, s, w = zmv.shape
    H = MLSTM_HEADS
    blk = pl.BlockSpec((1, s, w), lambda i: (i, 0, 0))
    units = 2 * H
    return pl.pallas_call(
        _mlstm_body, grid=(b,),
        in_specs=[pl.BlockSpec((1, s + 2 * CONV_PAD, w), lambda i: (i, 0, 0)), blk, blk,
                  pl.BlockSpec((1, s, LANES), lambda i: (i, 0, 0)),
                  pl.BlockSpec((N_GATES, s), lambda i: (0, i)),
                  _full(conv_w.shape), _full(gate_b.shape), _full(gate_b_t.shape), _full(norm_g.shape)],
        out_specs=blk, out_shape=jax.ShapeDtypeStruct((b, s, w), BF16),
        scratch_shapes=[pltpu.VMEM((H, s, LANES), BF16), pltpu.VMEM((H, s, LANES), BF16),
                        pltpu.VMEM((s, w), F32), pltpu.VMEM((s, w), F32),
                        pltpu.VMEM((units, LANES, MLSTM_DV), F32), pltpu.VMEM((units, 1, LANES), F32),
                        pltpu.VMEM((units, 1, LANES), F32)],
        compiler_params=_params("parallel"), name="mlstm",
    )(zqk_pad, zmv, zo, zg, zgt, conv_w, gate_b, gate_b_t, norm_g)


def _post_mix_body(h_ref, yf_ref, yg_ref, ym_ref, wo_ref, g2_ref, rw_ref, rb_ref, sgu_ref, sd_ref,
                   hs_ref, xn3_ref, sel_ref, gate_ref, idx_ref):
    wf, wg = yf_ref.shape[-1], yg_ref.shape[-1]
    h1 = (h_ref[...] + _dot(yf_ref[...], wo_ref[0:wf, :]) + _dot(yg_ref[...], wo_ref[wf:wf + wg, :])
          + _dot(ym_ref[...], wo_ref[wf + wg:, :]))
    xn = _rms(h1, g2_ref[...])
    xb = xn.astype(BF16)
    for j in range(ROW_SUB):
        xn3_ref[:, j, :] = xn[:, j * LANES:(j + 1) * LANES]
    gu = _dot(xb, sgu_ref[...])
    de = gu.shape[-1] // 2
    act = (jax.nn.silu(gu[:, :de]) * gu[:, de:]).astype(BF16)
    hs_ref[...] = h1 + _dot(act, sd_ref[...])
    scores = jax.nn.sigmoid(jnp.dot(xn, rw_ref[...], precision=HIGHEST, preferred_element_type=F32))
    work = scores + rb_ref[...]
    lane = lax.broadcasted_iota(I32, scores.shape, 1)
    sel = jnp.zeros(scores.shape, F32)
    idx = jnp.zeros(scores.shape, I32)
    for k in range(TOP_K):
        m = jnp.max(work, axis=-1, keepdims=True)
        e = jnp.min(jnp.where(work == m, lane, N_EXPERTS), axis=-1, keepdims=True)
        hit = lane == e
        sel = jnp.where(hit, 1.0, sel)
        idx = jnp.where(lane == k, e, idx)
        work = jnp.where(hit, -jnp.inf, work)
    picked = sel * scores
    sel_ref[...] = sel
    gate_ref[...] = picked / jnp.sum(picked, axis=-1, keepdims=True)
    idx_ref[...] = idx


def _post_mix(h, yf, yg, ym, w_out, g2, router_w, router_b, sgu, sd):
    t, d = h.shape
    tm = min(TOKEN_TILE, t)
    row = lambda n: pl.BlockSpec((tm, n), lambda i: (i, 0))
    return pl.pallas_call(
        _post_mix_body, grid=(t // tm,),
        in_specs=[row(d), row(yf.shape[1]), row(yg.shape[1]), row(ym.shape[1]), _full(w_out.shape),
                  _full((1, d)), _full(router_w.shape), _full((1, N_EXPERTS)), _full(sgu.shape), _full(sd.shape)],
        out_specs=[row(d), pl.BlockSpec((tm, ROW_SUB, LANES), lambda i: (i, 0, 0)),
                   row(N_EXPERTS), row(N_EXPERTS), row(N_EXPERTS)],
        out_shape=[jax.ShapeDtypeStruct((t, d), F32), jax.ShapeDtypeStruct((t, ROW_SUB, LANES), F32),
                   jax.ShapeDtypeStruct((t, N_EXPERTS), F32), jax.ShapeDtypeStruct((t, N_EXPERTS), F32),
                   jax.ShapeDtypeStruct((t, N_EXPERTS), I32)],
        compiler_params=_params("parallel"), name="post_mix",
    )(h, yf, yg, ym, w_out, g2, router_w, router_b, sgu, sd)


def _rank_body(sel_ref, rank_ref, cnt_ref, carry):
    @pl.when(pl.program_id(0) == 0)
    def _():
        carry[...] = jnp.zeros(carry.shape, F32)

    m = sel_ref[...]
    tm = m.shape[0]
    ri = lax.broadcasted_iota(I32, (tm, tm), 0)
    ci = lax.broadcasted_iota(I32, (tm, tm), 1)
    below = jnp.where(ci < ri, 1.0, 0.0).astype(BF16)
    rank = _dot(below, m.astype(BF16)) + carry[...]
    rank_ref[...] = rank.astype(I32)
    carry[...] = carry[...] + jnp.sum(m, axis=0, keepdims=True)
    cnt_ref[...] = carry[...].astype(I32)


def _rank(sel):
    t, e = sel.shape
    tm = min(TOKEN_TILE, t)
    return pl.pallas_call(
        _rank_body, grid=(t // tm,),
        in_specs=[pl.BlockSpec((tm, e), lambda i: (i, 0))],
        out_specs=[pl.BlockSpec((tm, e), lambda i: (i, 0)), _full((1, e))],
        out_shape=[jax.ShapeDtypeStruct((t, e), I32), jax.ShapeDtypeStruct((1, e), I32)],
        scratch_shapes=[pltpu.VMEM((1, e), F32)],
        compiler_params=_params("arbitrary"), name="rank",
    )(sel)


def _dest_body(rank_ref, idx_ref, gate_ref, offs_ref, dest_ref, g8_ref):
    pos = (rank_ref[...] + offs_ref[...]).astype(F32)
    gates = gate_ref[...]
    lane = lax.broadcasted_iota(I32, pos.shape, 1)
    dest = jnp.zeros(pos.shape, F32)
    g8 = jnp.zeros(pos.shape, F32)
    for k in range(TOP_K):
        hit = lane == idx_ref[:, k:k + 1]
        dk = jnp.sum(jnp.where(hit, pos, 0.0), axis=-1, keepdims=True)
        gk = jnp.sum(jnp.where(hit, gates, 0.0), axis=-1, keepdims=True)
        dest = jnp.where(lane == k, dk, dest)
        g8 = jnp.where(lane == k, gk, g8)
    dest_ref[...] = dest.astype(I32)
    g8_ref[...] = g8


def _dest(rank, idx, gates, offs):
    t, e = rank.shape
    tm = min(TOKEN_TILE, t)
    row = pl.BlockSpec((tm, e), lambda i: (i, 0))
    return pl.pallas_call(
        _dest_body, grid=(t // tm,),
        in_specs=[row, row, row, _full((1, e))],
        out_specs=[row, row],
        out_shape=[jax.ShapeDtypeStruct((t, e), I32), jax.ShapeDtypeStruct((t, e), F32)],
        compiler_params=_params("parallel"), name="dest",
    )(rank, idx, gates, offs)


def _dispatch_body(dest_ref, x_hbm, xs_init_hbm, xs_hbm, sem):
    del xs_init_hbm
    n_tok = dest_ref.shape[0] // TOP_K
    t0 = pl.program_id(0) * n_tok

    def row_copy(src_row, dst_row):
        return pltpu.make_async_copy(x_hbm.at[src_row], xs_hbm.at[dst_row], sem)

    def issue(i, carry):
        for k in range(TOP_K):
            row_copy(t0 + i, dest_ref[i * TOP_K + k]).start()
        return carry

    lax.fori_loop(0, n_tok, issue, 0)

    def drain(i, carry):
        for k in range(TOP_K):
            row_copy(0, 0).wait()
        return carry

    lax.fori_loop(0, n_tok, drain, 0)


def _dispatch(dest_flat, xn3, xs_init):
    t = xn3.shape[0]
    tt = min(DISPATCH_TILE, t)
    return pl.pallas_call(
        _dispatch_body, grid=(t // tt,),
        in_specs=[pl.BlockSpec((tt * TOP_K,), lambda i: (i,), memory_space=pltpu.SMEM),
                  pl.BlockSpec(memory_space=pl.ANY), pl.BlockSpec(memory_space=pl.ANY)],
        out_specs=pl.BlockSpec(memory_space=pl.ANY),
        out_shape=jax.ShapeDtypeStruct(xs_init.shape, xs_init.dtype),
        scratch_shapes=[pltpu.SemaphoreType.DMA(())],
        input_output_aliases={2: 0},
        compiler_params=_params("arbitrary"), name="dispatch",
    )(dest_flat, xn3, xs_init)


def _experts_body(blk_e_ref, n_used_ref, xs_ref, wg_ref, wu_ref, wd_ref, ys_ref):
    del blk_e_ref
    i = pl.program_id(0)

    @pl.when(i < n_used_ref[0])
    def _():
        x = jnp.concatenate([xs_ref[:, j, :] for j in range(ROW_SUB)], axis=-1).astype(BF16)
        g = _dot(x, wg_ref[0].astype(BF16))
        u = _dot(x, wu_ref[0].astype(BF16))
        a = (jax.nn.silu(g) * u).astype(BF16)
        y = _dot(a, wd_ref[0].astype(BF16))
        for j in range(ROW_SUB):
            ys_ref[:, j, :] = y[:, j * LANES:(j + 1) * LANES]

    @pl.when(i >= n_used_ref[0])
    def _():
        ys_ref[...] = jnp.zeros(ys_ref.shape, F32)


def _experts(blk_e, n_used, xs, wg, wu, wd):
    p = xs.shape[0]
    nb = p // EXPERT_BLOCK
    d, de = wg.shape[1], wg.shape[2]
    rows = pl.BlockSpec((EXPERT_BLOCK, ROW_SUB, LANES), lambda i, be, nu: (i, 0, 0))
    grid_spec = pltpu.PrefetchScalarGridSpec(
        num_scalar_prefetch=2, grid=(nb,),
        in_specs=[rows,
                  pl.BlockSpec((1, d, de), lambda i, be, nu: (be[i], 0, 0)),
                  pl.BlockSpec((1, d, de), lambda i, be, nu: (be[i], 0, 0)),
                  pl.BlockSpec((1, de, d), lambda i, be, nu: (be[i], 0, 0))],
        out_specs=rows)
    return pl.pallas_call(
        _experts_body, grid_spec=grid_spec,
        out_shape=jax.ShapeDtypeStruct(xs.shape, F32),
        compiler_params=_params("arbitrary"), name="experts",
    )(blk_e, n_used, xs, wg, wu, wd)


def _combine_body(dest_ref, gate_ref, ys_hbm, o_ref, buf, sem):
    n_pair = dest_ref.shape[0]
    n_tok = n_pair // TOP_K

    def row_copy(src_row, slot):
        return pltpu.make_async_copy(ys_hbm.at[src_row], buf.at[slot], sem)

    def issue(i, carry):
        for k in range(TOP_K):
            p = i * TOP_K + k
            row_copy(dest_ref[p], p).start()
        return carry

    lax.fori_loop(0, n_tok, issue, 0)

    def drain(i, carry):
        for k in range(TOP_K):
            row_copy(0, 0).wait()
        return carry

    lax.fori_loop(0, n_tok, drain, 0)

    def reduce(i, carry):
        acc = gate_ref[i * TOP_K] * buf[i * TOP_K]
        for k in range(1, TOP_K):
            acc = acc + gate_ref[i * TOP_K + k] * buf[i * TOP_K + k]
        o_ref[i] = acc
        return carry

    lax.fori_loop(0, n_tok, reduce, 0)


def _combine(dest_flat, gate_flat, ys):
    t = dest_flat.shape[0] // TOP_K
    tt = min(COMBINE_TILE, t)
    smem = lambda: pl.BlockSpec((tt * TOP_K,), lambda i: (i,), memory_space=pltpu.SMEM)
    return pl.pallas_call(
        _combine_body, grid=(t // tt,),
        in_specs=[smem(), smem(), pl.BlockSpec(memory_space=pl.ANY)],
        out_specs=pl.BlockSpec((tt, ROW_SUB, LANES), lambda i: (i, 0, 0)),
        out_shape=jax.ShapeDtypeStruct((t, ROW_SUB, LANES), F32),
        scratch_shapes=[pltpu.VMEM((tt * TOP_K, ROW_SUB, LANES), F32), pltpu.SemaphoreType.DMA(())],
        compiler_params=_params("arbitrary"), name="combine",
    )(dest_flat, gate_flat, ys)


def _ple_body(final, hs_ref, moe_ref, p_ref, win_ref, wgate_ref, g_ref, gf_ref, o_ref):
    moe = jnp.concatenate([moe_ref[:, j, :] for j in range(ROW_SUB)], axis=-1)
    h2 = hs_ref[...] + moe
    e = _dot(p_ref[...].astype(BF16), win_ref[...])
    gate = jax.nn.sigmoid(_dot(h2.astype(BF16), wgate_ref[...]))
    h3 = h2 + _rms(gate * e, g_ref[...])
    o_ref[...] = _rms(h3, gf_ref[...]) if final else h3


def _ple(hs, moe3, p, w_in, w_gate, g, g_final, final):
    t, d = hs.shape
    tm = min(TOKEN_TILE, t)
    row = lambda n: pl.BlockSpec((tm, n), lambda i: (i, 0))
    return pl.pallas_call(
        functools.partial(_ple_body, final), grid=(t // tm,),
        in_specs=[row(d), pl.BlockSpec((tm, ROW_SUB, LANES), lambda i: (i, 0, 0)), row(p.shape[1]),
                  _full(w_in.shape), _full(w_gate.shape), _full((1, d)), _full((1, d))],
        out_specs=row(d), out_shape=jax.ShapeDtypeStruct((t, d), F32),
        compiler_params=_params("parallel"), name="ple",
    )(hs, moe3, p, w_in, w_gate, g, g_final)


def _mixers(h, bsz, seq, norm1_g, w_in, four_w, ln_g, ln_b, ws, bs, conv_w, gate_b, norm_g, tables):
    d = h.shape[1]
    n_main = w_in.shape[1] - N_GATES
    w_main = w_in[:, :n_main].astype(BF16)
    w_g = w_in[:, n_main:]
    w_gate = jnp.pad(w_g, ((0, 0), (0, LANES - N_GATES))).astype(BF16)
    w_gate_t = w_g.T.astype(BF16)
    zf, zu, zv, zqk, zmv, zo, zg, zgt = _in_proj(h, norm1_g.reshape(1, d), w_main, w_gate, w_gate_t)
    b3 = lambda a: a.reshape(bsz, seq, a.shape[-1])

    cs, ss, cm, sm = tables
    yf = _fourier(b3(zf), cs, ss, cm, sm, _block_diag(four_w).astype(BF16))

    bs_full = jnp.repeat(bs.T, GMLP_HD, axis=1)
    yg = _gmlp(b3(zu), b3(zv), ln_g.reshape(1, -1), ln_b.reshape(1, -1), ws.astype(BF16), bs_full)

    zqk_pad = jnp.pad(b3(zqk), ((0, 0), (CONV_PAD, CONV_PAD), (0, 0)))
    gate_b_c = jnp.pad(gate_b, (0, LANES - N_GATES)).reshape(1, LANES)
    gate_b_t = jnp.broadcast_to(gate_b.reshape(N_GATES, 1), (N_GATES, LANES))
    ym = _mlstm(zqk_pad, b3(zmv), b3(zo), b3(zg), zgt, conv_w, gate_b_c, gate_b_t, norm_g.reshape(1, -1))
    t = bsz * seq
    return yf.reshape(t, -1), yg.reshape(t, -1), ym.reshape(t, -1)


def _moe(h, yf, yg, ym, w_out, norm2_g, router_w, router_b, wg, wu, wd, sg, su, sd):
    t, d = h.shape
    sgu = jnp.concatenate([sg, su], axis=1).astype(BF16)
    hs, xn3, sel, gates, idx = _post_mix(h, yf, yg, ym, w_out.astype(BF16), norm2_g.reshape(1, d),
                                         router_w, router_b.reshape(1, -1), sgu, sd.astype(BF16))
    rank, counts = _rank(sel)
    counts = counts[0]
    pcounts = ((counts + EXPERT_BLOCK - 1) // EXPERT_BLOCK) * EXPERT_BLOCK
    pend = jnp.cumsum(pcounts)
    offs = (pend - pcounts).astype(I32)
    n_blk = (t * TOP_K) // EXPERT_BLOCK + N_EXPERTS
    blk_start = jnp.arange(n_blk, dtype=I32) * EXPERT_BLOCK
    blk_e = jnp.minimum(jnp.searchsorted(pend, blk_start, side="right"), N_EXPERTS - 1).astype(I32)
    n_used = (pend[-1:] // EXPERT_BLOCK).astype(I32)

    dest, g8 = _dest(rank, idx, gates, offs.reshape(1, -1))
    dest_flat = dest[:, :TOP_K].reshape(-1)
    gate_flat = g8[:, :TOP_K].reshape(-1)
    xs = _dispatch(dest_flat, xn3, jnp.zeros((n_blk * EXPERT_BLOCK, ROW_SUB, LANES), F32))
    ys = _experts(blk_e, n_used, xs, wg, wu, wd)
    moe3 = _combine(dest_flat, gate_flat, ys)
    return hs, moe3


def kernel(x, p, norm1_g, w_in, four_w, gmlp_ln_g, gmlp_ln_b, gmlp_ws, gmlp_bs, mlstm_conv_w,
           mlstm_gate_b, mlstm_norm_g, w_out, norm2_g, router_w, router_b, exp_w_gate, exp_w_up,
           exp_w_down, sh_w_gate, sh_w_up, sh_w_down, ple_w_in, ple_w_gate, ple_norm_g, final_norm_g):
    bsz, seq, d = x.shape
    depth = w_in.shape[0]
    t = bsz * seq
    cos_s, sin_s = _dft_tables(seq)
    cos_m, sin_m = _dft_tables(FOUR_HD)
    eye = np.eye(FOUR_HEADS)
    tables = (jnp.asarray(cos_s, BF16), jnp.asarray(sin_s, BF16),
              jnp.asarray(np.kron(eye, cos_m), BF16), jnp.asarray(np.kron(eye, sin_m), BF16))
    h = x.reshape(t, d)
    for i in range(depth):
        yf, yg, ym = _mixers(h, bsz, seq, norm1_g[i], w_in[i], four_w[i], gmlp_ln_g[i], gmlp_ln_b[i],
                             gmlp_ws[i], gmlp_bs[i], mlstm_conv_w[i], mlstm_gate_b[i], mlstm_norm_g[i], tables)
        hs, moe3 = _moe(h, yf, yg, ym, w_out[i], norm2_g[i], router_w[i], router_b[i],
                        exp_w_gate[i], exp_w_up[i], exp_w_down[i], sh_w_gate[i], sh_w_up[i], sh_w_down[i])
        h = _ple(hs, moe3, p[i].reshape(t, -1), ple_w_in[i].astype(BF16), ple_w_gate[i].astype(BF16),
                 ple_norm_g[i].reshape(1, d), final_norm_g.reshape(1, d), i == depth - 1)
    return h.reshape(bsz, seq, d)
```

```python
import functools
import math

import numpy as np
import jax
import jax.numpy as jnp
from jax import lax
from jax.experimental import pallas as pl
from jax.experimental.pallas import tpu as pltpu

F32, BF16, I32 = jnp.float32, jnp.bfloat16, jnp.int32
HIGHEST = lax.Precision.HIGHEST

EPS = 1e-6
LANES = 128
VMEM_LIMIT_BYTES = 48 * 1024 * 1024

FOUR_HEADS, FOUR_HD = 4, 64
GMLP_HEADS, GMLP_HD, GMLP_CHUNK = 4, 64, 128
MLSTM_HEADS, MLSTM_DV, MLSTM_DQK, MLSTM_CHUNK = 4, 128, 64, 128
CONV_PAD = 16
N_GATES = 16
N_EXPERTS, TOP_K = 128, 8
EXPERT_BLOCK = 256
TOKEN_TILE = 512
DISPATCH_TILE = 256
COMBINE_TILE = 128


def _params(*sem):
    return pltpu.CompilerParams(dimension_semantics=sem, vmem_limit_bytes=VMEM_LIMIT_BYTES)


def _rms(x, g):
    return x * lax.rsqrt(jnp.mean(x * x, axis=-1, keepdims=True) + EPS) * g


def _full(shape):
    nd = len(shape)
    return pl.BlockSpec(shape, lambda *_: (0,) * nd)


def _dot(a, b):
    return jnp.dot(a, b, preferred_element_type=F32)


def _in_proj_body(h_ref, g_ref, w_ref, wg_ref, wgt_ref,
                  zf_ref, zu_ref, zv_ref, zqk_ref, zmv_ref, zo_ref, zg_ref, zgt_ref):
    ab = _rms(h_ref[...], g_ref[...]).astype(BF16)
    off = 0
    for o_ref in (zf_ref, zu_ref, zv_ref, zqk_ref, zmv_ref, zo_ref):
        n = o_ref.shape[-1]
        o_ref[...] = _dot(ab, w_ref[:, off:off + n]).astype(o_ref.dtype)
        off += n
    zg_ref[...] = _dot(ab, wg_ref[...])
    zgt_ref[...] = lax.dot_general(wgt_ref[...], ab, (((1,), (1,)), ((), ())),
                                   preferred_element_type=F32)


def _in_proj(h, g, w_main, w_gate, w_gate_t):
    t, d = h.shape
    tm = min(TOKEN_TILE, t)
    widths = (256, 256, 256, 512, 512, 512)
    row = lambda n: pl.BlockSpec((tm, n), lambda i: (i, 0))
    out_shape = [jax.ShapeDtypeStruct((t, n), BF16) for n in widths]
    out_shape += [jax.ShapeDtypeStruct((t, LANES), F32), jax.ShapeDtypeStruct((N_GATES, t), F32)]
    out_specs = [row(n) for n in widths]
    out_specs += [row(LANES), pl.BlockSpec((N_GATES, tm), lambda i: (0, i))]
    return pl.pallas_call(
        _in_proj_body, grid=(t // tm,),
        in_specs=[row(d), _full((1, d)), _full(w_main.shape), _full(w_gate.shape), _full(w_gate_t.shape)],
        out_specs=out_specs, out_shape=out_shape,
        compiler_params=_params("parallel"), name="in_proj",
    )(h, g, w_main, w_gate, w_gate_t)


def _dft_tables(n):
    k = np.arange(n, dtype=np.int64)
    ang = 2.0 * np.pi * ((k[:, None] * k[None, :]) % n).astype(np.float64) / n
    return np.cos(ang), np.sin(ang)


def _block_diag(blocks):
    h, a, b = blocks.shape
    eye = jnp.eye(h, dtype=blocks.dtype)
    return (eye[:, None, :, None] * blocks[:, :, None, :]).reshape(h * a, h * b)


def _fourier_body(scale, row_tile, z_ref, cs_ref, ss_ref, cm_ref, sm_ref, w_ref, o_ref, p_scr, q_scr):
    z = z_ref[0]
    p_scr[...] = _dot(z, cm_ref[...]).astype(BF16)
    q_scr[...] = _dot(z, sm_ref[...]).astype(BF16)
    s = z.shape[0]
    for r in range(s // row_tile):
        rs = slice(r * row_tile, (r + 1) * row_tile)
        re = (_dot(cs_ref[rs, :], p_scr[...]) - _dot(ss_ref[rs, :], q_scr[...])) * scale
        o_ref[0, rs, :] = _dot(re.astype(BF16), w_ref[...]).astype(o_ref.dtype)


def _fourier(zf, cs, ss, cm, sm, wbd):
    b, s, w = zf.shape
    row_tile = min(512, s)
    scale = 1.0 / math.sqrt(s * FOUR_HD)
    blk = pl.BlockSpec((1, s, w), lambda i: (i, 0, 0))
    return pl.pallas_call(
        functools.partial(_fourier_body, scale, row_tile), grid=(b,),
        in_specs=[blk, _full(cs.shape), _full(ss.shape), _full(cm.shape), _full(sm.shape), _full(wbd.shape)],
        out_specs=blk, out_shape=jax.ShapeDtypeStruct((b, s, w), BF16),
        scratch_shapes=[pltpu.VMEM((s, w), BF16), pltpu.VMEM((s, w), BF16)],
        compiler_params=_params("parallel"), name="fourier",
    )(zf, cs, ss, cm, sm, wbd)


def _gmlp_body(zu_ref, zv_ref, lg_ref, lb_ref, ws_ref, bs_ref, o_ref):
    s = zu_ref.shape[1]
    w = zu_ref.shape[2]
    lane = lax.broadcasted_iota(I32, (GMLP_CHUNK, w), 1)

    def chunk(c, carry):
        r0 = pl.multiple_of(c * GMLP_CHUNK, GMLP_CHUNK)
        rows = pl.ds(r0, GMLP_CHUNK)
        v = jax.nn.gelu(zv_ref[0, rows, :].astype(F32))
        vc = v - jnp.mean(v, axis=-1, keepdims=True)
        vn = vc * lax.rsqrt(jnp.mean(vc * vc, axis=-1, keepdims=True) + EPS) * lg_ref[...] + lb_ref[...]
        acc = bs_ref[...]
        for h in range(GMLP_HEADS):
            vh = jnp.where((lane >= h * GMLP_HD) & (lane < (h + 1) * GMLP_HD), vn, 0.0).astype(BF16)
            acc = acc + _dot(ws_ref[h], vh)
        u = jax.nn.gelu(zu_ref[0, rows, :].astype(F32))
        o_ref[0, rows, :] = (u * acc).astype(o_ref.dtype)
        return carry

    lax.fori_loop(0, s // GMLP_CHUNK, chunk, 0)


def _gmlp(zu, zv, ln_g, ln_b, ws, bs_full):
    b, s, w = zu.shape
    blk = pl.BlockSpec((1, s, w), lambda i: (i, 0, 0))
    return pl.pallas_call(
        _gmlp_body, grid=(b,),
        in_specs=[blk, blk, _full((1, w)), _full((1, w)), _full(ws.shape), _full(bs_full.shape)],
        out_specs=blk, out_shape=jax.ShapeDtypeStruct((b, s, w), BF16),
        compiler_params=_params("parallel"), name="gmlp",
    )(zu, zv, ln_g, ln_b, ws, bs_full)


def _mlstm_body(qkp_ref, v_ref, zo_ref, g_ref, gt_ref, cw_ref, gb_ref, gbt_ref, ng_ref, o_ref,
                qm_scr, km_scr, hf_scr, hb_scr, c_scr, n_scr, m_scr):
    L = MLSTM_CHUNK
    H = MLSTM_HEADS
    s = v_ref.shape[1]
    nc = s // L
    qkw = H * MLSTM_DQK

    ext = L + 2 * CONV_PAD
    r_i = lax.broadcasted_iota(I32, (L, ext), 0)
    c_i = lax.broadcasted_iota(I32, (L, ext), 1)
    sh_m1 = jnp.where(c_i == r_i + CONV_PAD - 1, 1.0, 0.0).astype(BF16)
    sh_0 = jnp.where(c_i == r_i + CONV_PAD, 1.0, 0.0).astype(BF16)
    sh_p1 = jnp.where(c_i == r_i + CONV_PAD + 1, 1.0, 0.0).astype(BF16)
    lane = lax.broadcasted_iota(I32, (L, LANES), 1)
    low_half = lane < MLSTM_DQK

    def conv_chunk(c, carry):
        r0 = pl.multiple_of(c * L, L)
        xe = qkp_ref[0, pl.ds(r0, ext), :]
        y = (cw_ref[0:1, :] * _dot(sh_m1, xe) + cw_ref[1:2, :] * _dot(sh_0, xe)
             + cw_ref[2:3, :] * _dot(sh_p1, xe))
        qk = y * jax.nn.sigmoid(y)
        for h in range(H):
            keep = low_half if h % 2 == 0 else jnp.logical_not(low_half)
            t0 = (h // 2) * LANES
            q_t = qk[:, t0:t0 + LANES] * (MLSTM_DQK ** -0.5)
            k_t = qk[:, qkw + t0:qkw + t0 + LANES]
            qm_scr[h, pl.ds(r0, L), :] = jnp.where(keep, q_t, 0.0).astype(BF16)
            km_scr[h, pl.ds(r0, L), :] = jnp.where(keep, k_t, 0.0).astype(BF16)
        return carry

    lax.fori_loop(0, nc, conv_chunk, 0)

    c_scr[...] = jnp.zeros(c_scr.shape, F32)
    n_scr[...] = jnp.zeros(n_scr.shape, F32)
    m_scr[...] = jnp.zeros(m_scr.shape, F32)
    ri = lax.broadcasted_iota(I32, (L, L), 0)
    ci = lax.broadcasted_iota(I32, (L, L), 1)
    tri_l = jnp.where(ci <= ri, 1.0, 0.0)
    tri_u = jnp.where(ci >= ri, 1.0, 0.0)

    def direction(d, chunk):
        r0 = pl.multiple_of(chunk * L, L)
        rows = pl.ds(r0, L)
        gc = g_ref[0, rows, :] + gb_ref[...]
        gr = gt_ref[:, rows] + gbt_ref[:, 0:1]
        lf_c = jax.nn.log_sigmoid(gc)
        lf_r = jax.nn.log_sigmoid(gr)
        if d == 0:
            cum_c = jnp.dot(tri_l, lf_c, precision=HIGHEST, preferred_element_type=F32)
            cum_r = jnp.dot(lf_r, tri_u, precision=HIGHEST, preferred_element_type=F32)
            mask = ci <= ri
        else:
            cum_c = jnp.dot(tri_u, lf_c, precision=HIGHEST, preferred_element_type=F32)
            cum_r = jnp.dot(lf_r, tri_l, precision=HIGHEST, preferred_element_type=F32)
            mask = ci >= ri
        i_lane = 2 * d * H
        f_lane = (2 * d + 1) * H
        for h in range(H):
            u = d * H + h
            b_c = cum_c[:, f_lane + h:f_lane + h + 1]
            li_c = gc[:, i_lane + h:i_lane + h + 1]
            b_r = cum_r[f_lane + h:f_lane + h + 1, :]
            li_r = gr[i_lane + h:i_lane + h + 1, :]
            g_tot = b_c[L - 1:L, :] if d == 0 else b_c[0:1, :]
            qh = qm_scr[h, rows, :]
            kh = km_scr[h, rows, :]
            vh = v_ref[0, rows, h * MLSTM_DV:(h + 1) * MLSTM_DV]
            c_prev = c_scr[u]
            n_prev = n_scr[u]
            m_prev = m_scr[u][:, 0:1]
            dmat = jnp.where(mask, b_c - b_r + li_r, -jnp.inf)
            m_intra = jnp.max(dmat, axis=-1, keepdims=True)
            m_inter = b_c + m_prev
            m_tot = jnp.maximum(m_intra, m_inter)
            qk = lax.dot_general(qh, kh, (((1,), (1,)), ((), ())), preferred_element_type=F32)
            s_mat = jnp.exp(dmat - m_tot) * qk
            inter = jnp.exp(m_inter - m_tot)
            num = _dot(s_mat.astype(BF16), vh) + inter * _dot(qh, c_prev.astype(BF16))
            qn = jnp.sum(qh.astype(F32) * n_prev, axis=-1, keepdims=True)
            den = jnp.sum(s_mat, axis=-1, keepdims=True) + inter * qn
            hh = num / jnp.maximum(jnp.abs(den), jnp.exp(-m_tot))
            dst = hf_scr if d == 0 else hb_scr
            dst[rows, h * MLSTM_DV:(h + 1) * MLSTM_DV] = hh
            a_c = g_tot - b_c + li_c
            m_loc = jnp.max(a_c, axis=0, keepdims=True)
            w_c = jnp.exp(a_c - m_loc)
            c_loc = lax.dot_general(kh, (w_c * vh.astype(F32)).astype(BF16),
                                    (((0,), (0,)), ((), ())), preferred_element_type=F32)
            n_loc = jnp.sum(w_c * kh.astype(F32), axis=0, keepdims=True)
            m_new = jnp.maximum(g_tot + m_prev, m_loc)
            a_old = jnp.exp(g_tot + m_prev - m_new)
            a_new = jnp.exp(m_loc - m_new)
            c_scr[u] = a_old * c_prev + a_new * c_loc
            n_scr[u] = a_old * n_prev + a_new * n_loc
            m_scr[u] = jnp.broadcast_to(m_new, (1, LANES))

    def step(i, carry):
        direction(0, i)
        direction(1, nc - 1 - i)
        return carry

    lax.fori_loop(0, nc, step, 0)

    def finish(c, carry):
        r0 = pl.multiple_of(c * L, L)
        rows = pl.ds(r0, L)
        og = jax.nn.sigmoid(zo_ref[0, rows, :].astype(F32))
        hsum = og * (hf_scr[rows, :] + hb_scr[rows, :])
        for h in range(H):
            cols = slice(h * MLSTM_DV, (h + 1) * MLSTM_DV)
            x = hsum[:, cols]
            xc = x - jnp.mean(x, axis=-1, keepdims=True)
            y = xc * lax.rsqrt(jnp.mean(xc * xc, axis=-1, keepdims=True) + EPS) * ng_ref[:, cols]
            o_ref[0, rows, cols] = y.astype(o_ref.dtype)
        return carry

    lax.fori_loop(0, nc, finish, 0)


def _mlstm(zqk_pad, zmv, zo, zg, zgt, conv_w, gate_b, gate_b_t, norm_g):
    b, s, w = zmv.shape
    H = MLSTM_HEADS
    blk = pl.BlockSpec((1, s, w), lambda i: (i, 0, 0))
    units = 2 * H
    return pl.pallas_call(
        _mlstm_body, grid=(b,),
        in_specs=[pl.BlockSpec((1, s + 2 * CONV_PAD, w), lambda i: (i, 0, 0)), blk, blk,
                  pl.BlockSpec((1, s, LANES), lambda i: (i, 0, 0)),
                  pl.BlockSpec((N_GATES, s), lambda i: (0, i)),
                  _full(conv_w.shape), _full(gate_b.shape), _full(gate_b_t.shape), _full(norm_g.shape)],
        out_specs=blk, out_shape=jax.ShapeDtypeStruct((b, s, w), BF16),
        scratch_shapes=[pltpu.VMEM((H, s, LANES), BF16), pltpu.VMEM((H, s, LANES), BF16),
                        pltpu.VMEM((s, w), F32), pltpu.VMEM((s, w), F32),
                        pltpu.VMEM((units, LANES, MLSTM_DV), F32), pltpu.VMEM((units, 1, LANES), F32),
                        pltpu.VMEM((units, 1, LANES), F32)],
        compiler_params=_params("parallel"), name="mlstm",
    )(zqk_pad, zmv, zo, zg, zgt, conv_w, gate_b, gate_b_t, norm_g)


def _post_mix_body(h_ref, yf_ref, yg_ref, ym_ref, wo_ref, g2_ref, rw_ref, rb_ref, sgu_ref, sd_ref,
                   hs_ref, xn_ref, sel_ref, gate_ref, idx_ref):
    wf, wg = yf_ref.shape[-1], yg_ref.shape[-1]
    h1 = (h_ref[...] + _dot(yf_ref[...], wo_ref[0:wf, :]) + _dot(yg_ref[...], wo_ref[wf:wf + wg, :])
          + _dot(ym_ref[...], wo_ref[wf + wg:, :]))
    xn = _rms(h1, g2_ref[...])
    xb = xn.astype(BF16)
    xn_ref[...] = xn
    gu = _dot(xb, sgu_ref[...])
    de = gu.shape[-1] // 2
    act = (jax.nn.silu(gu[:, :de]) * gu[:, de:]).astype(BF16)
    hs_ref[...] = h1 + _dot(act, sd_ref[...])
    scores = jax.nn.sigmoid(jnp.dot(xn, rw_ref[...], precision=HIGHEST, preferred_element_type=F32))
    work = scores + rb_ref[...]
    lane = lax.broadcasted_iota(I32, scores.shape, 1)
    sel = jnp.zeros(scores.shape, F32)
    idx = jnp.zeros(scores.shape, I32)
    for k in range(TOP_K):
        m = jnp.max(work, axis=-1, keepdims=True)
        e = jnp.min(jnp.where(work == m, lane, N_EXPERTS), axis=-1, keepdims=True)
        hit = lane == e
        sel = jnp.where(hit, 1.0, sel)
        idx = jnp.where(lane == k, e, idx)
        work = jnp.where(hit, -jnp.inf, work)
    picked = sel * scores
    sel_ref[...] = sel
    gate_ref[...] = picked / jnp.sum(picked, axis=-1, keepdims=True)
    idx_ref[...] = idx


def _post_mix(h, yf, yg, ym, w_out, g2, router_w, router_b, sgu, sd):
    t, d = h.shape
    tm = min(TOKEN_TILE, t)
    row = lambda n: pl.BlockSpec((tm, n), lambda i: (i, 0))
    return pl.pallas_call(
        _post_mix_body, grid=(t // tm,),
        in_specs=[row(d), row(yf.shape[1]), row(yg.shape[1]), row(ym.shape[1]), _full(w_out.shape),
                  _full((1, d)), _full(router_w.shape), _full((1, N_EXPERTS)), _full(sgu.shape), _full(sd.shape)],
        out_specs=[row(d), row(d), row(N_EXPERTS), row(N_EXPERTS), row(N_EXPERTS)],
        out_shape=[jax.ShapeDtypeStruct((t, d), F32), jax.ShapeDtypeStruct((t, d), F32),
                   jax.ShapeDtypeStruct((t, N_EXPERTS), F32), jax.ShapeDtypeStruct((t, N_EXPERTS), F32),
                   jax.ShapeDtypeStruct((t, N_EXPERTS), I32)],
        compiler_params=_params("parallel"), name="post_mix",
    )(h, yf, yg, ym, w_out, g2, router_w, router_b, sgu, sd)


def _rank_body(sel_ref, rank_ref, cnt_ref, carry):
    @pl.when(pl.program_id(0) == 0)
    def _():
        carry[...] = jnp.zeros(carry.shape, F32)

    m = sel_ref[...]
    tm = m.shape[0]
    ri = lax.broadcasted_iota(I32, (tm, tm), 0)
    ci = lax.broadcasted_iota(I32, (tm, tm), 1)
    below = jnp.where(ci < ri, 1.0, 0.0).astype(BF16)
    rank = _dot(below, m.astype(BF16)) + carry[...]
    rank_ref[...] = rank.astype(I32)
    carry[...] = carry[...] + jnp.sum(m, axis=0, keepdims=True)
    cnt_ref[...] = carry[...].astype(I32)


def _rank(sel):
    t, e = sel.shape
    tm = min(TOKEN_TILE, t)
    return pl.pallas_call(
        _rank_body, grid=(t // tm,),
        in_specs=[pl.BlockSpec((tm, e), lambda i: (i, 0))],
        out_specs=[pl.BlockSpec((tm, e), lambda i: (i, 0)), _full((1, e))],
        out_shape=[jax.ShapeDtypeStruct((t, e), I32), jax.ShapeDtypeStruct((1, e), I32)],
        scratch_shapes=[pltpu.VMEM((1, e), F32)],
        compiler_params=_params("arbitrary"), name="rank",
    )(sel)


def _dest_body(rank_ref, idx_ref, gate_ref, offs_ref, dest_ref, g8_ref):
    pos = (rank_ref[...] + offs_ref[...]).astype(F32)
    gates = gate_ref[...]
    lane = lax.broadcasted_iota(I32, pos.shape, 1)
    dest = jnp.zeros(pos.shape, F32)
    g8 = jnp.zeros(pos.shape, F32)
    for k in range(TOP_K):
        hit = lane == idx_ref[:, k:k + 1]
        dk = jnp.sum(jnp.where(hit, pos, 0.0), axis=-1, keepdims=True)
        gk = jnp.sum(jnp.where(hit, gates, 0.0), axis=-1, keepdims=True)
        dest = jnp.where(lane == k, dk, dest)
        g8 = jnp.where(lane == k, gk, g8)
    dest_ref[...] = dest.astype(I32)
    g8_ref[...] = g8


def _dest(rank, idx, gates, offs):
    t, e = rank.shape
    tm = min(TOKEN_TILE, t)
    row = pl.BlockSpec((tm, e), lambda i: (i, 0))
    return pl.pallas_call(
        _dest_body, grid=(t // tm,),
        in_specs=[row, row, row, _full((1, e))],
        out_specs=[row, row],
        out_shape=[jax.ShapeDtypeStruct((t, e), I32), jax.ShapeDtypeStruct((t, e), F32)],
        compiler_params=_params("parallel"), name="dest",
    )(rank, idx, gates, offs)


def _dispatch_body(dest_ref, tail_ref, x_ref, xs_hbm, zero_scr, sem):
    n_tok = x_ref.shape[0]

    @pl.when(pl.program_id(0) == 0)
    def _():
        zero_scr[...] = jnp.zeros(zero_scr.shape, F32)

        def tail_copy(e):
            start = pl.multiple_of(jnp.maximum(tail_ref[e], 0), EXPERT_BLOCK)
            return pltpu.make_async_copy(zero_scr, xs_hbm.at[pl.ds(start, EXPERT_BLOCK), :], sem)

        def clear(e, carry):
            @pl.when(tail_ref[e] >= 0)
            def _():
                tail_copy(e).start()
            return carry

        def clear_done(e, carry):
            @pl.when(tail_ref[e] >= 0)
            def _():
                tail_copy(e).wait()
            return carry

        lax.fori_loop(0, N_EXPERTS, clear, 0)
        lax.fori_loop(0, N_EXPERTS, clear_done, 0)

    def row_copy(src_row, dst_row):
        return pltpu.make_async_copy(x_ref.at[pl.ds(src_row, 1), :], xs_hbm.at[pl.ds(dst_row, 1), :], sem)

    def issue(i, carry):
        for k in range(TOP_K):
            row_copy(i, dest_ref[i * TOP_K + k]).start()
        return carry

    lax.fori_loop(0, n_tok, issue, 0)

    def drain(i, carry):
        for k in range(TOP_K):
            row_copy(0, 0).wait()
        return carry

    lax.fori_loop(0, n_tok, drain, 0)


def _dispatch(dest_flat, tail_start, xn, n_rows):
    t, d = xn.shape
    tt = min(DISPATCH_TILE, t)
    return pl.pallas_call(
        _dispatch_body, grid=(t // tt,),
        in_specs=[pl.BlockSpec((tt * TOP_K,), lambda i: (i,), memory_space=pltpu.SMEM),
                  pl.BlockSpec(memory_space=pltpu.SMEM),
                  pl.BlockSpec((tt, d), lambda i: (i, 0))],
        out_specs=pl.BlockSpec(memory_space=pl.ANY),
        out_shape=jax.ShapeDtypeStruct((n_rows, d), F32),
        scratch_shapes=[pltpu.VMEM((EXPERT_BLOCK, d), F32), pltpu.SemaphoreType.DMA(())],
        compiler_params=_params("arbitrary"), name="dispatch",
    )(dest_flat, tail_start, xn)


def _experts_body(blk_e_ref, n_used_ref, xs_ref, wg_ref, wu_ref, wd_ref, ys_ref, wg_scr, wu_scr, wd_scr):
    i = pl.program_id(0)
    used = i < n_used_ref[0]
    prev = blk_e_ref[jnp.maximum(i - 1, 0)]

    @pl.when(used & ((i == 0) | (blk_e_ref[i] != prev)))
    def _():
        wg_scr[...] = wg_ref[0, 0].astype(BF16)
        wu_scr[...] = wu_ref[0, 0].astype(BF16)
        wd_scr[...] = wd_ref[0, 0].astype(BF16)

    @pl.when(used)
    def _():
        x = xs_ref[...].astype(BF16)
        a = (jax.nn.silu(_dot(x, wg_scr[...])) * _dot(x, wu_scr[...])).astype(BF16)
        ys_ref[...] = _dot(a, wd_scr[...])

    @pl.when(jnp.logical_not(used))
    def _():
        ys_ref[...] = jnp.zeros(ys_ref.shape, F32)


def _experts(layer, blk_e, n_used, xs, wg, wu, wd):
    p, d = xs.shape
    nb = p // EXPERT_BLOCK
    de = wg.shape[-1]
    last_used = lambda i, nu: jnp.minimum(i, nu[0] - 1)
    grid_spec = pltpu.PrefetchScalarGridSpec(
        num_scalar_prefetch=2, grid=(nb,),
        in_specs=[pl.BlockSpec((EXPERT_BLOCK, d), lambda i, be, nu: (last_used(i, nu), 0)),
                  pl.BlockSpec((1, 1, d, de), lambda i, be, nu: (layer, be[i], 0, 0)),
                  pl.BlockSpec((1, 1, d, de), lambda i, be, nu: (layer, be[i], 0, 0)),
                  pl.BlockSpec((1, 1, de, d), lambda i, be, nu: (layer, be[i], 0, 0))],
        out_specs=pl.BlockSpec((EXPERT_BLOCK, d), lambda i, be, nu: (i, 0)),
        scratch_shapes=[pltpu.VMEM((d, de), BF16), pltpu.VMEM((d, de), BF16), pltpu.VMEM((de, d), BF16)])
    return pl.pallas_call(
        _experts_body, grid_spec=grid_spec,
        out_shape=jax.ShapeDtypeStruct(xs.shape, F32),
        compiler_params=_params("arbitrary"), name="experts",
    )(blk_e, n_used, xs, wg, wu, wd)


def _combine_body(dest_ref, g8_ref, hs_ref, ys_hbm, o_ref, buf, sem):
    n_tok = hs_ref.shape[0]

    def row_copy(src_row, slot):
        return pltpu.make_async_copy(ys_hbm.at[pl.ds(src_row, 1), :], buf.at[pl.ds(slot, 1), :], sem)

    def issue(i, carry):
        for k in range(TOP_K):
            row_copy(dest_ref[i * TOP_K + k], k * n_tok + i).start()
        return carry

    lax.fori_loop(0, n_tok, issue, 0)

    def drain(i, carry):
        for k in range(TOP_K):
            row_copy(0, 0).wait()
        return carry

    lax.fori_loop(0, n_tok, drain, 0)

    acc = hs_ref[...]
    for k in range(TOP_K):
        acc = acc + g8_ref[:, k:k + 1] * buf[k * n_tok:(k + 1) * n_tok, :]
    o_ref[...] = acc


def _combine(dest_flat, g8, hs, ys):
    t, d = hs.shape
    tt = min(COMBINE_TILE, t)
    return pl.pallas_call(
        _combine_body, grid=(t // tt,),
        in_specs=[pl.BlockSpec((tt * TOP_K,), lambda i: (i,), memory_space=pltpu.SMEM),
                  pl.BlockSpec((tt, g8.shape[1]), lambda i: (i, 0)),
                  pl.BlockSpec((tt, d), lambda i: (i, 0)),
                  pl.BlockSpec(memory_space=pl.ANY)],
        out_specs=pl.BlockSpec((tt, d), lambda i: (i, 0)),
        out_shape=jax.ShapeDtypeStruct((t, d), F32),
        scratch_shapes=[pltpu.VMEM((tt * TOP_K, d), F32), pltpu.SemaphoreType.DMA(())],
        compiler_params=_params("arbitrary"), name="combine",
    )(dest_flat, g8, hs, ys)


def _ple_body(final, h2_ref, p_ref, win_ref, wgate_ref, g_ref, gf_ref, o_ref):
    h2 = h2_ref[...]
    e = _dot(p_ref[...].astype(BF16), win_ref[...])
    gate = jax.nn.sigmoid(_dot(h2.astype(BF16), wgate_ref[...]))
    h3 = h2 + _rms(gate * e, g_ref[...])
    o_ref[...] = _rms(h3, gf_ref[...]) if final else h3


def _ple(h2, p, w_in, w_gate, g, g_final, final):
    t, d = h2.shape
    tm = min(TOKEN_TILE, t)
    row = lambda n: pl.BlockSpec((tm, n), lambda i: (i, 0))
    return pl.pallas_call(
        functools.partial(_ple_body, final), grid=(t // tm,),
        in_specs=[row(d), row(p.shape[1]), _full(w_in.shape), _full(w_gate.shape), _full((1, d)), _full((1, d))],
        out_specs=row(d), out_shape=jax.ShapeDtypeStruct((t, d), F32),
        compiler_params=_params("parallel"), name="ple",
    )(h2, p, w_in, w_gate, g, g_final)


def _mixers(h, bsz, seq, norm1_g, w_in, four_w, ln_g, ln_b, ws, bs, conv_w, gate_b, norm_g, tables):
    d = h.shape[1]
    n_main = w_in.shape[1] - N_GATES
    w_main = w_in[:, :n_main].astype(BF16)
    w_g = w_in[:, n_main:]
    w_gate = jnp.pad(w_g, ((0, 0), (0, LANES - N_GATES))).astype(BF16)
    w_gate_t = w_g.T.astype(BF16)
    zf, zu, zv, zqk, zmv, zo, zg, zgt = _in_proj(h, norm1_g.reshape(1, d), w_main, w_gate, w_gate_t)
    b3 = lambda a: a.reshape(bsz, seq, a.shape[-1])

    cs, ss, cm, sm = tables
    yf = _fourier(b3(zf), cs, ss, cm, sm, _block_diag(four_w).astype(BF16))

    bs_full = jnp.repeat(bs.T, GMLP_HD, axis=1)
    yg = _gmlp(b3(zu), b3(zv), ln_g.reshape(1, -1), ln_b.reshape(1, -1), ws.astype(BF16), bs_full)

    zqk_pad = jnp.pad(b3(zqk), ((0, 0), (CONV_PAD, CONV_PAD), (0, 0)))
    gate_b_c = jnp.pad(gate_b, (0, LANES - N_GATES)).reshape(1, LANES)
    gate_b_t = jnp.broadcast_to(gate_b.reshape(N_GATES, 1), (N_GATES, LANES))
    ym = _mlstm(zqk_pad, b3(zmv), b3(zo), b3(zg), zgt, conv_w, gate_b_c, gate_b_t, norm_g.reshape(1, -1))
    t = bsz * seq
    return yf.reshape(t, -1), yg.reshape(t, -1), ym.reshape(t, -1)


def _moe(layer, h, yf, yg, ym, w_out, norm2_g, router_w, router_b, wg, wu, wd, sg, su, sd):
    t, d = h.shape
    sgu = jnp.concatenate([sg, su], axis=1).astype(BF16)
    hs, xn, sel, gates, idx = _post_mix(h, yf, yg, ym, w_out.astype(BF16), norm2_g.reshape(1, d),
                                        router_w, router_b.reshape(1, -1), sgu, sd.astype(BF16))
    rank, counts = _rank(sel)
    counts = counts[0]
    pcounts = ((counts + EXPERT_BLOCK - 1) // EXPERT_BLOCK) * EXPERT_BLOCK
    pend = jnp.cumsum(pcounts)
    offs = (pend - pcounts).astype(I32)
    tail_start = jnp.where(pcounts > 0, pend - EXPERT_BLOCK, -1).astype(I32)
    n_blk = (t * TOP_K) // EXPERT_BLOCK + N_EXPERTS
    blk_start = jnp.arange(n_blk, dtype=I32) * EXPERT_BLOCK
    blk_e = jnp.minimum(jnp.sum(pend[None, :] <= blk_start[:, None], axis=1), N_EXPERTS - 1).astype(I32)
    n_used = (pend[-1:] // EXPERT_BLOCK).astype(I32)

    dest, g8 = _dest(rank, idx, gates, offs.reshape(1, -1))
    dest_flat = dest[:, :TOP_K].reshape(-1)
    xs = _dispatch(dest_flat, tail_start, xn, n_blk * EXPERT_BLOCK)
    ys = _experts(layer, blk_e, n_used, xs, wg, wu, wd)
    return _combine(dest_flat, g8, hs, ys)


def kernel(x, p, norm1_g, w_in, four_w, gmlp_ln_g, gmlp_ln_b, gmlp_ws, gmlp_bs, mlstm_conv_w,
           mlstm_gate_b, mlstm_norm_g, w_out, norm2_g, router_w, router_b, exp_w_gate, exp_w_up,
           exp_w_down, sh_w_gate, sh_w_up, sh_w_down, ple_w_in, ple_w_gate, ple_norm_g, final_norm_g):
    bsz, seq, d = x.shape
    depth = w_in.shape[0]
    t = bsz * seq
    cos_s, sin_s = _dft_tables(seq)
    cos_m, sin_m = _dft_tables(FOUR_HD)
    eye = np.eye(FOUR_HEADS)
    tables = (jnp.asarray(cos_s, BF16), jnp.asarray(sin_s, BF16),
              jnp.asarray(np.kron(eye, cos_m), BF16), jnp.asarray(np.kron(eye, sin_m), BF16))
    h = x.reshape(t, d)
    for i in range(depth):
        yf, yg, ym = _mixers(h, bsz, seq, norm1_g[i], w_in[i], four_w[i], gmlp_ln_g[i], gmlp_ln_b[i],
                             gmlp_ws[i], gmlp_bs[i], mlstm_conv_w[i], mlstm_gate_b[i], mlstm_norm_g[i], tables)
        h2 = _moe(i, h, yf, yg, ym, w_out[i], norm2_g[i], router_w[i], router_b[i],
                  exp_w_gate, exp_w_up, exp_w_down, sh_w_gate[i], sh_w_up[i], sh_w_down[i])
        h = _ple(h2, p[i].reshape(t, -1), ple_w_in[i].astype(BF16), ple_w_gate[i].astype(BF16),
                 ple_norm_g[i].reshape(1, d), final_norm_g.reshape(1, d), i == depth - 1)
    return h.reshape(bsz, seq, d)
```

```python
import functools
import math

import numpy as np
import jax
import jax.numpy as jnp
from jax import lax
from jax.experimental import pallas as pl
from jax.experimental.pallas import tpu as pltpu

F32, BF16, I32, U32 = jnp.float32, jnp.bfloat16, jnp.int32, jnp.uint32
HIGHEST = lax.Precision.HIGHEST

EPS = 1e-6
LANES = 128
VMEM_LIMIT_BYTES = 48 * 1024 * 1024

FOUR_HEADS, FOUR_HD = 4, 64
GMLP_HEADS, GMLP_HD, GMLP_CHUNK = 4, 64, 128
MLSTM_HEADS, MLSTM_DV, MLSTM_DQK, MLSTM_CHUNK = 4, 128, 64, 128
CONV_PAD = 16
N_GATES = 16
N_EXPERTS, TOP_K = 128, 8
EXPERT_BLOCK = 256
TOKEN_TILE = 512
DISPATCH_TILE = 256
COMBINE_TILE = 256
ISSUE_GROUP = 8


def _params(*sem):
    return pltpu.CompilerParams(dimension_semantics=sem, vmem_limit_bytes=VMEM_LIMIT_BYTES)


def _rms(x, g):
    return x * lax.rsqrt(jnp.mean(x * x, axis=-1, keepdims=True) + EPS) * g


def _full(shape):
    nd = len(shape)
    return pl.BlockSpec(shape, lambda *_: (0,) * nd)


def _dot(a, b):
    return jnp.dot(a, b, preferred_element_type=F32)


def _pack_row(x):
    n = x.shape[-1] // 2
    lo = lax.bitcast_convert_type(x[:, :n].astype(BF16).astype(F32), U32)
    hi = lax.bitcast_convert_type(x[:, n:].astype(BF16).astype(F32), U32)
    return hi | (lo >> 16)


def _unpack_row(w):
    lo = lax.bitcast_convert_type(w << 16, F32)
    hi = lax.bitcast_convert_type(w & jnp.uint32(0xFFFF0000), F32)
    return lo, hi


ROW_SUB = 4


def _store_rows(ref, w):
    m = w.shape[0]
    for c in range(ROW_SUB):
        ref[pl.ds(c, m, stride=ROW_SUB), :] = w[:, c * LANES:(c + 1) * LANES]


def _load_rows(ref, r0, m):
    return [ref[pl.ds(r0 * ROW_SUB + c, m, stride=ROW_SUB), :] for c in range(ROW_SUB)]


def _in_proj_body(h_ref, g_ref, w_ref, wg_ref, wgt_ref,
                  zf_ref, zu_ref, zv_ref, zqk_ref, zmv_ref, zo_ref, zg_ref, zgt_ref):
    ab = _rms(h_ref[...], g_ref[...]).astype(BF16)
    off = 0
    for o_ref in (zf_ref, zu_ref, zv_ref, zqk_ref, zmv_ref, zo_ref):
        n = o_ref.shape[-1]
        o_ref[...] = _dot(ab, w_ref[:, off:off + n]).astype(o_ref.dtype)
        off += n
    zg_ref[...] = _dot(ab, wg_ref[...])
    zgt_ref[...] = lax.dot_general(wgt_ref[...], ab, (((1,), (1,)), ((), ())),
                                   preferred_element_type=F32)


def _in_proj(h, g, w_main, w_gate, w_gate_t):
    t, d = h.shape
    tm = min(TOKEN_TILE, t)
    widths = (256, 256, 256, 512, 512, 512)
    row = lambda n: pl.BlockSpec((tm, n), lambda i: (i, 0))
    out_shape = [jax.ShapeDtypeStruct((t, n), BF16) for n in widths]
    out_shape += [jax.ShapeDtypeStruct((t, LANES), F32), jax.ShapeDtypeStruct((N_GATES, t), F32)]
    out_specs = [row(n) for n in widths]
    out_specs += [row(LANES), pl.BlockSpec((N_GATES, tm), lambda i: (0, i))]
    return pl.pallas_call(
        _in_proj_body, grid=(t // tm,),
        in_specs=[row(d), _full((1, d)), _full(w_main.shape), _full(w_gate.shape), _full(w_gate_t.shape)],
        out_specs=out_specs, out_shape=out_shape,
        compiler_params=_params("parallel"), name="in_proj",
    )(h, g, w_main, w_gate, w_gate_t)


def _dft_tables(n):
    k = np.arange(n, dtype=np.int64)
    ang = 2.0 * np.pi * ((k[:, None] * k[None, :]) % n).astype(np.float64) / n
    return np.cos(ang), np.sin(ang)


def _block_diag(blocks):
    h, a, b = blocks.shape
    eye = jnp.eye(h, dtype=blocks.dtype)
    return (eye[:, None, :, None] * blocks[:, :, None, :]).reshape(h * a, h * b)


def _fourier_body(scale, row_tile, z_ref, cs_ref, ss_ref, cm_ref, sm_ref, w_ref, o_ref, p_scr, q_scr):
    z = z_ref[0]
    p_scr[...] = _dot(z, cm_ref[...]).astype(BF16)
    q_scr[...] = _dot(z, sm_ref[...]).astype(BF16)
    s = z.shape[0]
    for r in range(s // row_tile):
        rs = slice(r * row_tile, (r + 1) * row_tile)
        re = (_dot(cs_ref[rs, :], p_scr[...]) - _dot(ss_ref[rs, :], q_scr[...])) * scale
        o_ref[0, rs, :] = _dot(re.astype(BF16), w_ref[...]).astype(o_ref.dtype)


def _fourier(zf, cs, ss, cm, sm, wbd):
    b, s, w = zf.shape
    row_tile = min(512, s)
    scale = 1.0 / math.sqrt(s * FOUR_HD)
    blk = pl.BlockSpec((1, s, w), lambda i: (i, 0, 0))
    return pl.pallas_call(
        functools.partial(_fourier_body, scale, row_tile), grid=(b,),
        in_specs=[blk, _full(cs.shape), _full(ss.shape), _full(cm.shape), _full(sm.shape), _full(wbd.shape)],
        out_specs=blk, out_shape=jax.ShapeDtypeStruct((b, s, w), BF16),
        scratch_shapes=[pltpu.VMEM((s, w), BF16), pltpu.VMEM((s, w), BF16)],
        compiler_params=_params("parallel"), name="fourier",
    )(zf, cs, ss, cm, sm, wbd)


def _gmlp_body(zu_ref, zv_ref, lg_ref, lb_ref, ws_ref, bs_ref, o_ref):
    s = zu_ref.shape[1]
    w = zu_ref.shape[2]
    lane = lax.broadcasted_iota(I32, (GMLP_CHUNK, w), 1)

    def chunk(c, carry):
        r0 = pl.multiple_of(c * GMLP_CHUNK, GMLP_CHUNK)
        rows = pl.ds(r0, GMLP_CHUNK)
        v = jax.nn.gelu(zv_ref[0, rows, :].astype(F32))
        vc = v - jnp.mean(v, axis=-1, keepdims=True)
        vn = vc * lax.rsqrt(jnp.mean(vc * vc, axis=-1, keepdims=True) + EPS) * lg_ref[...] + lb_ref[...]
        acc = bs_ref[...]
        for h in range(GMLP_HEADS):
            vh = jnp.where((lane >= h * GMLP_HD) & (lane < (h + 1) * GMLP_HD), vn, 0.0).astype(BF16)
            acc = acc + _dot(ws_ref[h], vh)
        u = jax.nn.gelu(zu_ref[0, rows, :].astype(F32))
        o_ref[0, rows, :] = (u * acc).astype(o_ref.dtype)
        return carry

    lax.fori_loop(0, s // GMLP_CHUNK, chunk, 0)


def _gmlp(zu, zv, ln_g, ln_b, ws, bs_full):
    b, s, w = zu.shape
    blk = pl.BlockSpec((1, s, w), lambda i: (i, 0, 0))
    return pl.pallas_call(
        _gmlp_body, grid=(b,),
        in_specs=[blk, blk, _full((1, w)), _full((1, w)), _full(ws.shape), _full(bs_full.shape)],
        out_specs=blk, out_shape=jax.ShapeDtypeStruct((b, s, w), BF16),
        compiler_params=_params("parallel"), name="gmlp",
    )(zu, zv, ln_g, ln_b, ws, bs_full)


def _mlstm_body(qkp_ref, v_ref, zo_ref, g_ref, gt_ref, cw_ref, gb_ref, gbt_ref, ng_ref, o_ref,
                qm_scr, km_scr, hf_scr, hb_scr, c_scr, n_scr, m_scr):
    L = MLSTM_CHUNK
    H = MLSTM_HEADS
    s = v_ref.shape[1]
    nc = s // L
    qkw = H * MLSTM_DQK

    ext = L + 2 * CONV_PAD
    r_i = lax.broadcasted_iota(I32, (L, ext), 0)
    c_i = lax.broadcasted_iota(I32, (L, ext), 1)
    sh_m1 = jnp.where(c_i == r_i + CONV_PAD - 1, 1.0, 0.0).astype(BF16)
    sh_0 = jnp.where(c_i == r_i + CONV_PAD, 1.0, 0.0).astype(BF16)
    sh_p1 = jnp.where(c_i == r_i + CONV_PAD + 1, 1.0, 0.0).astype(BF16)
    lane = lax.broadcasted_iota(I32, (L, LANES), 1)
    low_half = lane < MLSTM_DQK

    def conv_chunk(c, carry):
        r0 = pl.multiple_of(c * L, L)
        xe = qkp_ref[0, pl.ds(r0, ext), :]
        y = (cw_ref[0:1, :] * _dot(sh_m1, xe) + cw_ref[1:2, :] * _dot(sh_0, xe)
             + cw_ref[2:3, :] * _dot(sh_p1, xe))
        qk = y * jax.nn.sigmoid(y)
        for h in range(H):
            keep = low_half if h % 2 == 0 else jnp.logical_not(low_half)
            t0 = (h // 2) * LANES
            q_t = qk[:, t0:t0 + LANES] * (MLSTM_DQK ** -0.5)
            k_t = qk[:, qkw + t0:qkw + t0 + LANES]
            qm_scr[h, pl.ds(r0, L), :] = jnp.where(keep, q_t, 0.0).astype(BF16)
            km_scr[h, pl.ds(r0, L), :] = jnp.where(keep, k_t, 0.0).astype(BF16)
        return carry

    lax.fori_loop(0, nc, conv_chunk, 0)

    c_scr[...] = jnp.zeros(c_scr.shape, F32)
    n_scr[...] = jnp.zeros(n_scr.shape, F32)
    m_scr[...] = jnp.zeros(m_scr.shape, F32)
    ri = lax.broadcasted_iota(I32, (L, L), 0)
    ci = lax.broadcasted_iota(I32, (L, L), 1)
    tri_l = jnp.where(ci <= ri, 1.0, 0.0)
    tri_u = jnp.where(ci >= ri, 1.0, 0.0)

    def direction(d, chunk):
        r0 = pl.multiple_of(chunk * L, L)
        rows = pl.ds(r0, L)
        gc = g_ref[0, rows, :] + gb_ref[...]
        gr = gt_ref[:, rows] + gbt_ref[:, 0:1]
        lf_c = jax.nn.log_sigmoid(gc)
        lf_r = jax.nn.log_sigmoid(gr)
        if d == 0:
            cum_c = jnp.dot(tri_l, lf_c, precision=HIGHEST, preferred_element_type=F32)
            cum_r = jnp.dot(lf_r, tri_u, precision=HIGHEST, preferred_element_type=F32)
            mask = ci <= ri
        else:
            cum_c = jnp.dot(tri_u, lf_c, precision=HIGHEST, preferred_element_type=F32)
            cum_r = jnp.dot(lf_r, tri_l, precision=HIGHEST, preferred_element_type=F32)
            mask = ci >= ri
        i_lane = 2 * d * H
        f_lane = (2 * d + 1) * H
        for h in range(H):
            u = d * H + h
            b_c = cum_c[:, f_lane + h:f_lane + h + 1]
            li_c = gc[:, i_lane + h:i_lane + h + 1]
            b_r = cum_r[f_lane + h:f_lane + h + 1, :]
            li_r = gr[i_lane + h:i_lane + h + 1, :]
            g_tot = b_c[L - 1:L, :] if d == 0 else b_c[0:1, :]
            qh = qm_scr[h, rows, :]
            kh = km_scr[h, rows, :]
            vh = v_ref[0, rows, h * MLSTM_DV:(h + 1) * MLSTM_DV]
            c_prev = c_scr[u]
            n_prev = n_scr[u]
            m_prev = m_scr[u][:, 0:1]
            dmat = jnp.where(mask, b_c - b_r + li_r, -jnp.inf)
            m_intra = jnp.max(dmat, axis=-1, keepdims=True)
            m_inter = b_c + m_prev
            m_tot = jnp.maximum(m_intra, m_inter)
            qk = lax.dot_general(qh, kh, (((1,), (1,)), ((), ())), preferred_element_type=F32)
            s_mat = jnp.exp(dmat - m_tot) * qk
            inter = jnp.exp(m_inter - m_tot)
            num = _dot(s_mat.astype(BF16), vh) + inter * _dot(qh, c_prev.astype(BF16))
            qn = jnp.sum(qh.astype(F32) * n_prev, axis=-1, keepdims=True)
            den = jnp.sum(s_mat, axis=-1, keepdims=True) + inter * qn
            hh = num / jnp.maximum(jnp.abs(den), jnp.exp(-m_tot))
            dst = hf_scr if d == 0 else hb_scr
            dst[rows, h * MLSTM_DV:(h + 1) * MLSTM_DV] = hh
            a_c = g_tot - b_c + li_c
            m_loc = jnp.max(a_c, axis=0, keepdims=True)
            w_c = jnp.exp(a_c - m_loc)
            c_loc = lax.dot_general(kh, (w_c * vh.astype(F32)).astype(BF16),
                                    (((0,), (0,)), ((), ())), preferred_element_type=F32)
            n_loc = jnp.sum(w_c * kh.astype(F32), axis=0, keepdims=True)
            m_new = jnp.maximum(g_tot + m_prev, m_loc)
            a_old = jnp.exp(g_tot + m_prev - m_new)
            a_new = jnp.exp(m_loc - m_new)
            c_scr[u] = a_old * c_prev + a_new * c_loc
            n_scr[u] = a_old * n_prev + a_new * n_loc
            m_scr[u] = jnp.broadcast_to(m_new, (1, LANES))

    def step(i, carry):
        direction(0, i)
        direction(1, nc - 1 - i)
        return carry

    lax.fori_loop(0, nc, step, 0)

    def finish(c, carry):
        r0 = pl.multiple_of(c * L, L)
        rows = pl.ds(r0, L)
        og = jax.nn.sigmoid(zo_ref[0, rows, :].astype(F32))
        hsum = og * (hf_scr[rows, :] + hb_scr[rows, :])
        for h in range(H):
            cols = slice(h * MLSTM_DV, (h + 1) * MLSTM_DV)
            x = hsum[:, cols]
            xc = x - jnp.mean(x, axis=-1, keepdims=True)
            y = xc * lax.rsqrt(jnp.mean(xc * xc, axis=-1, keepdims=True) + EPS) * ng_ref[:, cols]
            o_ref[0, rows, cols] = y.astype(o_ref.dtype)
        return carry

    lax.fori_loop(0, nc, finish, 0)


def _mlstm(zqk_pad, zmv, zo, zg, zgt, conv_w, gate_b, gate_b_t, norm_g):
    b, s, w = zmv.shape
    H = MLSTM_HEADS
    blk = pl.BlockSpec((1, s, w), lambda i: (i, 0, 0))
    units = 2 * H
    return pl.pallas_call(
        _mlstm_body, grid=(b,),
        in_specs=[pl.BlockSpec((1, s + 2 * CONV_PAD, w), lambda i: (i, 0, 0)), blk, blk,
                  pl.BlockSpec((1, s, LANES), lambda i: (i, 0, 0)),
                  pl.BlockSpec((N_GATES, s), lambda i: (0, i)),
                  _full(conv_w.shape), _full(gate_b.shape), _full(gate_b_t.shape), _full(norm_g.shape)],
        out_specs=blk, out_shape=jax.ShapeDtypeStruct((b, s, w), BF16),
        scratch_shapes=[pltpu.VMEM((H, s, LANES), BF16), pltpu.VMEM((H, s, LANES), BF16),
                        pltpu.VMEM((s, w), F32), pltpu.VMEM((s, w), F32),
                        pltpu.VMEM((units, LANES, MLSTM_DV), F32), pltpu.VMEM((units, 1, LANES), F32),
                        pltpu.VMEM((units, 1, LANES), F32)],
        compiler_params=_params("parallel"), name="mlstm",
    )(zqk_pad, zmv, zo, zg, zgt, conv_w, gate_b, gate_b_t, norm_g)


def _post_mix_body(h_ref, yf_ref, yg_ref, ym_ref, wo_ref, g2_ref, rw_ref, rb_ref, sgu_ref, sd_ref,
                   hs_ref, xw_ref, sel_ref, gate_ref, idx_ref):
    wf, wg = yf_ref.shape[-1], yg_ref.shape[-1]
    h1 = (h_ref[...] + _dot(yf_ref[...], wo_ref[0:wf, :]) + _dot(yg_ref[...], wo_ref[wf:wf + wg, :])
          + _dot(ym_ref[...], wo_ref[wf + wg:, :]))
    xn = _rms(h1, g2_ref[...])
    xb = xn.astype(BF16)
    _store_rows(xw_ref, _pack_row(xn))
    gu = _dot(xb, sgu_ref[...])
    de = gu.shape[-1] // 2
    act = (jax.nn.silu(gu[:, :de]) * gu[:, de:]).astype(BF16)
    hs_ref[...] = h1 + _dot(act, sd_ref[...])
    scores = jax.nn.sigmoid(jnp.dot(xn, rw_ref[...], precision=HIGHEST, preferred_element_type=F32))
    work = scores + rb_ref[...]
    lane = lax.broadcasted_iota(I32, scores.shape, 1)
    sel = jnp.zeros(scores.shape, F32)
    idx = jnp.zeros(scores.shape, I32)
    for k in range(TOP_K):
        m = jnp.max(work, axis=-1, keepdims=True)
        e = jnp.min(jnp.where(work == m, lane, N_EXPERTS), axis=-1, keepdims=True)
        hit = lane == e
        sel = jnp.where(hit, 1.0, sel)
        idx = jnp.where(lane == k, e, idx)
        work = jnp.where(hit, -jnp.inf, work)
    picked = sel * scores
    sel_ref[...] = sel
    gate_ref[...] = picked / jnp.sum(picked, axis=-1, keepdims=True)
    idx_ref[...] = idx


def _post_mix(h, yf, yg, ym, w_out, g2, router_w, router_b, sgu, sd):
    t, d = h.shape
    tm = min(TOKEN_TILE, t)
    row = lambda n: pl.BlockSpec((tm, n), lambda i: (i, 0))
    return pl.pallas_call(
        _post_mix_body, grid=(t // tm,),
        in_specs=[row(d), row(yf.shape[1]), row(yg.shape[1]), row(ym.shape[1]), _full(w_out.shape),
                  _full((1, d)), _full(router_w.shape), _full((1, N_EXPERTS)), _full(sgu.shape), _full(sd.shape)],
        out_specs=[row(d), pl.BlockSpec((tm * ROW_SUB, LANES), lambda i: (i, 0)),
                   row(N_EXPERTS), row(N_EXPERTS), row(N_EXPERTS)],
        out_shape=[jax.ShapeDtypeStruct((t, d), F32), jax.ShapeDtypeStruct((t * ROW_SUB, LANES), U32),
                   jax.ShapeDtypeStruct((t, N_EXPERTS), F32), jax.ShapeDtypeStruct((t, N_EXPERTS), F32),
                   jax.ShapeDtypeStruct((t, N_EXPERTS), I32)],
        compiler_params=_params("parallel"), name="post_mix",
    )(h, yf, yg, ym, w_out, g2, router_w, router_b, sgu, sd)


def _rank_body(sel_ref, rank_ref, cnt_ref, carry):
    @pl.when(pl.program_id(0) == 0)
    def _():
        carry[...] = jnp.zeros(carry.shape, F32)

    m = sel_ref[...]
    tm = m.shape[0]
    ri = lax.broadcasted_iota(I32, (tm, tm), 0)
    ci = lax.broadcasted_iota(I32, (tm, tm), 1)
    below = jnp.where(ci < ri, 1.0, 0.0).astype(BF16)
    rank = _dot(below, m.astype(BF16)) + carry[...]
    rank_ref[...] = rank.astype(I32)
    carry[...] = carry[...] + jnp.sum(m, axis=0, keepdims=True)
    cnt_ref[...] = carry[...].astype(I32)


def _rank(sel):
    t, e = sel.shape
    tm = min(TOKEN_TILE, t)
    return pl.pallas_call(
        _rank_body, grid=(t // tm,),
        in_specs=[pl.BlockSpec((tm, e), lambda i: (i, 0))],
        out_specs=[pl.BlockSpec((tm, e), lambda i: (i, 0)), _full((1, e))],
        out_shape=[jax.ShapeDtypeStruct((t, e), I32), jax.ShapeDtypeStruct((1, e), I32)],
        scratch_shapes=[pltpu.VMEM((1, e), F32)],
        compiler_params=_params("arbitrary"), name="rank",
    )(sel)


def _dest_body(rank_ref, idx_ref, gate_ref, offs_ref, dest_ref, g8_ref):
    pos = (rank_ref[...] + offs_ref[...]).astype(F32)
    gates = gate_ref[...]
    lane = lax.broadcasted_iota(I32, pos.shape, 1)
    dest = jnp.zeros(pos.shape, F32)
    g8 = jnp.zeros(pos.shape, F32)
    for k in range(TOP_K):
        hit = lane == idx_ref[:, k:k + 1]
        dk = jnp.sum(jnp.where(hit, pos, 0.0), axis=-1, keepdims=True)
        gk = jnp.sum(jnp.where(hit, gates, 0.0), axis=-1, keepdims=True)
        dest = jnp.where(lane == k, dk, dest)
        g8 = jnp.where(lane == k, gk, g8)
    dest_ref[...] = dest.astype(I32)
    g8_ref[...] = g8


def _dest(rank, idx, gates, offs):
    t, e = rank.shape
    tm = min(TOKEN_TILE, t)
    row = pl.BlockSpec((tm, e), lambda i: (i, 0))
    return pl.pallas_call(
        _dest_body, grid=(t // tm,),
        in_specs=[row, row, row, _full((1, e))],
        out_specs=[row, row],
        out_shape=[jax.ShapeDtypeStruct((t, e), I32), jax.ShapeDtypeStruct((t, e), F32)],
        compiler_params=_params("parallel"), name="dest",
    )(rank, idx, gates, offs)


def _dispatch_body(dest_ref, tail_ref, x_ref, xs_hbm, zero_scr, sem):
    n_grp = x_ref.shape[0]

    @pl.when(pl.program_id(0) == 0)
    def _():
        zero_scr[...] = jnp.zeros(zero_scr.shape, zero_scr.dtype)

        def tail_copy(e):
            start = pl.multiple_of(jnp.maximum(tail_ref[e], 0) * ROW_SUB, EXPERT_BLOCK * ROW_SUB)
            return pltpu.make_async_copy(zero_scr, xs_hbm.at[pl.ds(start, EXPERT_BLOCK * ROW_SUB), :], sem)

        def clear(e, carry):
            @pl.when(tail_ref[e] >= 0)
            def _():
                tail_copy(e).start()
            return carry

        def clear_done(e, carry):
            @pl.when(tail_ref[e] >= 0)
            def _():
                tail_copy(e).wait()
            return carry

        lax.fori_loop(0, N_EXPERTS, clear, 0)
        lax.fori_loop(0, N_EXPERTS, clear_done, 0)

    def row_copy(group, g, dst_row):
        dst = pl.ds(pl.multiple_of(dst_row * ROW_SUB, ROW_SUB), ROW_SUB)
        return pltpu.make_async_copy(x_ref.at[group, g], xs_hbm.at[dst, :], sem)

    def for_group(op):
        def body(gi, carry):
            p0 = gi * (ISSUE_GROUP * TOP_K)
            for g in range(ISSUE_GROUP):
                for k in range(TOP_K):
                    op(row_copy(gi, g, dest_ref[p0 + g * TOP_K + k]))
            return carry
        lax.fori_loop(0, n_grp, body, 0)

    for_group(lambda cp: cp.start())
    for_group(lambda cp: cp.wait())


def _dispatch(dest_flat, tail_start, xw, n_rows):
    t = xw.shape[0] // ROW_SUB
    tt = min(DISPATCH_TILE, t)
    x4 = xw.reshape(t // ISSUE_GROUP, ISSUE_GROUP, ROW_SUB, LANES)
    return pl.pallas_call(
        _dispatch_body, grid=(t // tt,),
        in_specs=[pl.BlockSpec((tt * TOP_K,), lambda i: (i,), memory_space=pltpu.SMEM),
                  pl.BlockSpec(memory_space=pltpu.SMEM),
                  pl.BlockSpec((tt // ISSUE_GROUP, ISSUE_GROUP, ROW_SUB, LANES), lambda i: (i, 0, 0, 0))],
        out_specs=pl.BlockSpec(memory_space=pl.ANY),
        out_shape=jax.ShapeDtypeStruct((n_rows * ROW_SUB, LANES), xw.dtype),
        scratch_shapes=[pltpu.VMEM((EXPERT_BLOCK * ROW_SUB, LANES), xw.dtype), pltpu.SemaphoreType.DMA(())],
        compiler_params=_params("arbitrary"), name="dispatch",
    )(dest_flat, tail_start, x4)


def _experts_body(blk_e_ref, n_used_ref, xs_ref, wg_ref, wu_ref, wd_ref, ys_ref, wg_scr, wu_scr, wd_scr):
    i = pl.program_id(0)
    used = i < n_used_ref[0]
    prev = blk_e_ref[jnp.maximum(i - 1, 0)]

    @pl.when(used & ((i == 0) | (blk_e_ref[i] != prev)))
    def _():
        wg_scr[...] = wg_ref[0, 0].astype(BF16)
        wu_scr[...] = wu_ref[0, 0].astype(BF16)
        wd_scr[...] = wd_ref[0, 0].astype(BF16)

    @pl.when(used)
    def _():
        halves = [_unpack_row(w) for w in _load_rows(xs_ref, 0, EXPERT_BLOCK)]
        x = jnp.concatenate([lo for lo, _ in halves] + [hi for _, hi in halves], axis=-1).astype(BF16)
        a = (jax.nn.silu(_dot(x, wg_scr[...])) * _dot(x, wu_scr[...])).astype(BF16)
        _store_rows(ys_ref, _pack_row(_dot(a, wd_scr[...])))

    @pl.when(jnp.logical_not(used))
    def _():
        ys_ref[...] = jnp.zeros(ys_ref.shape, ys_ref.dtype)


def _experts(layer, blk_e, n_used, xs, wg, wu, wd):
    blk = (EXPERT_BLOCK * ROW_SUB, LANES)
    nb = xs.shape[0] // blk[0]
    d, de = wg.shape[-2], wg.shape[-1]
    last_used = lambda i, nu: jnp.minimum(i, nu[0] - 1)
    grid_spec = pltpu.PrefetchScalarGridSpec(
        num_scalar_prefetch=2, grid=(nb,),
        in_specs=[pl.BlockSpec(blk, lambda i, be, nu: (last_used(i, nu), 0)),
                  pl.BlockSpec((1, 1, d, de), lambda i, be, nu: (layer, be[i], 0, 0)),
                  pl.BlockSpec((1, 1, d, de), lambda i, be, nu: (layer, be[i], 0, 0)),
                  pl.BlockSpec((1, 1, de, d), lambda i, be, nu: (layer, be[i], 0, 0))],
        out_specs=pl.BlockSpec(blk, lambda i, be, nu: (i, 0)),
        scratch_shapes=[pltpu.VMEM((d, de), BF16), pltpu.VMEM((d, de), BF16), pltpu.VMEM((de, d), BF16)])
    return pl.pallas_call(
        _experts_body, grid_spec=grid_spec,
        out_shape=jax.ShapeDtypeStruct(xs.shape, xs.dtype),
        compiler_params=_params("arbitrary"), name="experts",
    )(blk_e, n_used, xs, wg, wu, wd)


def _combine_body(dest_ref, g8_ref, hs_ref, ys_hbm, o_ref, buf, sem):
    n_tok = hs_ref.shape[0]

    n_grp = n_tok // ISSUE_GROUP

    def row_copy(src_row, k, gi, g):
        src = pl.ds(pl.multiple_of(src_row * ROW_SUB, ROW_SUB), ROW_SUB)
        slot = (k * n_grp + gi) * (ISSUE_GROUP * ROW_SUB) + g * ROW_SUB
        return pltpu.make_async_copy(ys_hbm.at[src, :], buf.at[pl.ds(pl.multiple_of(slot, ROW_SUB), ROW_SUB), :], sem)

    def for_group(op):
        def body(gi, carry):
            p0 = gi * (ISSUE_GROUP * TOP_K)
            for g in range(ISSUE_GROUP):
                for k in range(TOP_K):
                    op(row_copy(dest_ref[p0 + g * TOP_K + k], k, gi, g))
            return carry
        lax.fori_loop(0, n_grp, body, 0)

    for_group(lambda cp: cp.start())
    for_group(lambda cp: cp.wait())

    half = ROW_SUB * LANES

    def reduce(gi, carry):
        r0 = pl.multiple_of(gi * ISSUE_GROUP, ISSUE_GROUP)
        rows = pl.ds(r0, ISSUE_GROUP)
        gates = [g8_ref[rows, k:k + 1] for k in range(TOP_K)]
        picked = [_load_rows(buf, k * n_tok + r0, ISSUE_GROUP) for k in range(TOP_K)]
        for c in range(ROW_SUB):
            lo_cols = slice(c * LANES, (c + 1) * LANES)
            hi_cols = slice(half + c * LANES, half + (c + 1) * LANES)
            acc_lo = hs_ref[rows, lo_cols]
            acc_hi = hs_ref[rows, hi_cols]
            for k in range(TOP_K):
                lo, hi = _unpack_row(picked[k][c])
                acc_lo = acc_lo + gates[k] * lo
                acc_hi = acc_hi + gates[k] * hi
            o_ref[rows, lo_cols] = acc_lo
            o_ref[rows, hi_cols] = acc_hi
        return carry

    lax.fori_loop(0, n_grp, reduce, 0)


def _combine(dest_flat, g8, hs, ys):
    t, d = hs.shape
    tt = min(COMBINE_TILE, t)
    return pl.pallas_call(
        _combine_body, grid=(t // tt,),
        in_specs=[pl.BlockSpec((tt * TOP_K,), lambda i: (i,), memory_space=pltpu.SMEM),
                  pl.BlockSpec((tt, g8.shape[1]), lambda i: (i, 0)),
                  pl.BlockSpec((tt, d), lambda i: (i, 0)),
                  pl.BlockSpec(memory_space=pl.ANY)],
        out_specs=pl.BlockSpec((tt, d), lambda i: (i, 0)),
        out_shape=jax.ShapeDtypeStruct((t, d), F32),
        scratch_shapes=[pltpu.VMEM((TOP_K * tt * ROW_SUB, LANES), ys.dtype), pltpu.SemaphoreType.DMA(())],
        compiler_params=_params("arbitrary"), name="combine",
    )(dest_flat, g8, hs, ys)


def _ple_body(final, h2_ref, p_ref, win_ref, wgate_ref, g_ref, gf_ref, o_ref):
    h2 = h2_ref[...]
    e = _dot(p_ref[...].astype(BF16), win_ref[...])
    gate = jax.nn.sigmoid(_dot(h2.astype(BF16), wgate_ref[...]))
    h3 = h2 + _rms(gate * e, g_ref[...])
    o_ref[...] = _rms(h3, gf_ref[...]) if final else h3


def _ple(h2, p, w_in, w_gate, g, g_final, final):
    t, d = h2.shape
    tm = min(TOKEN_TILE, t)
    row = lambda n: pl.BlockSpec((tm, n), lambda i: (i, 0))
    return pl.pallas_call(
        functools.partial(_ple_body, final), grid=(t // tm,),
        in_specs=[row(d), row(p.shape[1]), _full(w_in.shape), _full(w_gate.shape), _full((1, d)), _full((1, d))],
        out_specs=row(d), out_shape=jax.ShapeDtypeStruct((t, d), F32),
        compiler_params=_params("parallel"), name="ple",
    )(h2, p, w_in, w_gate, g, g_final)


def _mixers(h, bsz, seq, norm1_g, w_in, four_w, ln_g, ln_b, ws, bs, conv_w, gate_b, norm_g, tables):
    d = h.shape[1]
    n_main = w_in.shape[1] - N_GATES
    w_main = w_in[:, :n_main].astype(BF16)
    w_g = w_in[:, n_main:]
    w_gate = jnp.pad(w_g, ((0, 0), (0, LANES - N_GATES))).astype(BF16)
    w_gate_t = w_g.T.astype(BF16)
    zf, zu, zv, zqk, zmv, zo, zg, zgt = _in_proj(h, norm1_g.reshape(1, d), w_main, w_gate, w_gate_t)
    b3 = lambda a: a.reshape(bsz, seq, a.shape[-1])

    cs, ss, cm, sm = tables
    yf = _fourier(b3(zf), cs, ss, cm, sm, _block_diag(four_w).astype(BF16))

    bs_full = jnp.repeat(bs.T, GMLP_HD, axis=1)
    yg = _gmlp(b3(zu), b3(zv), ln_g.reshape(1, -1), ln_b.reshape(1, -1), ws.astype(BF16), bs_full)

    zqk_pad = jnp.pad(b3(zqk), ((0, 0), (CONV_PAD, CONV_PAD), (0, 0)))
    gate_b_c = jnp.pad(gate_b, (0, LANES - N_GATES)).reshape(1, LANES)
    gate_b_t = jnp.broadcast_to(gate_b.reshape(N_GATES, 1), (N_GATES, LANES))
    ym = _mlstm(zqk_pad, b3(zmv), b3(zo), b3(zg), zgt, conv_w, gate_b_c, gate_b_t, norm_g.reshape(1, -1))
    t = bsz * seq
    return yf.reshape(t, -1), yg.reshape(t, -1), ym.reshape(t, -1)


def _moe(layer, h, yf, yg, ym, w_out, norm2_g, router_w, router_b, wg, wu, wd, sg, su, sd):
    t, d = h.shape
    sgu = jnp.concatenate([sg, su], axis=1).astype(BF16)
    hs, xw, sel, gates, idx = _post_mix(h, yf, yg, ym, w_out.astype(BF16), norm2_g.reshape(1, d),
                                        router_w, router_b.reshape(1, -1), sgu, sd.astype(BF16))
    rank, counts = _rank(sel)
    counts = counts[0]
    pcounts = ((counts + EXPERT_BLOCK - 1) // EXPERT_BLOCK) * EXPERT_BLOCK
    pend = jnp.cumsum(pcounts)
    offs = (pend - pcounts).astype(I32)
    tail_start = jnp.where(pcounts > 0, pend - EXPERT_BLOCK, -1).astype(I32)
    n_blk = (t * TOP_K) // EXPERT_BLOCK + N_EXPERTS
    blk_start = jnp.arange(n_blk, dtype=I32) * EXPERT_BLOCK
    blk_e = jnp.minimum(jnp.sum(pend[None, :] <= blk_start[:, None], axis=1), N_EXPERTS - 1).astype(I32)
    n_used = (pend[-1:] // EXPERT_BLOCK).astype(I32)

    dest, g8 = _dest(rank, idx, gates, offs.reshape(1, -1))
    dest_flat = dest[:, :TOP_K].reshape(-1)
    xs = _dispatch(dest_flat, tail_start, xw, n_blk * EXPERT_BLOCK)
    ys = _experts(layer, blk_e, n_used, xs, wg, wu, wd)
    return _combine(dest_flat, g8, hs, ys)


def kernel(x, p, norm1_g, w_in, four_w, gmlp_ln_g, gmlp_ln_b, gmlp_ws, gmlp_bs, mlstm_conv_w,
           mlstm_gate_b, mlstm_norm_g, w_out, norm2_g, router_w, router_b, exp_w_gate, exp_w_up,
           exp_w_down, sh_w_gate, sh_w_up, sh_w_down, ple_w_in, ple_w_gate, ple_norm_g, final_norm_g):
    bsz, seq, d = x.shape
    depth = w_in.shape[0]
    t = bsz * seq
    cos_s, sin_s = _dft_tables(seq)
    cos_m, sin_m = _dft_tables(FOUR_HD)
    eye = np.eye(FOUR_HEADS)
    tables = (jnp.asarray(cos_s, BF16), jnp.asarray(sin_s, BF16),
              jnp.asarray(np.kron(eye, cos_m), BF16), jnp.asarray(np.kron(eye, sin_m), BF16))
    h = x.reshape(t, d)
    for i in range(depth):
        yf, yg, ym = _mixers(h, bsz, seq, norm1_g[i], w_in[i], four_w[i], gmlp_ln_g[i], gmlp_ln_b[i],
                             gmlp_ws[i], gmlp_bs[i], mlstm_conv_w[i], mlstm_gate_b[i], mlstm_norm_g[i], tables)
        h2 = _moe(i, h, yf, yg, ym, w_out[i], norm2_g[i], router_w[i], router_b[i],
                  exp_w_gate, exp_w_up, exp_w_down, sh_w_gate[i], sh_w_up[i], sh_w_down[i])
        h = _ple(h2, p[i].reshape(t, -1), ple_w_in[i].astype(BF16), ple_w_gate[i].astype(BF16),
                 ple_norm_g[i].reshape(1, d), final_norm_g.reshape(1, d), i == depth - 1)
    return h.reshape(bsz, seq, d)
```

```python
import functools
import math

import numpy as np
import jax
import jax.numpy as jnp
from jax import lax
from jax.experimental import pallas as pl
from jax.experimental.pallas import tpu as pltpu

F32, BF16, I32, U32 = jnp.float32, jnp.bfloat16, jnp.int32, jnp.uint32
HIGHEST = lax.Precision.HIGHEST

EPS = 1e-6
LANES = 128
VMEM_LIMIT_BYTES = 48 * 1024 * 1024

FOUR_HEADS, FOUR_HD = 4, 64
GMLP_HEADS, GMLP_HD, GMLP_CHUNK = 4, 64, 128
MLSTM_HEADS, MLSTM_DV, MLSTM_DQK, MLSTM_CHUNK = 4, 128, 64, 128
CONV_PAD = 16
N_GATES = 16
N_EXPERTS, TOP_K = 128, 8
EXPERT_BLOCK = 512
TOKEN_TILE = 512
DISPATCH_TILE = 256
COMBINE_TILE = 256
ISSUE_GROUP = 8


def _params(*sem):
    return pltpu.CompilerParams(dimension_semantics=sem, vmem_limit_bytes=VMEM_LIMIT_BYTES)


def _rms(x, g):
    return x * lax.rsqrt(jnp.mean(x * x, axis=-1, keepdims=True) + EPS) * g


def _full(shape):
    nd = len(shape)
    return pl.BlockSpec(shape, lambda *_: (0,) * nd)


def _dot(a, b):
    return jnp.dot(a, b, preferred_element_type=F32)


def _pack_row(x):
    n = x.shape[-1] // 2
    lo = lax.bitcast_convert_type(x[:, :n].astype(BF16).astype(F32), U32)
    hi = lax.bitcast_convert_type(x[:, n:].astype(BF16).astype(F32), U32)
    return hi | (lo >> 16)


def _unpack_row(w):
    lo = lax.bitcast_convert_type(w << 16, F32)
    hi = lax.bitcast_convert_type(w & jnp.uint32(0xFFFF0000), F32)
    return lo, hi


ROW_SUB = 4


def _store_rows(ref, w):
    m = w.shape[0]
    for c in range(ROW_SUB):
        ref[pl.ds(c, m, stride=ROW_SUB), :] = w[:, c * LANES:(c + 1) * LANES]


def _load_rows(ref, r0, m):
    return [ref[pl.ds(r0 * ROW_SUB + c, m, stride=ROW_SUB), :] for c in range(ROW_SUB)]


def _in_proj_body(h_ref, g_ref, w_ref, wg_ref, wgt_ref,
                  zf_ref, zu_ref, zv_ref, zqk_ref, zmv_ref, zo_ref, zg_ref, zgt_ref):
    ab = _rms(h_ref[...], g_ref[...]).astype(BF16)
    off = 0
    for o_ref in (zf_ref, zu_ref, zv_ref, zqk_ref, zmv_ref, zo_ref):
        n = o_ref.shape[-1]
        o_ref[...] = _dot(ab, w_ref[:, off:off + n]).astype(o_ref.dtype)
        off += n
    zg_ref[...] = _dot(ab, wg_ref[...])
    zgt_ref[...] = lax.dot_general(wgt_ref[...], ab, (((1,), (1,)), ((), ())),
                                   preferred_element_type=F32)


def _in_proj(h, g, w_main, w_gate, w_gate_t):
    t, d = h.shape
    tm = min(TOKEN_TILE, t)
    widths = (256, 256, 256, 512, 512, 512)
    row = lambda n: pl.BlockSpec((tm, n), lambda i: (i, 0))
    out_shape = [jax.ShapeDtypeStruct((t, n), BF16) for n in widths]
    out_shape += [jax.ShapeDtypeStruct((t, LANES), F32), jax.ShapeDtypeStruct((N_GATES, t), F32)]
    out_specs = [row(n) for n in widths]
    out_specs += [row(LANES), pl.BlockSpec((N_GATES, tm), lambda i: (0, i))]
    return pl.pallas_call(
        _in_proj_body, grid=(t // tm,),
        in_specs=[row(d), _full((1, d)), _full(w_main.shape), _full(w_gate.shape), _full(w_gate_t.shape)],
        out_specs=out_specs, out_shape=out_shape,
        compiler_params=_params("parallel"), name="in_proj",
    )(h, g, w_main, w_gate, w_gate_t)


def _dft_tables(n):
    k = np.arange(n, dtype=np.int64)
    ang = 2.0 * np.pi * ((k[:, None] * k[None, :]) % n).astype(np.float64) / n
    return np.cos(ang), np.sin(ang)


def _block_diag(blocks):
    h, a, b = blocks.shape
    eye = jnp.eye(h, dtype=blocks.dtype)
    return (eye[:, None, :, None] * blocks[:, :, None, :]).reshape(h * a, h * b)


def _fourier_body(scale, row_tile, z_ref, cs_ref, ss_ref, cm_ref, sm_ref, w_ref, o_ref, p_scr, q_scr):
    z = z_ref[0]
    p_scr[...] = _dot(z, cm_ref[...]).astype(BF16)
    q_scr[...] = _dot(z, sm_ref[...]).astype(BF16)
    s = z.shape[0]
    for r in range(s // row_tile):
        rs = slice(r * row_tile, (r + 1) * row_tile)
        re = (_dot(cs_ref[rs, :], p_scr[...]) - _dot(ss_ref[rs, :], q_scr[...])) * scale
        o_ref[0, rs, :] = _dot(re.astype(BF16), w_ref[...]).astype(o_ref.dtype)


def _fourier(zf, cs, ss, cm, sm, wbd):
    b, s, w = zf.shape
    row_tile = min(512, s)
    scale = 1.0 / math.sqrt(s * FOUR_HD)
    blk = pl.BlockSpec((1, s, w), lambda i: (i, 0, 0))
    return pl.pallas_call(
        functools.partial(_fourier_body, scale, row_tile), grid=(b,),
        in_specs=[blk, _full(cs.shape), _full(ss.shape), _full(cm.shape), _full(sm.shape), _full(wbd.shape)],
        out_specs=blk, out_shape=jax.ShapeDtypeStruct((b, s, w), BF16),
        scratch_shapes=[pltpu.VMEM((s, w), BF16), pltpu.VMEM((s, w), BF16)],
        compiler_params=_params("parallel"), name="fourier",
    )(zf, cs, ss, cm, sm, wbd)


def _gmlp_body(zu_ref, zv_ref, lg_ref, lb_ref, ws_ref, bs_ref, o_ref):
    s = zu_ref.shape[1]
    w = zu_ref.shape[2]
    lane = lax.broadcasted_iota(I32, (GMLP_CHUNK, w), 1)

    def chunk(c, carry):
        r0 = pl.multiple_of(c * GMLP_CHUNK, GMLP_CHUNK)
        rows = pl.ds(r0, GMLP_CHUNK)
        v = jax.nn.gelu(zv_ref[0, rows, :].astype(F32))
        vc = v - jnp.mean(v, axis=-1, keepdims=True)
        vn = vc * lax.rsqrt(jnp.mean(vc * vc, axis=-1, keepdims=True) + EPS) * lg_ref[...] + lb_ref[...]
        acc = bs_ref[...]
        for h in range(GMLP_HEADS):
            vh = jnp.where((lane >= h * GMLP_HD) & (lane < (h + 1) * GMLP_HD), vn, 0.0).astype(BF16)
            acc = acc + _dot(ws_ref[h], vh)
        u = jax.nn.gelu(zu_ref[0, rows, :].astype(F32))
        o_ref[0, rows, :] = (u * acc).astype(o_ref.dtype)
        return carry

    lax.fori_loop(0, s // GMLP_CHUNK, chunk, 0)


def _gmlp(zu, zv, ln_g, ln_b, ws, bs_full):
    b, s, w = zu.shape
    blk = pl.BlockSpec((1, s, w), lambda i: (i, 0, 0))
    return pl.pallas_call(
        _gmlp_body, grid=(b,),
        in_specs=[blk, blk, _full((1, w)), _full((1, w)), _full(ws.shape), _full(bs_full.shape)],
        out_specs=blk, out_shape=jax.ShapeDtypeStruct((b, s, w), BF16),
        compiler_params=_params("parallel"), name="gmlp",
    )(zu, zv, ln_g, ln_b, ws, bs_full)


def _mlstm_body(qkp_ref, v_ref, zo_ref, g_ref, gt_ref, cw_ref, gb_ref, gbt_ref, ng_ref, o_ref,
                qm_scr, km_scr, hf_scr, hb_scr, c_scr, n_scr, m_scr):
    L = MLSTM_CHUNK
    H = MLSTM_HEADS
    s = v_ref.shape[1]
    nc = s // L
    qkw = H * MLSTM_DQK

    ext = L + 2 * CONV_PAD
    r_i = lax.broadcasted_iota(I32, (L, ext), 0)
    c_i = lax.broadcasted_iota(I32, (L, ext), 1)
    sh_m1 = jnp.where(c_i == r_i + CONV_PAD - 1, 1.0, 0.0).astype(BF16)
    sh_0 = jnp.where(c_i == r_i + CONV_PAD, 1.0, 0.0).astype(BF16)
    sh_p1 = jnp.where(c_i == r_i + CONV_PAD + 1, 1.0, 0.0).astype(BF16)
    lane = lax.broadcasted_iota(I32, (L, LANES), 1)
    low_half = lane < MLSTM_DQK

    def conv_chunk(c, carry):
        r0 = pl.multiple_of(c * L, L)
        xe = qkp_ref[0, pl.ds(r0, ext), :]
        y = (cw_ref[0:1, :] * _dot(sh_m1, xe) + cw_ref[1:2, :] * _dot(sh_0, xe)
             + cw_ref[2:3, :] * _dot(sh_p1, xe))
        qk = y * jax.nn.sigmoid(y)
        for h in range(H):
            keep = low_half if h % 2 == 0 else jnp.logical_not(low_half)
            t0 = (h // 2) * LANES
            q_t = qk[:, t0:t0 + LANES] * (MLSTM_DQK ** -0.5)
            k_t = qk[:, qkw + t0:qkw + t0 + LANES]
            qm_scr[h, pl.ds(r0, L), :] = jnp.where(keep, q_t, 0.0).astype(BF16)
            km_scr[h, pl.ds(r0, L), :] = jnp.where(keep, k_t, 0.0).astype(BF16)
        return carry

    lax.fori_loop(0, nc, conv_chunk, 0)

    c_scr[...] = jnp.zeros(c_scr.shape, F32)
    n_scr[...] = jnp.zeros(n_scr.shape, F32)
    m_scr[...] = jnp.zeros(m_scr.shape, F32)
    ri = lax.broadcasted_iota(I32, (L, L), 0)
    ci = lax.broadcasted_iota(I32, (L, L), 1)
    tri_l = jnp.where(ci <= ri, 1.0, 0.0)
    tri_u = jnp.where(ci >= ri, 1.0, 0.0)

    def direction(d, chunk):
        r0 = pl.multiple_of(chunk * L, L)
        rows = pl.ds(r0, L)
        gc = g_ref[0, rows, :] + gb_ref[...]
        gr = gt_ref[:, rows] + gbt_ref[:, 0:1]
        lf_c = jax.nn.log_sigmoid(gc)
        lf_r = jax.nn.log_sigmoid(gr)
        if d == 0:
            cum_c = jnp.dot(tri_l, lf_c, precision=HIGHEST, preferred_element_type=F32)
            cum_r = jnp.dot(lf_r, tri_u, precision=HIGHEST, preferred_element_type=F32)
            mask = ci <= ri
        else:
            cum_c = jnp.dot(tri_u, lf_c, precision=HIGHEST, preferred_element_type=F32)
            cum_r = jnp.dot(lf_r, tri_l, precision=HIGHEST, preferred_element_type=F32)
            mask = ci >= ri
        i_lane = 2 * d * H
        f_lane = (2 * d + 1) * H
        for h in range(H):
            u = d * H + h
            b_c = cum_c[:, f_lane + h:f_lane + h + 1]
            li_c = gc[:, i_lane + h:i_lane + h + 1]
            b_r = cum_r[f_lane + h:f_lane + h + 1, :]
            li_r = gr[i_lane + h:i_lane + h + 1, :]
            g_tot = b_c[L - 1:L, :] if d == 0 else b_c[0:1, :]
            qh = qm_scr[h, rows, :]
            kh = km_scr[h, rows, :]
            vh = v_ref[0, rows, h * MLSTM_DV:(h + 1) * MLSTM_DV]
            c_prev = c_scr[u]
            n_prev = n_scr[u]
            m_prev = m_scr[u][:, 0:1]
            dmat = jnp.where(mask, b_c - b_r + li_r, -jnp.inf)
            m_intra = jnp.max(dmat, axis=-1, keepdims=True)
            m_inter = b_c + m_prev
            m_tot = jnp.maximum(m_intra, m_inter)
            qk = lax.dot_general(qh, kh, (((1,), (1,)), ((), ())), preferred_element_type=F32)
            s_mat = jnp.exp(dmat - m_tot) * qk
            inter = jnp.exp(m_inter - m_tot)
            num = _dot(s_mat.astype(BF16), vh) + inter * _dot(qh, c_prev.astype(BF16))
            qn = jnp.sum(qh.astype(F32) * n_prev, axis=-1, keepdims=True)
            den = jnp.sum(s_mat, axis=-1, keepdims=True) + inter * qn
            hh = num / jnp.maximum(jnp.abs(den), jnp.exp(-m_tot))
            dst = hf_scr if d == 0 else hb_scr
            dst[rows, h * MLSTM_DV:(h + 1) * MLSTM_DV] = hh
            a_c = g_tot - b_c + li_c
            m_loc = jnp.max(a_c, axis=0, keepdims=True)
            w_c = jnp.exp(a_c - m_loc)
            c_loc = lax.dot_general(kh, (w_c * vh.astype(F32)).astype(BF16),
                                    (((0,), (0,)), ((), ())), preferred_element_type=F32)
            n_loc = jnp.sum(w_c * kh.astype(F32), axis=0, keepdims=True)
            m_new = jnp.maximum(g_tot + m_prev, m_loc)
            a_old = jnp.exp(g_tot + m_prev - m_new)
            a_new = jnp.exp(m_loc - m_new)
            c_scr[u] = a_old * c_prev + a_new * c_loc
            n_scr[u] = a_old * n_prev + a_new * n_loc
            m_scr[u] = jnp.broadcast_to(m_new, (1, LANES))

    def step(i, carry):
        direction(0, i)
        direction(1, nc - 1 - i)
        return carry

    lax.fori_loop(0, nc, step, 0)

    def finish(c, carry):
        r0 = pl.multiple_of(c * L, L)
        rows = pl.ds(r0, L)
        og = jax.nn.sigmoid(zo_ref[0, rows, :].astype(F32))
        hsum = og * (hf_scr[rows, :] + hb_scr[rows, :])
        for h in range(H):
            cols = slice(h * MLSTM_DV, (h + 1) * MLSTM_DV)
            x = hsum[:, cols]
            xc = x - jnp.mean(x, axis=-1, keepdims=True)
            y = xc * lax.rsqrt(jnp.mean(xc * xc, axis=-1, keepdims=True) + EPS) * ng_ref[:, cols]
            o_ref[0, rows, cols] = y.astype(o_ref.dtype)
        return carry

    lax.fori_loop(0, nc, finish, 0)


def _mlstm(zqk_pad, zmv, zo, zg, zgt, conv_w, gate_b, gate_b_t, norm_g):
    b, s, w = zmv.shape
    H = MLSTM_HEADS
    blk = pl.BlockSpec((1, s, w), lambda i: (i, 0, 0))
    units = 2 * H
    return pl.pallas_call(
        _mlstm_body, grid=(b,),
        in_specs=[pl.BlockSpec((1, s + 2 * CONV_PAD, w), lambda i: (i, 0, 0)), blk, blk,
                  pl.BlockSpec((1, s, LANES), lambda i: (i, 0, 0)),
                  pl.BlockSpec((N_GATES, s), lambda i: (0, i)),
                  _full(conv_w.shape), _full(gate_b.shape), _full(gate_b_t.shape), _full(norm_g.shape)],
        out_specs=blk, out_shape=jax.ShapeDtypeStruct((b, s, w), BF16),
        scratch_shapes=[pltpu.VMEM((H, s, LANES), BF16), pltpu.VMEM((H, s, LANES), BF16),
                        pltpu.VMEM((s, w), F32), pltpu.VMEM((s, w), F32),
                        pltpu.VMEM((units, LANES, MLSTM_DV), F32), pltpu.VMEM((units, 1, LANES), F32),
                        pltpu.VMEM((units, 1, LANES), F32)],
        compiler_params=_params("parallel"), name="mlstm",
    )(zqk_pad, zmv, zo, zg, zgt, conv_w, gate_b, gate_b_t, norm_g)


def _post_mix_body(h_ref, yf_ref, yg_ref, ym_ref, wo_ref, g2_ref, rw_ref, rb_ref, sgu_ref, sd_ref,
                   hs_ref, xw_ref, sel_ref, gate_ref, idx_ref):
    wf, wg = yf_ref.shape[-1], yg_ref.shape[-1]
    h1 = (h_ref[...] + _dot(yf_ref[...], wo_ref[0:wf, :]) + _dot(yg_ref[...], wo_ref[wf:wf + wg, :])
          + _dot(ym_ref[...], wo_ref[wf + wg:, :]))
    xn = _rms(h1, g2_ref[...])
    xb = xn.astype(BF16)
    _store_rows(xw_ref, _pack_row(xn))
    gu = _dot(xb, sgu_ref[...])
    de = gu.shape[-1] // 2
    act = (jax.nn.silu(gu[:, :de]) * gu[:, de:]).astype(BF16)
    hs_ref[...] = h1 + _dot(act, sd_ref[...])
    scores = jax.nn.sigmoid(jnp.dot(xn, rw_ref[...], precision=HIGHEST, preferred_element_type=F32))
    work = scores + rb_ref[...]
    lane = lax.broadcasted_iota(I32, scores.shape, 1)
    sel = jnp.zeros(scores.shape, F32)
    idx = jnp.zeros(scores.shape, I32)
    for k in range(TOP_K):
        m = jnp.max(work, axis=-1, keepdims=True)
        e = jnp.min(jnp.where(work == m, lane, N_EXPERTS), axis=-1, keepdims=True)
        hit = lane == e
        sel = jnp.where(hit, 1.0, sel)
        idx = jnp.where(lane == k, e, idx)
        work = jnp.where(hit, -jnp.inf, work)
    picked = sel * scores
    sel_ref[...] = sel
    gate_ref[...] = picked / jnp.sum(picked, axis=-1, keepdims=True)
    idx_ref[...] = idx


def _post_mix(h, yf, yg, ym, w_out, g2, router_w, router_b, sgu, sd):
    t, d = h.shape
    tm = min(TOKEN_TILE, t)
    row = lambda n: pl.BlockSpec((tm, n), lambda i: (i, 0))
    return pl.pallas_call(
        _post_mix_body, grid=(t // tm,),
        in_specs=[row(d), row(yf.shape[1]), row(yg.shape[1]), row(ym.shape[1]), _full(w_out.shape),
                  _full((1, d)), _full(router_w.shape), _full((1, N_EXPERTS)), _full(sgu.shape), _full(sd.shape)],
        out_specs=[row(d), pl.BlockSpec((tm * ROW_SUB, LANES), lambda i: (i, 0)),
                   row(N_EXPERTS), row(N_EXPERTS), row(N_EXPERTS)],
        out_shape=[jax.ShapeDtypeStruct((t, d), F32), jax.ShapeDtypeStruct((t * ROW_SUB, LANES), U32),
                   jax.ShapeDtypeStruct((t, N_EXPERTS), F32), jax.ShapeDtypeStruct((t, N_EXPERTS), F32),
                   jax.ShapeDtypeStruct((t, N_EXPERTS), I32)],
        compiler_params=_params("parallel"), name="post_mix",
    )(h, yf, yg, ym, w_out, g2, router_w, router_b, sgu, sd)


def _rank_body(sel_ref, rank_ref, cnt_ref, carry):
    @pl.when(pl.program_id(0) == 0)
    def _():
        carry[...] = jnp.zeros(carry.shape, F32)

    m = sel_ref[...]
    tm = m.shape[0]
    ri = lax.broadcasted_iota(I32, (tm, tm), 0)
    ci = lax.broadcasted_iota(I32, (tm, tm), 1)
    below = jnp.where(ci < ri, 1.0, 0.0).astype(BF16)
    rank = _dot(below, m.astype(BF16)) + carry[...]
    rank_ref[...] = rank.astype(I32)
    carry[...] = carry[...] + jnp.sum(m, axis=0, keepdims=True)
    cnt_ref[...] = carry[...].astype(I32)


def _rank(sel):
    t, e = sel.shape
    tm = min(TOKEN_TILE, t)
    return pl.pallas_call(
        _rank_body, grid=(t // tm,),
        in_specs=[pl.BlockSpec((tm, e), lambda i: (i, 0))],
        out_specs=[pl.BlockSpec((tm, e), lambda i: (i, 0)), _full((1, e))],
        out_shape=[jax.ShapeDtypeStruct((t, e), I32), jax.ShapeDtypeStruct((1, e), I32)],
        scratch_shapes=[pltpu.VMEM((1, e), F32)],
        compiler_params=_params("arbitrary"), name="rank",
    )(sel)


def _dest_body(rank_ref, idx_ref, gate_ref, offs_ref, dest_ref, g8_ref):
    pos = (rank_ref[...] + offs_ref[...]).astype(F32)
    gates = gate_ref[...]
    lane = lax.broadcasted_iota(I32, pos.shape, 1)
    dest = jnp.zeros(pos.shape, F32)
    g8 = jnp.zeros(pos.shape, F32)
    for k in range(TOP_K):
        hit = lane == idx_ref[:, k:k + 1]
        dk = jnp.sum(jnp.where(hit, pos, 0.0), axis=-1, keepdims=True)
        gk = jnp.sum(jnp.where(hit, gates, 0.0), axis=-1, keepdims=True)
        dest = jnp.where(lane == k, dk, dest)
        g8 = jnp.where(lane == k, gk, g8)
    dest_ref[...] = dest.astype(I32)
    g8_ref[...] = g8


def _dest(rank, idx, gates, offs):
    t, e = rank.shape
    tm = min(TOKEN_TILE, t)
    row = pl.BlockSpec((tm, e), lambda i: (i, 0))
    return pl.pallas_call(
        _dest_body, grid=(t // tm,),
        in_specs=[row, row, row, _full((1, e))],
        out_specs=[row, row],
        out_shape=[jax.ShapeDtypeStruct((t, e), I32), jax.ShapeDtypeStruct((t, e), F32)],
        compiler_params=_params("parallel"), name="dest",
    )(rank, idx, gates, offs)


def _dispatch_body(dest_ref, tail_ref, x_ref, xs_hbm, zero_scr, sem):
    n_grp = x_ref.shape[0]

    @pl.when(pl.program_id(0) == 0)
    def _():
        zero_scr[...] = jnp.zeros(zero_scr.shape, zero_scr.dtype)

        def tail_copy(e):
            start = pl.multiple_of(jnp.maximum(tail_ref[e], 0) * ROW_SUB, EXPERT_BLOCK * ROW_SUB)
            return pltpu.make_async_copy(zero_scr, xs_hbm.at[pl.ds(start, EXPERT_BLOCK * ROW_SUB), :], sem)

        def clear(e, carry):
            @pl.when(tail_ref[e] >= 0)
            def _():
                tail_copy(e).start()
            return carry

        def clear_done(e, carry):
            @pl.when(tail_ref[e] >= 0)
            def _():
                tail_copy(e).wait()
            return carry

        lax.fori_loop(0, N_EXPERTS, clear, 0)
        lax.fori_loop(0, N_EXPERTS, clear_done, 0)

    def row_copy(group, g, dst_row):
        dst = pl.ds(pl.multiple_of(dst_row * ROW_SUB, ROW_SUB), ROW_SUB)
        return pltpu.make_async_copy(x_ref.at[group, g], xs_hbm.at[dst, :], sem)

    def for_group(op):
        def body(gi, carry):
            p0 = gi * (ISSUE_GROUP * TOP_K)
            for g in range(ISSUE_GROUP):
                for k in range(TOP_K):
                    op(row_copy(gi, g, dest_ref[p0 + g * TOP_K + k]), k)
            return carry
        lax.fori_loop(0, n_grp, body, 0)

    for_group(lambda cp, k: cp.start(priority=k % 2))
    for_group(lambda cp, k: cp.wait())


def _dispatch(dest_flat, tail_start, xw, n_rows):
    t = xw.shape[0] // ROW_SUB
    tt = min(DISPATCH_TILE, t)
    x4 = xw.reshape(t // ISSUE_GROUP, ISSUE_GROUP, ROW_SUB, LANES)
    return pl.pallas_call(
        _dispatch_body, grid=(t // tt,),
        in_specs=[pl.BlockSpec((tt * TOP_K,), lambda i: (i,), memory_space=pltpu.SMEM),
                  pl.BlockSpec(memory_space=pltpu.SMEM),
                  pl.BlockSpec((tt // ISSUE_GROUP, ISSUE_GROUP, ROW_SUB, LANES), lambda i: (i, 0, 0, 0))],
        out_specs=pl.BlockSpec(memory_space=pl.ANY),
        out_shape=jax.ShapeDtypeStruct((n_rows * ROW_SUB, LANES), xw.dtype),
        scratch_shapes=[pltpu.VMEM((EXPERT_BLOCK * ROW_SUB, LANES), xw.dtype), pltpu.SemaphoreType.DMA(())],
        compiler_params=_params("arbitrary"), name="dispatch",
    )(dest_flat, tail_start, x4)


def _experts_body(blk_e_ref, n_used_ref, xs_ref, wg_ref, wu_ref, wd_ref, ys_ref, wg_scr, wu_scr, wd_scr):
    i = pl.program_id(0)
    used = i < n_used_ref[0]
    prev = blk_e_ref[jnp.maximum(i - 1, 0)]

    @pl.when(used & ((i == 0) | (blk_e_ref[i] != prev)))
    def _():
        wg_scr[...] = wg_ref[0, 0].astype(BF16)
        wu_scr[...] = wu_ref[0, 0].astype(BF16)
        wd_scr[...] = wd_ref[0, 0].astype(BF16)

    @pl.when(used)
    def _():
        halves = [_unpack_row(w) for w in _load_rows(xs_ref, 0, EXPERT_BLOCK)]
        x = jnp.concatenate([lo for lo, _ in halves] + [hi for _, hi in halves], axis=-1).astype(BF16)
        a = (jax.nn.silu(_dot(x, wg_scr[...])) * _dot(x, wu_scr[...])).astype(BF16)
        _store_rows(ys_ref, _pack_row(_dot(a, wd_scr[...])))

    @pl.when(jnp.logical_not(used))
    def _():
        ys_ref[...] = jnp.zeros(ys_ref.shape, ys_ref.dtype)


def _experts(layer, blk_e, n_used, xs, wg, wu, wd):
    blk = (EXPERT_BLOCK * ROW_SUB, LANES)
    nb = xs.shape[0] // blk[0]
    d, de = wg.shape[-2], wg.shape[-1]
    last_used = lambda i, nu: jnp.minimum(i, nu[0] - 1)
    grid_spec = pltpu.PrefetchScalarGridSpec(
        num_scalar_prefetch=2, grid=(nb,),
        in_specs=[pl.BlockSpec(blk, lambda i, be, nu: (last_used(i, nu), 0)),
                  pl.BlockSpec((1, 1, d, de), lambda i, be, nu: (layer, be[i], 0, 0)),
                  pl.BlockSpec((1, 1, d, de), lambda i, be, nu: (layer, be[i], 0, 0)),
                  pl.BlockSpec((1, 1, de, d), lambda i, be, nu: (layer, be[i], 0, 0))],
        out_specs=pl.BlockSpec(blk, lambda i, be, nu: (i, 0)),
        scratch_shapes=[pltpu.VMEM((d, de), BF16), pltpu.VMEM((d, de), BF16), pltpu.VMEM((de, d), BF16)])
    return pl.pallas_call(
        _experts_body, grid_spec=grid_spec,
        out_shape=jax.ShapeDtypeStruct(xs.shape, xs.dtype),
        compiler_params=_params("arbitrary"), name="experts",
    )(blk_e, n_used, xs, wg, wu, wd)


def _combine_body(dest_ref, g8_ref, hs_ref, ys_hbm, o_ref, buf, sem):
    n_tok = hs_ref.shape[0]

    n_grp = n_tok // ISSUE_GROUP

    def row_copy(src_row, k, gi, g):
        src = pl.ds(pl.multiple_of(src_row * ROW_SUB, ROW_SUB), ROW_SUB)
        slot = (k * n_grp + gi) * (ISSUE_GROUP * ROW_SUB) + g * ROW_SUB
        return pltpu.make_async_copy(ys_hbm.at[src, :], buf.at[pl.ds(pl.multiple_of(slot, ROW_SUB), ROW_SUB), :], sem)

    def for_group(op):
        def body(gi, carry):
            p0 = gi * (ISSUE_GROUP * TOP_K)
            for g in range(ISSUE_GROUP):
                for k in range(TOP_K):
                    op(row_copy(dest_ref[p0 + g * TOP_K + k], k, gi, g), k)
            return carry
        lax.fori_loop(0, n_grp, body, 0)

    for_group(lambda cp, k: cp.start(priority=k % 2))
    for_group(lambda cp, k: cp.wait())

    half = ROW_SUB * LANES

    def reduce(gi, carry):
        r0 = pl.multiple_of(gi * ISSUE_GROUP, ISSUE_GROUP)
        rows = pl.ds(r0, ISSUE_GROUP)
        gates = [g8_ref[rows, k:k + 1] for k in range(TOP_K)]
        picked = [_load_rows(buf, k * n_tok + r0, ISSUE_GROUP) for k in range(TOP_K)]
        for c in range(ROW_SUB):
            lo_cols = slice(c * LANES, (c + 1) * LANES)
            hi_cols = slice(half + c * LANES, half + (c + 1) * LANES)
            acc_lo = hs_ref[rows, lo_cols]
            acc_hi = hs_ref[rows, hi_cols]
            for k in range(TOP_K):
                lo, hi = _unpack_row(picked[k][c])
                acc_lo = acc_lo + gates[k] * lo
                acc_hi = acc_hi + gates[k] * hi
            o_ref[rows, lo_cols] = acc_lo
            o_ref[rows, hi_cols] = acc_hi
        return carry

    lax.fori_loop(0, n_grp, reduce, 0)


def _combine(dest_flat, g8, hs, ys):
    t, d = hs.shape
    tt = min(COMBINE_TILE, t)
    return pl.pallas_call(
        _combine_body, grid=(t // tt,),
        in_specs=[pl.BlockSpec((tt * TOP_K,), lambda i: (i,), memory_space=pltpu.SMEM),
                  pl.BlockSpec((tt, g8.shape[1]), lambda i: (i, 0)),
                  pl.BlockSpec((tt, d), lambda i: (i, 0)),
                  pl.BlockSpec(memory_space=pl.ANY)],
        out_specs=pl.BlockSpec((tt, d), lambda i: (i, 0)),
        out_shape=jax.ShapeDtypeStruct((t, d), F32),
        scratch_shapes=[pltpu.VMEM((TOP_K * tt * ROW_SUB, LANES), ys.dtype), pltpu.SemaphoreType.DMA(())],
        compiler_params=_params("arbitrary"), name="combine",
    )(dest_flat, g8, hs, ys)


def _ple_body(final, h2_ref, p_ref, win_ref, wgate_ref, g_ref, gf_ref, o_ref):
    h2 = h2_ref[...]
    e = _dot(p_ref[...].astype(BF16), win_ref[...])
    gate = jax.nn.sigmoid(_dot(h2.astype(BF16), wgate_ref[...]))
    h3 = h2 + _rms(gate * e, g_ref[...])
    o_ref[...] = _rms(h3, gf_ref[...]) if final else h3


def _ple(h2, p, w_in, w_gate, g, g_final, final):
    t, d = h2.shape
    tm = min(TOKEN_TILE, t)
    row = lambda n: pl.BlockSpec((tm, n), lambda i: (i, 0))
    return pl.pallas_call(
        functools.partial(_ple_body, final), grid=(t // tm,),
        in_specs=[row(d), row(p.shape[1]), _full(w_in.shape), _full(w_gate.shape), _full((1, d)), _full((1, d))],
        out_specs=row(d), out_shape=jax.ShapeDtypeStruct((t, d), F32),
        compiler_params=_params("parallel"), name="ple",
    )(h2, p, w_in, w_gate, g, g_final)


def _mixers(h, bsz, seq, norm1_g, w_in, four_w, ln_g, ln_b, ws, bs, conv_w, gate_b, norm_g, tables):
    d = h.shape[1]
    n_main = w_in.shape[1] - N_GATES
    w_main = w_in[:, :n_main].astype(BF16)
    w_g = w_in[:, n_main:]
    w_gate = jnp.pad(w_g, ((0, 0), (0, LANES - N_GATES))).astype(BF16)
    w_gate_t = w_g.T.astype(BF16)
    zf, zu, zv, zqk, zmv, zo, zg, zgt = _in_proj(h, norm1_g.reshape(1, d), w_main, w_gate, w_gate_t)
    b3 = lambda a: a.reshape(bsz, seq, a.shape[-1])

    cs, ss, cm, sm = tables
    yf = _fourier(b3(zf), cs, ss, cm, sm, _block_diag(four_w).astype(BF16))

    bs_full = jnp.repeat(bs.T, GMLP_HD, axis=1)
    yg = _gmlp(b3(zu), b3(zv), ln_g.reshape(1, -1), ln_b.reshape(1, -1), ws.astype(BF16), bs_full)

    zqk_pad = jnp.pad(b3(zqk), ((0, 0), (CONV_PAD, CONV_PAD), (0, 0)))
    gate_b_c = jnp.pad(gate_b, (0, LANES - N_GATES)).reshape(1, LANES)
    gate_b_t = jnp.broadcast_to(gate_b.reshape(N_GATES, 1), (N_GATES, LANES))
    ym = _mlstm(zqk_pad, b3(zmv), b3(zo), b3(zg), zgt, conv_w, gate_b_c, gate_b_t, norm_g.reshape(1, -1))
    t = bsz * seq
    return yf.reshape(t, -1), yg.reshape(t, -1), ym.reshape(t, -1)


def _moe(layer, h, yf, yg, ym, w_out, norm2_g, router_w, router_b, wg, wu, wd, sg, su, sd):
    t, d = h.shape
    sgu = jnp.concatenate([sg, su], axis=1).astype(BF16)
    hs, xw, sel, gates, idx = _post_mix(h, yf, yg, ym, w_out.astype(BF16), norm2_g.reshape(1, d),
                                        router_w, router_b.reshape(1, -1), sgu, sd.astype(BF16))
    rank, counts = _rank(sel)
    counts = counts[0]
    pcounts = ((counts + EXPERT_BLOCK - 1) // EXPERT_BLOCK) * EXPERT_BLOCK
    pend = jnp.cumsum(pcounts)
    offs = (pend - pcounts).astype(I32)
    tail_start = jnp.where(pcounts > 0, pend - EXPERT_BLOCK, -1).astype(I32)
    n_blk = (t * TOP_K) // EXPERT_BLOCK + N_EXPERTS
    blk_start = jnp.arange(n_blk, dtype=I32) * EXPERT_BLOCK
    blk_e = jnp.minimum(jnp.sum(pend[None, :] <= blk_start[:, None], axis=1), N_EXPERTS - 1).astype(I32)
    n_used = (pend[-1:] // EXPERT_BLOCK).astype(I32)

    dest, g8 = _dest(rank, idx, gates, offs.reshape(1, -1))
    dest_flat = dest[:, :TOP_K].reshape(-1)
    xs = _dispatch(dest_flat, tail_start, xw, n_blk * EXPERT_BLOCK)
    ys = _experts(layer, blk_e, n_used, xs, wg, wu, wd)
    return _combine(dest_flat, g8, hs, ys)


def kernel(x, p, norm1_g, w_in, four_w, gmlp_ln_g, gmlp_ln_b, gmlp_ws, gmlp_bs, mlstm_conv_w,
           mlstm_gate_b, mlstm_norm_g, w_out, norm2_g, router_w, router_b, exp_w_gate, exp_w_up,
           exp_w_down, sh_w_gate, sh_w_up, sh_w_down, ple_w_in, ple_w_gate, ple_norm_g, final_norm_g):
    bsz, seq, d = x.shape
    depth = w_in.shape[0]
    t = bsz * seq
    cos_s, sin_s = _dft_tables(seq)
    cos_m, sin_m = _dft_tables(FOUR_HD)
    eye = np.eye(FOUR_HEADS)
    tables = (jnp.asarray(cos_s, BF16), jnp.asarray(sin_s, BF16),
              jnp.asarray(np.kron(eye, cos_m), BF16), jnp.asarray(np.kron(eye, sin_m), BF16))
    h = x.reshape(t, d)
    for i in range(depth):
        yf, yg, ym = _mixers(h, bsz, seq, norm1_g[i], w_in[i], four_w[i], gmlp_ln_g[i], gmlp_ln_b[i],
                             gmlp_ws[i], gmlp_bs[i], mlstm_conv_w[i], mlstm_gate_b[i], mlstm_norm_g[i], tables)
        h2 = _moe(i, h, yf, yg, ym, w_out[i], norm2_g[i], router_w[i], router_b[i],
                  exp_w_gate, exp_w_up, exp_w_down, sh_w_gate[i], sh_w_up[i], sh_w_down[i])
        h = _ple(h2, p[i].reshape(t, -1), ple_w_in[i].astype(BF16), ple_w_gate[i].astype(BF16),
                 ple_norm_g[i].reshape(1, d), final_norm_g.reshape(1, d), i == depth - 1)
    return h.reshape(bsz, seq, d)
```

```python
import functools
import math

import numpy as np
import jax
import jax.numpy as jnp
from jax import lax
from jax.experimental import pallas as pl
from jax.experimental.pallas import tpu as pltpu

F32, BF16, I32, U32 = jnp.float32, jnp.bfloat16, jnp.int32, jnp.uint32
HIGHEST = lax.Precision.HIGHEST

EPS = 1e-6
LANES = 128
VMEM_LIMIT_BYTES = 48 * 1024 * 1024

FOUR_HEADS, FOUR_HD = 4, 64
GMLP_HEADS, GMLP_HD, GMLP_CHUNK = 4, 64, 128
MLSTM_HEADS, MLSTM_DV, MLSTM_DQK, MLSTM_CHUNK = 4, 128, 64, 128
CONV_PAD = 16
N_GATES = 16
N_EXPERTS, TOP_K = 128, 8
EXPERT_BLOCK = 512
TOKEN_TILE = 512
DISPATCH_TILE = 256
COMBINE_TILE = 256
ISSUE_GROUP = 8


def _params(*sem):
    return pltpu.CompilerParams(dimension_semantics=sem, vmem_limit_bytes=VMEM_LIMIT_BYTES)


def _rms(x, g):
    return x * lax.rsqrt(jnp.mean(x * x, axis=-1, keepdims=True) + EPS) * g


def _full(shape):
    nd = len(shape)
    return pl.BlockSpec(shape, lambda *_: (0,) * nd)


def _dot(a, b):
    return jnp.dot(a, b, preferred_element_type=F32)


def _pack_row(x):
    n = x.shape[-1] // 2
    lo = lax.bitcast_convert_type(x[:, :n].astype(BF16).astype(F32), U32)
    hi = lax.bitcast_convert_type(x[:, n:].astype(BF16).astype(F32), U32)
    return hi | (lo >> 16)


def _unpack_row(w):
    lo = lax.bitcast_convert_type(w << 16, F32)
    hi = lax.bitcast_convert_type(w & jnp.uint32(0xFFFF0000), F32)
    return lo, hi


ROW_SUB = 4


def _store_rows(ref, w):
    m = w.shape[0]
    for c in range(ROW_SUB):
        ref[pl.ds(c, m, stride=ROW_SUB), :] = w[:, c * LANES:(c + 1) * LANES]


def _load_rows(ref, r0, m):
    return [ref[pl.ds(r0 * ROW_SUB + c, m, stride=ROW_SUB), :] for c in range(ROW_SUB)]


def _in_proj_body(h_ref, g_ref, w_ref, wg_ref, wgt_ref,
                  zf_ref, zu_ref, zv_ref, zqk_ref, zmv_ref, zo_ref, zg_ref, zgt_ref):
    ab = _rms(h_ref[...], g_ref[...]).astype(BF16)
    off = 0
    for o_ref in (zf_ref, zu_ref, zv_ref, zqk_ref, zmv_ref, zo_ref):
        n = o_ref.shape[-1]
        o_ref[...] = _dot(ab, w_ref[:, off:off + n]).astype(o_ref.dtype)
        off += n
    zg_ref[...] = _dot(ab, wg_ref[...])
    zgt_ref[...] = lax.dot_general(wgt_ref[...], ab, (((1,), (1,)), ((), ())),
                                   preferred_element_type=F32)


def _in_proj(h, g, w_main, w_gate, w_gate_t):
    t, d = h.shape
    tm = min(TOKEN_TILE, t)
    widths = (256, 256, 256, 512, 512, 512)
    row = lambda n: pl.BlockSpec((tm, n), lambda i: (i, 0))
    out_shape = [jax.ShapeDtypeStruct((t, n), BF16) for n in widths]
    out_shape += [jax.ShapeDtypeStruct((t, LANES), F32), jax.ShapeDtypeStruct((N_GATES, t), F32)]
    out_specs = [row(n) for n in widths]
    out_specs += [row(LANES), pl.BlockSpec((N_GATES, tm), lambda i: (0, i))]
    return pl.pallas_call(
        _in_proj_body, grid=(t // tm,),
        in_specs=[row(d), _full((1, d)), _full(w_main.shape), _full(w_gate.shape), _full(w_gate_t.shape)],
        out_specs=out_specs, out_shape=out_shape,
        compiler_params=_params("parallel"), name="in_proj",
    )(h, g, w_main, w_gate, w_gate_t)


def _dft_tables(n):
    k = np.arange(n, dtype=np.int64)
    ang = 2.0 * np.pi * ((k[:, None] * k[None, :]) % n).astype(np.float64) / n
    return np.cos(ang), np.sin(ang)


def _block_diag(blocks):
    h, a, b = blocks.shape
    eye = jnp.eye(h, dtype=blocks.dtype)
    return (eye[:, None, :, None] * blocks[:, :, None, :]).reshape(h * a, h * b)


def _fourier_body(scale, row_tile, z_ref, cs_ref, ss_ref, cm_ref, sm_ref, w_ref, o_ref, p_scr, q_scr):
    z = z_ref[0]
    p_scr[...] = _dot(z, cm_ref[...]).astype(BF16)
    q_scr[...] = _dot(z, sm_ref[...]).astype(BF16)
    s = z.shape[0]
    for r in range(s // row_tile):
        rs = slice(r * row_tile, (r + 1) * row_tile)
        re = (_dot(cs_ref[rs, :], p_scr[...]) - _dot(ss_ref[rs, :], q_scr[...])) * scale
        o_ref[0, rs, :] = _dot(re.astype(BF16), w_ref[...]).astype(o_ref.dtype)


def _fourier(zf, cs, ss, cm, sm, wbd):
    b, s, w = zf.shape
    row_tile = min(512, s)
    scale = 1.0 / math.sqrt(s * FOUR_HD)
    blk = pl.BlockSpec((1, s, w), lambda i: (i, 0, 0))
    return pl.pallas_call(
        functools.partial(_fourier_body, scale, row_tile), grid=(b,),
        in_specs=[blk, _full(cs.shape), _full(ss.shape), _full(cm.shape), _full(sm.shape), _full(wbd.shape)],
        out_specs=blk, out_shape=jax.ShapeDtypeStruct((b, s, w), BF16),
        scratch_shapes=[pltpu.VMEM((s, w), BF16), pltpu.VMEM((s, w), BF16)],
        compiler_params=_params("parallel"), name="fourier",
    )(zf, cs, ss, cm, sm, wbd)


def _gmlp_body(zu_ref, zv_ref, lg_ref, lb_ref, ws_ref, bs_ref, o_ref):
    s = zu_ref.shape[1]
    w = zu_ref.shape[2]
    lane = lax.broadcasted_iota(I32, (GMLP_CHUNK, w), 1)

    def chunk(c, carry):
        r0 = pl.multiple_of(c * GMLP_CHUNK, GMLP_CHUNK)
        rows = pl.ds(r0, GMLP_CHUNK)
        v = jax.nn.gelu(zv_ref[0, rows, :].astype(F32))
        vc = v - jnp.mean(v, axis=-1, keepdims=True)
        vn = vc * lax.rsqrt(jnp.mean(vc * vc, axis=-1, keepdims=True) + EPS) * lg_ref[...] + lb_ref[...]
        acc = bs_ref[...]
        for h in range(GMLP_HEADS):
            vh = jnp.where((lane >= h * GMLP_HD) & (lane < (h + 1) * GMLP_HD), vn, 0.0).astype(BF16)
            acc = acc + _dot(ws_ref[h], vh)
        u = jax.nn.gelu(zu_ref[0, rows, :].astype(F32))
        o_ref[0, rows, :] = (u * acc).astype(o_ref.dtype)
        return carry

    lax.fori_loop(0, s // GMLP_CHUNK, chunk, 0)


def _gmlp(zu, zv, ln_g, ln_b, ws, bs_full):
    b, s, w = zu.shape
    blk = pl.BlockSpec((1, s, w), lambda i: (i, 0, 0))
    return pl.pallas_call(
        _gmlp_body, grid=(b,),
        in_specs=[blk, blk, _full((1, w)), _full((1, w)), _full(ws.shape), _full(bs_full.shape)],
        out_specs=blk, out_shape=jax.ShapeDtypeStruct((b, s, w), BF16),
        compiler_params=_params("parallel"), name="gmlp",
    )(zu, zv, ln_g, ln_b, ws, bs_full)


def _mlstm_body(qkp_ref, v_ref, zo_ref, g_ref, gt_ref, cw_ref, gb_ref, gbt_ref, ng_ref, o_ref,
                qm_scr, km_scr, hf_scr, hb_scr, c_scr, n_scr, m_scr):
    L = MLSTM_CHUNK
    H = MLSTM_HEADS
    s = v_ref.shape[1]
    nc = s // L
    qkw = H * MLSTM_DQK

    ext = L + 2 * CONV_PAD
    r_i = lax.broadcasted_iota(I32, (L, ext), 0)
    c_i = lax.broadcasted_iota(I32, (L, ext), 1)
    sh_m1 = jnp.where(c_i == r_i + CONV_PAD - 1, 1.0, 0.0).astype(BF16)
    sh_0 = jnp.where(c_i == r_i + CONV_PAD, 1.0, 0.0).astype(BF16)
    sh_p1 = jnp.where(c_i == r_i + CONV_PAD + 1, 1.0, 0.0).astype(BF16)
    lane = lax.broadcasted_iota(I32, (L, LANES), 1)
    low_half = lane < MLSTM_DQK

    def conv_chunk(c, carry):
        r0 = pl.multiple_of(c * L, L)
        xe = qkp_ref[0, pl.ds(r0, ext), :]
        y = (cw_ref[0:1, :] * _dot(sh_m1, xe) + cw_ref[1:2, :] * _dot(sh_0, xe)
             + cw_ref[2:3, :] * _dot(sh_p1, xe))
        qk = y * jax.nn.sigmoid(y)
        for h in range(H):
            keep = low_half if h % 2 == 0 else jnp.logical_not(low_half)
            t0 = (h // 2) * LANES
            q_t = qk[:, t0:t0 + LANES] * (MLSTM_DQK ** -0.5)
            k_t = qk[:, qkw + t0:qkw + t0 + LANES]
            qm_scr[h, pl.ds(r0, L), :] = jnp.where(keep, q_t, 0.0).astype(BF16)
            km_scr[h, pl.ds(r0, L), :] = jnp.where(keep, k_t, 0.0).astype(BF16)
        return carry

    lax.fori_loop(0, nc, conv_chunk, 0)

    c_scr[...] = jnp.zeros(c_scr.shape, F32)
    n_scr[...] = jnp.zeros(n_scr.shape, F32)
    m_scr[...] = jnp.zeros(m_scr.shape, F32)
    ri = lax.broadcasted_iota(I32, (L, L), 0)
    ci = lax.broadcasted_iota(I32, (L, L), 1)
    tri_l = jnp.where(ci <= ri, 1.0, 0.0)
    tri_u = jnp.where(ci >= ri, 1.0, 0.0)

    def direction(d, chunk):
        r0 = pl.multiple_of(chunk * L, L)
        rows = pl.ds(r0, L)
        gc = g_ref[0, rows, :] + gb_ref[...]
        gr = gt_ref[:, rows] + gbt_ref[:, 0:1]
        lf_c = jax.nn.log_sigmoid(gc)
        lf_r = jax.nn.log_sigmoid(gr)
        if d == 0:
            cum_c = jnp.dot(tri_l, lf_c, precision=HIGHEST, preferred_element_type=F32)
            cum_r = jnp.dot(lf_r, tri_u, precision=HIGHEST, preferred_element_type=F32)
            mask = ci <= ri
        else:
            cum_c = jnp.dot(tri_u, lf_c, precision=HIGHEST, preferred_element_type=F32)
            cum_r = jnp.dot(lf_r, tri_l, precision=HIGHEST, preferred_element_type=F32)
            mask = ci >= ri
        i_lane = 2 * d * H
        f_lane = (2 * d + 1) * H
        for h in range(H):
            u = d * H + h
            b_c = cum_c[:, f_lane + h:f_lane + h + 1]
            li_c = gc[:, i_lane + h:i_lane + h + 1]
            b_r = cum_r[f_lane + h:f_lane + h + 1, :]
            li_r = gr[i_lane + h:i_lane + h + 1, :]
            g_tot = b_c[L - 1:L, :] if d == 0 else b_c[0:1, :]
            qh = qm_scr[h, rows, :]
            kh = km_scr[h, rows, :]
            vh = v_ref[0, rows, h * MLSTM_DV:(h + 1) * MLSTM_DV]
            c_prev = c_scr[u]
            n_prev = n_scr[u]
            m_prev = m_scr[u][:, 0:1]
            dmat = jnp.where(mask, b_c - b_r + li_r, -jnp.inf)
            m_intra = jnp.max(dmat, axis=-1, keepdims=True)
            m_inter = b_c + m_prev
            m_tot = jnp.maximum(m_intra, m_inter)
            qk = lax.dot_general(qh, kh, (((1,), (1,)), ((), ())), preferred_element_type=F32)
            s_mat = jnp.exp(dmat - m_tot) * qk
            inter = jnp.exp(m_inter - m_tot)
            num = _dot(s_mat.astype(BF16), vh) + inter * _dot(qh, c_prev.astype(BF16))
            qn = jnp.sum(qh.astype(F32) * n_prev, axis=-1, keepdims=True)
            den = jnp.sum(s_mat, axis=-1, keepdims=True) + inter * qn
            hh = num / jnp.maximum(jnp.abs(den), jnp.exp(-m_tot))
            dst = hf_scr if d == 0 else hb_scr
            dst[rows, h * MLSTM_DV:(h + 1) * MLSTM_DV] = hh
            a_c = g_tot - b_c + li_c
            m_loc = jnp.max(a_c, axis=0, keepdims=True)
            w_c = jnp.exp(a_c - m_loc)
            c_loc = lax.dot_general(kh, (w_c * vh.astype(F32)).astype(BF16),
                                    (((0,), (0,)), ((), ())), preferred_element_type=F32)
            n_loc = jnp.sum(w_c * kh.astype(F32), axis=0, keepdims=True)
            m_new = jnp.maximum(g_tot + m_prev, m_loc)
            a_old = jnp.exp(g_tot + m_prev - m_new)
            a_new = jnp.exp(m_loc - m_new)
            c_scr[u] = a_old * c_prev + a_new * c_loc
            n_scr[u] = a_old * n_prev + a_new * n_loc
            m_scr[u] = jnp.broadcast_to(m_new, (1, LANES))

    def step(i, carry):
        direction(0, i)
        direction(1, nc - 1 - i)
        return carry

    lax.fori_loop(0, nc, step, 0)

    def finish(c, carry):
        r0 = pl.multiple_of(c * L, L)
        rows = pl.ds(r0, L)
        og = jax.nn.sigmoid(zo_ref[0, rows, :].astype(F32))
        hsum = og * (hf_scr[rows, :] + hb_scr[rows, :])
        for h in range(H):
            cols = slice(h * MLSTM_DV, (h + 1) * MLSTM_DV)
            x = hsum[:, cols]
            xc = x - jnp.mean(x, axis=-1, keepdims=True)
            y = xc * lax.rsqrt(jnp.mean(xc * xc, axis=-1, keepdims=True) + EPS) * ng_ref[:, cols]
            o_ref[0, rows, cols] = y.astype(o_ref.dtype)
        return carry

    lax.fori_loop(0, nc, finish, 0)


def _mlstm(zqk_pad, zmv, zo, zg, zgt, conv_w, gate_b, gate_b_t, norm_g):
    b, s, w = zmv.shape
    H = MLSTM_HEADS
    blk = pl.BlockSpec((1, s, w), lambda i: (i, 0, 0))
    units = 2 * H
    return pl.pallas_call(
        _mlstm_body, grid=(b,),
        in_specs=[pl.BlockSpec((1, s + 2 * CONV_PAD, w), lambda i: (i, 0, 0)), blk, blk,
                  pl.BlockSpec((1, s, LANES), lambda i: (i, 0, 0)),
                  pl.BlockSpec((N_GATES, s), lambda i: (0, i)),
                  _full(conv_w.shape), _full(gate_b.shape), _full(gate_b_t.shape), _full(norm_g.shape)],
        out_specs=blk, out_shape=jax.ShapeDtypeStruct((b, s, w), BF16),
        scratch_shapes=[pltpu.VMEM((H, s, LANES), BF16), pltpu.VMEM((H, s, LANES), BF16),
                        pltpu.VMEM((s, w), F32), pltpu.VMEM((s, w), F32),
                        pltpu.VMEM((units, LANES, MLSTM_DV), F32), pltpu.VMEM((units, 1, LANES), F32),
                        pltpu.VMEM((units, 1, LANES), F32)],
        compiler_params=_params("parallel"), name="mlstm",
    )(zqk_pad, zmv, zo, zg, zgt, conv_w, gate_b, gate_b_t, norm_g)


def _post_mix_body(h_ref, yf_ref, yg_ref, ym_ref, wo_ref, g2_ref, rw_ref, rb_ref, sgu_ref, sd_ref,
                   hs_ref, xw_ref, sel_ref, gate_ref, idx_ref):
    wf, wg = yf_ref.shape[-1], yg_ref.shape[-1]
    h1 = (h_ref[...] + _dot(yf_ref[...], wo_ref[0:wf, :]) + _dot(yg_ref[...], wo_ref[wf:wf + wg, :])
          + _dot(ym_ref[...], wo_ref[wf + wg:, :]))
    xn = _rms(h1, g2_ref[...])
    xb = xn.astype(BF16)
    _store_rows(xw_ref, _pack_row(xn))
    gu = _dot(xb, sgu_ref[...])
    de = gu.shape[-1] // 2
    act = (jax.nn.silu(gu[:, :de]) * gu[:, de:]).astype(BF16)
    hs_ref[...] = h1 + _dot(act, sd_ref[...])
    scores = jax.nn.sigmoid(jnp.dot(xn, rw_ref[...], precision=HIGHEST, preferred_element_type=F32))
    work = scores + rb_ref[...]
    lane = lax.broadcasted_iota(I32, scores.shape, 1)
    sel = jnp.zeros(scores.shape, F32)
    idx = jnp.zeros(scores.shape, I32)
    for k in range(TOP_K):
        m = jnp.max(work, axis=-1, keepdims=True)
        e = jnp.min(jnp.where(work == m, lane, N_EXPERTS), axis=-1, keepdims=True)
        hit = lane == e
        sel = jnp.where(hit, 1.0, sel)
        idx = jnp.where(lane == k, e, idx)
        work = jnp.where(hit, -jnp.inf, work)
    picked = sel * scores
    sel_ref[...] = sel
    gate_ref[...] = picked / jnp.sum(picked, axis=-1, keepdims=True)
    idx_ref[...] = idx


def _post_mix(h, yf, yg, ym, w_out, g2, router_w, router_b, sgu, sd):
    t, d = h.shape
    tm = min(TOKEN_TILE, t)
    row = lambda n: pl.BlockSpec((tm, n), lambda i: (i, 0))
    return pl.pallas_call(
        _post_mix_body, grid=(t // tm,),
        in_specs=[row(d), row(yf.shape[1]), row(yg.shape[1]), row(ym.shape[1]), _full(w_out.shape),
                  _full((1, d)), _full(router_w.shape), _full((1, N_EXPERTS)), _full(sgu.shape), _full(sd.shape)],
        out_specs=[row(d), pl.BlockSpec((tm * ROW_SUB, LANES), lambda i: (i, 0)),
                   row(N_EXPERTS), row(N_EXPERTS), row(N_EXPERTS)],
        out_shape=[jax.ShapeDtypeStruct((t, d), F32), jax.ShapeDtypeStruct((t * ROW_SUB, LANES), U32),
                   jax.ShapeDtypeStruct((t, N_EXPERTS), F32), jax.ShapeDtypeStruct((t, N_EXPERTS), F32),
                   jax.ShapeDtypeStruct((t, N_EXPERTS), I32)],
        compiler_params=_params("parallel"), name="post_mix",
    )(h, yf, yg, ym, w_out, g2, router_w, router_b, sgu, sd)


def _rank_body(sel_ref, rank_ref, cnt_ref, carry):
    @pl.when(pl.program_id(0) == 0)
    def _():
        carry[...] = jnp.zeros(carry.shape, F32)

    m = sel_ref[...]
    tm = m.shape[0]
    ri = lax.broadcasted_iota(I32, (tm, tm), 0)
    ci = lax.broadcasted_iota(I32, (tm, tm), 1)
    below = jnp.where(ci < ri, 1.0, 0.0).astype(BF16)
    rank = _dot(below, m.astype(BF16)) + carry[...]
    rank_ref[...] = rank.astype(I32)
    carry[...] = carry[...] + jnp.sum(m, axis=0, keepdims=True)
    cnt_ref[...] = carry[...].astype(I32)


def _rank(sel):
    t, e = sel.shape
    tm = min(TOKEN_TILE, t)
    return pl.pallas_call(
        _rank_body, grid=(t // tm,),
        in_specs=[pl.BlockSpec((tm, e), lambda i: (i, 0))],
        out_specs=[pl.BlockSpec((tm, e), lambda i: (i, 0)), _full((1, e))],
        out_shape=[jax.ShapeDtypeStruct((t, e), I32), jax.ShapeDtypeStruct((1, e), I32)],
        scratch_shapes=[pltpu.VMEM((1, e), F32)],
        compiler_params=_params("arbitrary"), name="rank",
    )(sel)


def _dest_body(rank_ref, idx_ref, gate_ref, offs_ref, dest_ref, g8_ref):
    pos = (rank_ref[...] + offs_ref[...]).astype(F32)
    gates = gate_ref[...]
    lane = lax.broadcasted_iota(I32, pos.shape, 1)
    dest = jnp.zeros(pos.shape, F32)
    g8 = jnp.zeros(pos.shape, F32)
    for k in range(TOP_K):
        hit = lane == idx_ref[:, k:k + 1]
        dk = jnp.sum(jnp.where(hit, pos, 0.0), axis=-1, keepdims=True)
        gk = jnp.sum(jnp.where(hit, gates, 0.0), axis=-1, keepdims=True)
        dest = jnp.where(lane == k, dk, dest)
        g8 = jnp.where(lane == k, gk, g8)
    dest_ref[...] = dest.astype(I32)
    g8_ref[...] = g8


def _dest(rank, idx, gates, offs):
    t, e = rank.shape
    tm = min(TOKEN_TILE, t)
    row = pl.BlockSpec((tm, e), lambda i: (i, 0))
    return pl.pallas_call(
        _dest_body, grid=(t // tm,),
        in_specs=[row, row, row, _full((1, e))],
        out_specs=[row, row],
        out_shape=[jax.ShapeDtypeStruct((t, e), I32), jax.ShapeDtypeStruct((t, e), F32)],
        compiler_params=_params("parallel"), name="dest",
    )(rank, idx, gates, offs)


def _dispatch_body(dest_ref, tail_ref, x_ref, xs_hbm, zero_scr, sem):
    n_grp = x_ref.shape[0]

    @pl.when(pl.program_id(0) == 0)
    def _():
        zero_scr[...] = jnp.zeros(zero_scr.shape, zero_scr.dtype)

        def tail_copy(e):
            start = pl.multiple_of(jnp.maximum(tail_ref[e], 0) * ROW_SUB, EXPERT_BLOCK * ROW_SUB)
            return pltpu.make_async_copy(zero_scr, xs_hbm.at[pl.ds(start, EXPERT_BLOCK * ROW_SUB), :], sem)

        def clear(e, carry):
            @pl.when(tail_ref[e] >= 0)
            def _():
                tail_copy(e).start()
            return carry

        def clear_done(e, carry):
            @pl.when(tail_ref[e] >= 0)
            def _():
                tail_copy(e).wait()
            return carry

        lax.fori_loop(0, N_EXPERTS, clear, 0)
        lax.fori_loop(0, N_EXPERTS, clear_done, 0)

    def row_copy(group, g, dst_row):
        dst = pl.ds(pl.multiple_of(dst_row * ROW_SUB, ROW_SUB), ROW_SUB)
        return pltpu.make_async_copy(x_ref.at[group, g], xs_hbm.at[dst, :], sem)

    def for_group(op):
        def body(gi, carry):
            p0 = gi * (ISSUE_GROUP * TOP_K)
            for g in range(ISSUE_GROUP):
                for k in range(TOP_K):
                    op(row_copy(gi, g, dest_ref[p0 + g * TOP_K + k]), k)
            return carry
        lax.fori_loop(0, n_grp, body, 0)

    for_group(lambda cp, k: cp.start(priority=k % 2))
    for_group(lambda cp, k: cp.wait())


def _dispatch(dest_flat, tail_start, xw, n_rows):
    t = xw.shape[0] // ROW_SUB
    tt = min(DISPATCH_TILE, t)
    x4 = xw.reshape(t // ISSUE_GROUP, ISSUE_GROUP, ROW_SUB, LANES)
    return pl.pallas_call(
        _dispatch_body, grid=(t // tt,),
        in_specs=[pl.BlockSpec((tt * TOP_K,), lambda i: (i,), memory_space=pltpu.SMEM),
                  pl.BlockSpec(memory_space=pltpu.SMEM),
                  pl.BlockSpec((tt // ISSUE_GROUP, ISSUE_GROUP, ROW_SUB, LANES), lambda i: (i, 0, 0, 0))],
        out_specs=pl.BlockSpec(memory_space=pl.ANY),
        out_shape=jax.ShapeDtypeStruct((n_rows * ROW_SUB, LANES), xw.dtype),
        scratch_shapes=[pltpu.VMEM((EXPERT_BLOCK * ROW_SUB, LANES), xw.dtype), pltpu.SemaphoreType.DMA(())],
        compiler_params=_params("arbitrary"), name="dispatch",
    )(dest_flat, tail_start, x4)


def _experts_body(layer, blk_e_ref, n_used_ref, first_ref, next_e_ref, xs_ref, wg_hbm, wu_hbm, wd_hbm, ys_ref,
                  wg_buf, wu_buf, wd_buf, wg_scr, wu_scr, wd_scr, slot_ref, sems):
    i = pl.program_id(0)
    used = i < n_used_ref[0]

    def weight_copies(e, slot):
        return [pltpu.make_async_copy(hbm.at[layer, e], buf.at[slot], sems.at[slot])
                for hbm, buf in ((wg_hbm, wg_buf), (wu_hbm, wu_buf), (wd_hbm, wd_buf))]

    @pl.when(i == 0)
    def _():
        slot_ref[0] = 0
        for cp in weight_copies(blk_e_ref[0], 0):
            cp.start()

    @pl.when(first_ref[i] == 1)
    def _():
        slot = slot_ref[0]
        for cp in weight_copies(blk_e_ref[i], slot):
            cp.wait()
        wg_scr[...] = wg_buf[slot].astype(BF16)
        wu_scr[...] = wu_buf[slot].astype(BF16)
        wd_scr[...] = wd_buf[slot].astype(BF16)

        @pl.when(next_e_ref[i] >= 0)
        def _():
            for cp in weight_copies(next_e_ref[i], 1 - slot):
                cp.start()

        slot_ref[0] = 1 - slot

    @pl.when(used)
    def _():
        halves = [_unpack_row(w) for w in _load_rows(xs_ref, 0, EXPERT_BLOCK)]
        x = jnp.concatenate([lo for lo, _ in halves] + [hi for _, hi in halves], axis=-1).astype(BF16)
        a = (jax.nn.silu(_dot(x, wg_scr[...])) * _dot(x, wu_scr[...])).astype(BF16)
        _store_rows(ys_ref, _pack_row(_dot(a, wd_scr[...])))

    @pl.when(jnp.logical_not(used))
    def _():
        ys_ref[...] = jnp.zeros(ys_ref.shape, ys_ref.dtype)


def _experts(layer, blk_e, n_used, first, next_e, xs, wg, wu, wd):
    blk = (EXPERT_BLOCK * ROW_SUB, LANES)
    nb = xs.shape[0] // blk[0]
    d, de = wg.shape[-2], wg.shape[-1]
    last_used = lambda i, nu: jnp.minimum(i, nu[0] - 1)
    hbm = pl.BlockSpec(memory_space=pl.ANY)
    grid_spec = pltpu.PrefetchScalarGridSpec(
        num_scalar_prefetch=4, grid=(nb,),
        in_specs=[pl.BlockSpec(blk, lambda i, be, nu, fi, ne: (last_used(i, nu), 0)), hbm, hbm, hbm],
        out_specs=pl.BlockSpec(blk, lambda i, be, nu, fi, ne: (i, 0)),
        scratch_shapes=[pltpu.VMEM((2, d, de), F32), pltpu.VMEM((2, d, de), F32), pltpu.VMEM((2, de, d), F32),
                        pltpu.VMEM((d, de), BF16), pltpu.VMEM((d, de), BF16), pltpu.VMEM((de, d), BF16),
                        pltpu.SMEM((1,), I32), pltpu.SemaphoreType.DMA((2,))])
    return pl.pallas_call(
        functools.partial(_experts_body, layer), grid_spec=grid_spec,
        out_shape=jax.ShapeDtypeStruct(xs.shape, xs.dtype),
        compiler_params=_params("arbitrary"), name="experts",
    )(blk_e, n_used, first, next_e, xs, wg, wu, wd)


def _combine_body(dest_ref, dest_next_ref, g8_ref, hs_ref, ys_hbm, o_ref, buf, sems):
    n_tok = hs_ref.shape[0]
    n_grp = n_tok // ISSUE_GROUP
    step = pl.program_id(0)
    cur = step % 2
    half_rows = TOP_K * n_tok

    def for_group(idx_ref, slot, op):
        base = slot * (half_rows * ROW_SUB)

        def row_copy(src_row, k, gi, g):
            src = pl.ds(pl.multiple_of(src_row * ROW_SUB, ROW_SUB), ROW_SUB)
            dst = base + (k * n_grp + gi) * (ISSUE_GROUP * ROW_SUB) + g * ROW_SUB
            return pltpu.make_async_copy(ys_hbm.at[src, :], buf.at[pl.ds(pl.multiple_of(dst, ROW_SUB), ROW_SUB), :],
                                         sems.at[slot])

        def body(gi, carry):
            p0 = gi * (ISSUE_GROUP * TOP_K)
            for g in range(ISSUE_GROUP):
                for k in range(TOP_K):
                    op(row_copy(idx_ref[p0 + g * TOP_K + k], k, gi, g), k)
            return carry
        lax.fori_loop(0, n_grp, body, 0)

    start = lambda cp, k: cp.start(priority=k % 2)

    @pl.when(step == 0)
    def _():
        for_group(dest_ref, 0, start)

    @pl.when(step + 1 < pl.num_programs(0))
    def _():
        for_group(dest_next_ref, 1 - cur, start)

    for_group(dest_ref, cur, lambda cp, k: cp.wait())

    half = ROW_SUB * LANES

    def reduce(gi, carry):
        r0 = pl.multiple_of(gi * ISSUE_GROUP, ISSUE_GROUP)
        rows = pl.ds(r0, ISSUE_GROUP)
        gates = [g8_ref[rows, k:k + 1] for k in range(TOP_K)]
        picked = [_load_rows(buf, cur * half_rows + k * n_tok + r0, ISSUE_GROUP) for k in range(TOP_K)]
        for c in range(ROW_SUB):
            lo_cols = slice(c * LANES, (c + 1) * LANES)
            hi_cols = slice(half + c * LANES, half + (c + 1) * LANES)
            acc_lo = hs_ref[rows, lo_cols]
            acc_hi = hs_ref[rows, hi_cols]
            for k in range(TOP_K):
                lo, hi = _unpack_row(picked[k][c])
                acc_lo = acc_lo + gates[k] * lo
                acc_hi = acc_hi + gates[k] * hi
            o_ref[rows, lo_cols] = acc_lo
            o_ref[rows, hi_cols] = acc_hi
        return carry

    lax.fori_loop(0, n_grp, reduce, 0)


def _combine(dest_flat, g8, hs, ys):
    t, d = hs.shape
    tt = min(COMBINE_TILE, t)
    n_steps = t // tt
    return pl.pallas_call(
        _combine_body, grid=(n_steps,),
        in_specs=[pl.BlockSpec((tt * TOP_K,), lambda i: (i,), memory_space=pltpu.SMEM),
                  pl.BlockSpec((tt * TOP_K,), lambda i: (jnp.minimum(i + 1, n_steps - 1),), memory_space=pltpu.SMEM),
                  pl.BlockSpec((tt, g8.shape[1]), lambda i: (i, 0)),
                  pl.BlockSpec((tt, d), lambda i: (i, 0)),
                  pl.BlockSpec(memory_space=pl.ANY)],
        out_specs=pl.BlockSpec((tt, d), lambda i: (i, 0)),
        out_shape=jax.ShapeDtypeStruct((t, d), F32),
        scratch_shapes=[pltpu.VMEM((2 * TOP_K * tt * ROW_SUB, LANES), ys.dtype), pltpu.SemaphoreType.DMA((2,))],
        compiler_params=_params("arbitrary"), name="combine",
    )(dest_flat, dest_flat, g8, hs, ys)


def _ple_body(final, h2_ref, p_ref, win_ref, wgate_ref, g_ref, gf_ref, o_ref):
    h2 = h2_ref[...]
    e = _dot(p_ref[...].astype(BF16), win_ref[...])
    gate = jax.nn.sigmoid(_dot(h2.astype(BF16), wgate_ref[...]))
    h3 = h2 + _rms(gate * e, g_ref[...])
    o_ref[...] = _rms(h3, gf_ref[...]) if final else h3


def _ple(h2, p, w_in, w_gate, g, g_final, final):
    t, d = h2.shape
    tm = min(TOKEN_TILE, t)
    row = lambda n: pl.BlockSpec((tm, n), lambda i: (i, 0))
    return pl.pallas_call(
        functools.partial(_ple_body, final), grid=(t // tm,),
        in_specs=[row(d), row(p.shape[1]), _full(w_in.shape), _full(w_gate.shape), _full((1, d)), _full((1, d))],
        out_specs=row(d), out_shape=jax.ShapeDtypeStruct((t, d), F32),
        compiler_params=_params("parallel"), name="ple",
    )(h2, p, w_in, w_gate, g, g_final)


def _mixers(h, bsz, seq, norm1_g, w_in, four_w, ln_g, ln_b, ws, bs, conv_w, gate_b, norm_g, tables):
    d = h.shape[1]
    n_main = w_in.shape[1] - N_GATES
    w_main = w_in[:, :n_main].astype(BF16)
    w_g = w_in[:, n_main:]
    w_gate = jnp.pad(w_g, ((0, 0), (0, LANES - N_GATES))).astype(BF16)
    w_gate_t = w_g.T.astype(BF16)
    zf, zu, zv, zqk, zmv, zo, zg, zgt = _in_proj(h, norm1_g.reshape(1, d), w_main, w_gate, w_gate_t)
    b3 = lambda a: a.reshape(bsz, seq, a.shape[-1])

    cs, ss, cm, sm = tables
    yf = _fourier(b3(zf), cs, ss, cm, sm, _block_diag(four_w).astype(BF16))

    bs_full = jnp.repeat(bs.T, GMLP_HD, axis=1)
    yg = _gmlp(b3(zu), b3(zv), ln_g.reshape(1, -1), ln_b.reshape(1, -1), ws.astype(BF16), bs_full)

    zqk_pad = jnp.pad(b3(zqk), ((0, 0), (CONV_PAD, CONV_PAD), (0, 0)))
    gate_b_c = jnp.pad(gate_b, (0, LANES - N_GATES)).reshape(1, LANES)
    gate_b_t = jnp.broadcast_to(gate_b.reshape(N_GATES, 1), (N_GATES, LANES))
    ym = _mlstm(zqk_pad, b3(zmv), b3(zo), b3(zg), zgt, conv_w, gate_b_c, gate_b_t, norm_g.reshape(1, -1))
    t = bsz * seq
    return yf.reshape(t, -1), yg.reshape(t, -1), ym.reshape(t, -1)


def _moe(layer, h, yf, yg, ym, w_out, norm2_g, router_w, router_b, wg, wu, wd, sg, su, sd):
    t, d = h.shape
    sgu = jnp.concatenate([sg, su], axis=1).astype(BF16)
    hs, xw, sel, gates, idx = _post_mix(h, yf, yg, ym, w_out.astype(BF16), norm2_g.reshape(1, d),
                                        router_w, router_b.reshape(1, -1), sgu, sd.astype(BF16))
    rank, counts = _rank(sel)
    counts = counts[0]
    pcounts = ((counts + EXPERT_BLOCK - 1) // EXPERT_BLOCK) * EXPERT_BLOCK
    pend = jnp.cumsum(pcounts)
    offs = (pend - pcounts).astype(I32)
    tail_start = jnp.where(pcounts > 0, pend - EXPERT_BLOCK, -1).astype(I32)
    n_blk = (t * TOP_K) // EXPERT_BLOCK + N_EXPERTS
    blk_start = jnp.arange(n_blk, dtype=I32) * EXPERT_BLOCK
    blk_e = jnp.minimum(jnp.sum(pend[None, :] <= blk_start[:, None], axis=1), N_EXPERTS - 1).astype(I32)
    n_used = (pend[-1:] // EXPERT_BLOCK).astype(I32)
    blk_i = jnp.arange(n_blk, dtype=I32)
    first = ((blk_i < n_used[0]) & ((blk_i == 0) | (blk_e != jnp.roll(blk_e, 1)))).astype(I32)
    after = pend[blk_e] // EXPERT_BLOCK
    next_e = jnp.where(after < n_used[0], blk_e[jnp.minimum(after, n_blk - 1)], -1).astype(I32)

    dest, g8 = _dest(rank, idx, gates, offs.reshape(1, -1))
    dest_flat = dest[:, :TOP_K].reshape(-1)
    xs = _dispatch(dest_flat, tail_start, xw, n_blk * EXPERT_BLOCK)
    ys = _experts(layer, blk_e, n_used, first, next_e, xs, wg, wu, wd)
    return _combine(dest_flat, g8, hs, ys)


def kernel(x, p, norm1_g, w_in, four_w, gmlp_ln_g, gmlp_ln_b, gmlp_ws, gmlp_bs, mlstm_conv_w,
           mlstm_gate_b, mlstm_norm_g, w_out, norm2_g, router_w, router_b, exp_w_gate, exp_w_up,
           exp_w_down, sh_w_gate, sh_w_up, sh_w_down, ple_w_in, ple_w_gate, ple_norm_g, final_norm_g):
    bsz, seq, d = x.shape
    depth = w_in.shape[0]
    t = bsz * seq
    cos_s, sin_s = _dft_tables(seq)
    cos_m, sin_m = _dft_tables(FOUR_HD)
    eye = np.eye(FOUR_HEADS)
    tables = (jnp.asarray(cos_s, BF16), jnp.asarray(sin_s, BF16),
              jnp.asarray(np.kron(eye, cos_m), BF16), jnp.asarray(np.kron(eye, sin_m), BF16))
    h = x.reshape(t, d)
    for i in range(depth):
        yf, yg, ym = _mixers(h, bsz, seq, norm1_g[i], w_in[i], four_w[i], gmlp_ln_g[i], gmlp_ln_b[i],
                             gmlp_ws[i], gmlp_bs[i], mlstm_conv_w[i], mlstm_gate_b[i], mlstm_norm_g[i], tables)
        h2 = _moe(i, h, yf, yg, ym, w_out[i], norm2_g[i], router_w[i], router_b[i],
                  exp_w_gate, exp_w_up, exp_w_down, sh_w_gate[i], sh_w_up[i], sh_w_down[i])
        h = _ple(h2, p[i].reshape(t, -1), ple_w_in[i].astype(BF16), ple_w_gate[i].astype(BF16),
                 ple_norm_g[i].reshape(1, d), final_norm_g.reshape(1, d), i == depth - 1)
    return h.reshape(bsz, seq, d)
```

```python
import functools
import math

import numpy as np
import jax
import jax.numpy as jnp
from jax import lax
from jax.experimental import pallas as pl
from jax.experimental.pallas import tpu as pltpu

F32, BF16, I32, U32 = jnp.float32, jnp.bfloat16, jnp.int32, jnp.uint32
HIGHEST = lax.Precision.HIGHEST

EPS = 1e-6
LANES = 128
VMEM_LIMIT_BYTES = 48 * 1024 * 1024

FOUR_HEADS, FOUR_HD = 4, 64
GMLP_HEADS, GMLP_HD, GMLP_CHUNK = 4, 64, 128
MLSTM_HEADS, MLSTM_DV, MLSTM_DQK, MLSTM_CHUNK = 4, 128, 64, 128
CONV_PAD = 16
N_GATES = 16
N_EXPERTS, TOP_K = 128, 8
EXPERT_BLOCK = 512
TOKEN_TILE = 512
DISPATCH_TILE = 256
COMBINE_TILE = 256
ISSUE_GROUP = 8


def _params(*sem):
    return pltpu.CompilerParams(dimension_semantics=sem, vmem_limit_bytes=VMEM_LIMIT_BYTES)


def _rms(x, g):
    return x * lax.rsqrt(jnp.mean(x * x, axis=-1, keepdims=True) + EPS) * g


def _full(shape):
    nd = len(shape)
    return pl.BlockSpec(shape, lambda *_: (0,) * nd)


def _dot(a, b):
    return jnp.dot(a, b, preferred_element_type=F32)


def _pack_row(x):
    n = x.shape[-1] // 2
    lo = lax.bitcast_convert_type(x[:, :n].astype(BF16).astype(F32), U32)
    hi = lax.bitcast_convert_type(x[:, n:].astype(BF16).astype(F32), U32)
    return hi | (lo >> 16)


def _unpack_row(w):
    lo = lax.bitcast_convert_type(w << 16, F32)
    hi = lax.bitcast_convert_type(w & jnp.uint32(0xFFFF0000), F32)
    return lo, hi


ROW_SUB = 4


def _store_rows(ref, w):
    m = w.shape[0]
    for c in range(ROW_SUB):
        ref[pl.ds(c, m, stride=ROW_SUB), :] = w[:, c * LANES:(c + 1) * LANES]


def _load_rows(ref, r0, m):
    return [ref[pl.ds(r0 * ROW_SUB + c, m, stride=ROW_SUB), :] for c in range(ROW_SUB)]


def _in_proj_body(h_ref, g_ref, w_ref, wg_ref, wgt_ref,
                  zf_ref, zu_ref, zv_ref, zqk_ref, zmv_ref, zo_ref, zg_ref, zgt_ref):
    ab = _rms(h_ref[...], g_ref[...]).astype(BF16)
    off = 0
    for o_ref in (zf_ref, zu_ref, zv_ref, zqk_ref, zmv_ref, zo_ref):
        n = o_ref.shape[-1]
        o_ref[...] = _dot(ab, w_ref[:, off:off + n]).astype(o_ref.dtype)
        off += n
    zg_ref[...] = _dot(ab, wg_ref[...])
    zgt_ref[...] = lax.dot_general(wgt_ref[...], ab, (((1,), (1,)), ((), ())),
                                   preferred_element_type=F32)


def _in_proj(h, g, w_main, w_gate, w_gate_t):
    t, d = h.shape
    tm = min(TOKEN_TILE, t)
    widths = (256, 256, 256, 512, 512, 512)
    row = lambda n: pl.BlockSpec((tm, n), lambda i: (i, 0))
    out_shape = [jax.ShapeDtypeStruct((t, n), BF16) for n in widths]
    out_shape += [jax.ShapeDtypeStruct((t, LANES), F32), jax.ShapeDtypeStruct((N_GATES, t), F32)]
    out_specs = [row(n) for n in widths]
    out_specs += [row(LANES), pl.BlockSpec((N_GATES, tm), lambda i: (0, i))]
    return pl.pallas_call(
        _in_proj_body, grid=(t // tm,),
        in_specs=[row(d), _full((1, d)), _full(w_main.shape), _full(w_gate.shape), _full(w_gate_t.shape)],
        out_specs=out_specs, out_shape=out_shape,
        compiler_params=_params("parallel"), name="in_proj",
    )(h, g, w_main, w_gate, w_gate_t)


def _dft_tables(n):
    k = np.arange(n, dtype=np.int64)
    ang = 2.0 * np.pi * ((k[:, None] * k[None, :]) % n).astype(np.float64) / n
    return np.cos(ang), np.sin(ang)


def _block_diag(blocks):
    h, a, b = blocks.shape
    eye = jnp.eye(h, dtype=blocks.dtype)
    return (eye[:, None, :, None] * blocks[:, :, None, :]).reshape(h * a, h * b)


def _fourier_body(scale, row_tile, z_ref, cs_ref, ss_ref, cm_ref, sm_ref, w_ref, o_ref, p_scr, q_scr):
    z = z_ref[0]
    p_scr[...] = _dot(z, cm_ref[...]).astype(BF16)
    q_scr[...] = _dot(z, sm_ref[...]).astype(BF16)
    s = z.shape[0]
    for r in range(s // row_tile):
        rs = slice(r * row_tile, (r + 1) * row_tile)
        re = (_dot(cs_ref[rs, :], p_scr[...]) - _dot(ss_ref[rs, :], q_scr[...])) * scale
        o_ref[0, rs, :] = _dot(re.astype(BF16), w_ref[...]).astype(o_ref.dtype)


def _fourier(zf, cs, ss, cm, sm, wbd):
    b, s, w = zf.shape
    row_tile = min(512, s)
    scale = 1.0 / math.sqrt(s * FOUR_HD)
    blk = pl.BlockSpec((1, s, w), lambda i: (i, 0, 0))
    return pl.pallas_call(
        functools.partial(_fourier_body, scale, row_tile), grid=(b,),
        in_specs=[blk, _full(cs.shape), _full(ss.shape), _full(cm.shape), _full(sm.shape), _full(wbd.shape)],
        out_specs=blk, out_shape=jax.ShapeDtypeStruct((b, s, w), BF16),
        scratch_shapes=[pltpu.VMEM((s, w), BF16), pltpu.VMEM((s, w), BF16)],
        compiler_params=_params("parallel"), name="fourier",
    )(zf, cs, ss, cm, sm, wbd)


def _gmlp_body(zu_ref, zv_ref, lg_ref, lb_ref, ws_ref, bs_ref, o_ref):
    s = zu_ref.shape[1]
    w = zu_ref.shape[2]
    lane = lax.broadcasted_iota(I32, (GMLP_CHUNK, w), 1)

    def chunk(c, carry):
        r0 = pl.multiple_of(c * GMLP_CHUNK, GMLP_CHUNK)
        rows = pl.ds(r0, GMLP_CHUNK)
        v = jax.nn.gelu(zv_ref[0, rows, :].astype(F32))
        vc = v - jnp.mean(v, axis=-1, keepdims=True)
        vn = vc * lax.rsqrt(jnp.mean(vc * vc, axis=-1, keepdims=True) + EPS) * lg_ref[...] + lb_ref[...]
        acc = bs_ref[...]
        for h in range(GMLP_HEADS):
            vh = jnp.where((lane >= h * GMLP_HD) & (lane < (h + 1) * GMLP_HD), vn, 0.0).astype(BF16)
            acc = acc + _dot(ws_ref[h], vh)
        u = jax.nn.gelu(zu_ref[0, rows, :].astype(F32))
        o_ref[0, rows, :] = (u * acc).astype(o_ref.dtype)
        return carry

    lax.fori_loop(0, s // GMLP_CHUNK, chunk, 0)


def _gmlp(zu, zv, ln_g, ln_b, ws, bs_full):
    b, s, w = zu.shape
    blk = pl.BlockSpec((1, s, w), lambda i: (i, 0, 0))
    return pl.pallas_call(
        _gmlp_body, grid=(b,),
        in_specs=[blk, blk, _full((1, w)), _full((1, w)), _full(ws.shape), _full(bs_full.shape)],
        out_specs=blk, out_shape=jax.ShapeDtypeStruct((b, s, w), BF16),
        compiler_params=_params("parallel"), name="gmlp",
    )(zu, zv, ln_g, ln_b, ws, bs_full)


def _mlstm_body(qkp_ref, v_ref, zo_ref, g_ref, gt_ref, cw_ref, gb_ref, gbt_ref, ng_ref, o_ref,
                qm_scr, km_scr, vt_scr, hf_scr, hb_scr, c_scr, n_scr, m_scr, gc_scr, gr_scr, cumc_scr, cumr_scr):
    L = MLSTM_CHUNK
    H = MLSTM_HEADS
    DV = MLSTM_DV
    s = v_ref.shape[1]
    nc = s // L
    qkw = H * MLSTM_DQK
    nt = (((1,), (1,)), ((), ()))

    ext = L + 2 * CONV_PAD
    r_i = lax.broadcasted_iota(I32, (L, ext), 0)
    c_i = lax.broadcasted_iota(I32, (L, ext), 1)
    sh_m1 = jnp.where(c_i == r_i + CONV_PAD - 1, 1.0, 0.0).astype(BF16)
    sh_0 = jnp.where(c_i == r_i + CONV_PAD, 1.0, 0.0).astype(BF16)
    sh_p1 = jnp.where(c_i == r_i + CONV_PAD + 1, 1.0, 0.0).astype(BF16)
    lane = lax.broadcasted_iota(I32, (L, LANES), 1)
    low_half = lane < MLSTM_DQK

    def conv_chunk(c, carry):
        r0 = pl.multiple_of(c * L, L)
        rows = pl.ds(r0, L)
        xe = qkp_ref[0, pl.ds(r0, ext), :]
        y = (cw_ref[0:1, :] * _dot(sh_m1, xe) + cw_ref[1:2, :] * _dot(sh_0, xe)
             + cw_ref[2:3, :] * _dot(sh_p1, xe))
        qk = y * jax.nn.sigmoid(y)
        for h in range(H):
            keep = low_half if h % 2 == 0 else jnp.logical_not(low_half)
            t0 = (h // 2) * LANES
            q_t = qk[:, t0:t0 + LANES] * (MLSTM_DQK ** -0.5)
            k_t = qk[:, qkw + t0:qkw + t0 + LANES]
            qm_scr[h, rows, :] = jnp.where(keep, q_t, 0.0).astype(BF16)
            km_scr[h, rows, :] = jnp.where(keep, k_t, 0.0).astype(BF16)
            vt_scr[h * DV:(h + 1) * DV, rows] = v_ref[0, rows, h * DV:(h + 1) * DV].astype(F32).T.astype(BF16)
        gc = g_ref[0, rows, :] + gb_ref[...]
        gr = gt_ref[:, rows] + gbt_ref[...]
        gc_scr[rows, :] = gc
        gr_scr[:, rows] = gr
        lf_c = jax.nn.log_sigmoid(gc)
        lf_r = jax.nn.log_sigmoid(gr)
        cumc_scr[0, rows, :] = jnp.dot(tri_l, lf_c, precision=HIGHEST, preferred_element_type=F32)
        cumc_scr[1, rows, :] = jnp.dot(tri_u, lf_c, precision=HIGHEST, preferred_element_type=F32)
        cumr_scr[0, :, rows] = jnp.dot(lf_r, tri_u, precision=HIGHEST, preferred_element_type=F32)
        cumr_scr[1, :, rows] = jnp.dot(lf_r, tri_l, precision=HIGHEST, preferred_element_type=F32)
        return carry

    ri = lax.broadcasted_iota(I32, (L, L), 0)
    ci = lax.broadcasted_iota(I32, (L, L), 1)
    tri_l = jnp.where(ci <= ri, 1.0, 0.0)
    tri_u = jnp.where(ci >= ri, 1.0, 0.0)
    lax.fori_loop(0, nc, conv_chunk, 0)

    c_scr[...] = jnp.zeros(c_scr.shape, F32)
    n_scr[...] = jnp.zeros(n_scr.shape, F32)
    m_scr[...] = jnp.zeros(m_scr.shape, F32)

    def rep8(row):
        return jnp.broadcast_to(row, (8, row.shape[-1])).astype(BF16)

    def direction(d, chunk):
        r0 = pl.multiple_of(chunk * L, L)
        rows = pl.ds(r0, L)
        gc = gc_scr[rows, :]
        gr = gr_scr[:, rows]
        cum_c = cumc_scr[d, rows, :]
        cum_r = cumr_scr[d, :, rows]
        mask = (ci >= ri) if d == 0 else (ci <= ri)
        i_lane = 2 * d * H
        f_lane = (2 * d + 1) * H
        for h in range(H):
            u = d * H + h
            b_r = cum_r[f_lane + h:f_lane + h + 1, :]
            li_r = gr[i_lane + h:i_lane + h + 1, :]
            key_c = gc[:, i_lane + h:i_lane + h + 1] - cum_c[:, f_lane + h:f_lane + h + 1]
            g_edge = b_r[:, L - 1:L] if d == 0 else b_r[:, 0:1]
            g_tot = jnp.broadcast_to(g_edge, (1, L))
            qh = qm_scr[h, rows, :]
            kh = km_scr[h, rows, :]
            vt = vt_scr[h * DV:(h + 1) * DV, rows]
            ct_prev = c_scr[u]
            n_prev = n_scr[u]
            m_prev = m_scr[u]
            dmat = jnp.where(mask, b_r + key_c, -jnp.inf)
            m_intra = jnp.max(dmat, axis=0, keepdims=True)
            m_inter = b_r + m_prev
            m_tot = jnp.maximum(m_intra, m_inter)
            kq = lax.dot_general(kh, qh, nt, preferred_element_type=F32)
            s_mat = jnp.exp(dmat - m_tot) * kq
            inter = jnp.exp(m_inter - m_tot)
            qn = lax.dot_general(rep8(n_prev), qh, nt, preferred_element_type=F32)[0:1, :]
            den = jnp.sum(s_mat, axis=0, keepdims=True) + inter * qn
            num = (_dot(vt, s_mat.astype(BF16))
                   + inter * lax.dot_general(ct_prev.astype(BF16), qh, nt, preferred_element_type=F32))
            dst = hf_scr if d == 0 else hb_scr
            dst[h * DV:(h + 1) * DV, rows] = num / jnp.maximum(jnp.abs(den), jnp.exp(-m_tot))
            a_r = g_tot - b_r + li_r
            m_loc = jnp.broadcast_to(jnp.max(a_r, axis=-1, keepdims=True), (1, L))
            w_r = jnp.exp(a_r - m_loc)
            ct_loc = _dot((vt.astype(F32) * w_r).astype(BF16), kh)
            n_loc = _dot(rep8(w_r), kh)[0:1, :]
            m_new = jnp.maximum(g_tot + m_prev, m_loc)
            a_old = jnp.exp(g_tot + m_prev - m_new)
            a_new = jnp.exp(m_loc - m_new)
            c_scr[u] = a_old * ct_prev + a_new * ct_loc
            n_scr[u] = a_old * n_prev + a_new * n_loc
            m_scr[u] = m_new

    def step(i, carry):
        direction(0, i)
        direction(1, nc - 1 - i)
        return carry

    lax.fori_loop(0, nc, step, 0)

    def finish(c, carry):
        r0 = pl.multiple_of(c * L, L)
        rows = pl.ds(r0, L)
        og = jax.nn.sigmoid(zo_ref[0, rows, :].astype(F32))
        for h in range(H):
            cols = slice(h * DV, (h + 1) * DV)
            x = og[:, cols] * (hf_scr[cols, rows] + hb_scr[cols, rows]).T
            xc = x - jnp.mean(x, axis=-1, keepdims=True)
            y = xc * lax.rsqrt(jnp.mean(xc * xc, axis=-1, keepdims=True) + EPS) * ng_ref[:, cols]
            o_ref[0, rows, cols] = y.astype(o_ref.dtype)
        return carry

    lax.fori_loop(0, nc, finish, 0)


def _mlstm(zqk_pad, zmv, zo, zg, zgt, conv_w, gate_b, gate_b_t, norm_g):
    b, s, w = zmv.shape
    H = MLSTM_HEADS
    blk = pl.BlockSpec((1, s, w), lambda i: (i, 0, 0))
    units = 2 * H
    return pl.pallas_call(
        _mlstm_body, grid=(b,),
        in_specs=[pl.BlockSpec((1, s + 2 * CONV_PAD, w), lambda i: (i, 0, 0)), blk, blk,
                  pl.BlockSpec((1, s, LANES), lambda i: (i, 0, 0)),
                  pl.BlockSpec((N_GATES, s), lambda i: (0, i)),
                  _full(conv_w.shape), _full(gate_b.shape), _full(gate_b_t.shape), _full(norm_g.shape)],
        out_specs=blk, out_shape=jax.ShapeDtypeStruct((b, s, w), BF16),
        scratch_shapes=[pltpu.VMEM((H, s, LANES), BF16), pltpu.VMEM((H, s, LANES), BF16),
                        pltpu.VMEM((w, s), BF16), pltpu.VMEM((w, s), F32), pltpu.VMEM((w, s), F32),
                        pltpu.VMEM((units, MLSTM_DV, LANES), F32), pltpu.VMEM((units, 1, LANES), F32),
                        pltpu.VMEM((units, 1, LANES), F32),
                        pltpu.VMEM((s, LANES), F32), pltpu.VMEM((N_GATES, s), F32),
                        pltpu.VMEM((2, s, LANES), F32), pltpu.VMEM((2, N_GATES, s), F32)],
        compiler_params=_params("parallel"), name="mlstm",
    )(zqk_pad, zmv, zo, zg, zgt, conv_w, gate_b, gate_b_t, norm_g)


def _post_mix_body(h_ref, yf_ref, yg_ref, ym_ref, wo_ref, g2_ref, rw_ref, rb_ref, sgu_ref, sd_ref,
                   hs_ref, xw_ref, sel_ref, gate_ref, idx_ref):
    wf, wg = yf_ref.shape[-1], yg_ref.shape[-1]
    h1 = (h_ref[...] + _dot(yf_ref[...], wo_ref[0:wf, :]) + _dot(yg_ref[...], wo_ref[wf:wf + wg, :])
          + _dot(ym_ref[...], wo_ref[wf + wg:, :]))
    xn = _rms(h1, g2_ref[...])
    xb = xn.astype(BF16)
    _store_rows(xw_ref, _pack_row(xn))
    gu = _dot(xb, sgu_ref[...])
    de = gu.shape[-1] // 2
    act = (jax.nn.silu(gu[:, :de]) * gu[:, de:]).astype(BF16)
    hs_ref[...] = h1 + _dot(act, sd_ref[...])
    scores = jax.nn.sigmoid(jnp.dot(xn, rw_ref[...], precision=HIGHEST, preferred_element_type=F32))
    work = scores + rb_ref[...]
    lane = lax.broadcasted_iota(I32, scores.shape, 1)
    sel = jnp.zeros(scores.shape, F32)
    idx = jnp.zeros(scores.shape, I32)
    for k in range(TOP_K):
        m = jnp.max(work, axis=-1, keepdims=True)
        e = jnp.min(jnp.where(work == m, lane, N_EXPERTS), axis=-1, keepdims=True)
        hit = lane == e
        sel = jnp.where(hit, 1.0, sel)
        idx = jnp.where(lane == k, e, idx)
        work = jnp.where(hit, -jnp.inf, work)
    picked = sel * scores
    sel_ref[...] = sel
    gate_ref[...] = picked / jnp.sum(picked, axis=-1, keepdims=True)
    idx_ref[...] = idx


def _post_mix(h, yf, yg, ym, w_out, g2, router_w, router_b, sgu, sd):
    t, d = h.shape
    tm = min(TOKEN_TILE, t)
    row = lambda n: pl.BlockSpec((tm, n), lambda i: (i, 0))
    return pl.pallas_call(
        _post_mix_body, grid=(t // tm,),
        in_specs=[row(d), row(yf.shape[1]), row(yg.shape[1]), row(ym.shape[1]), _full(w_out.shape),
                  _full((1, d)), _full(router_w.shape), _full((1, N_EXPERTS)), _full(sgu.shape), _full(sd.shape)],
        out_specs=[row(d), pl.BlockSpec((tm * ROW_SUB, LANES), lambda i: (i, 0)),
                   row(N_EXPERTS), row(N_EXPERTS), row(N_EXPERTS)],
        out_shape=[jax.ShapeDtypeStruct((t, d), F32), jax.ShapeDtypeStruct((t * ROW_SUB, LANES), U32),
                   jax.ShapeDtypeStruct((t, N_EXPERTS), F32), jax.ShapeDtypeStruct((t, N_EXPERTS), F32),
                   jax.ShapeDtypeStruct((t, N_EXPERTS), I32)],
        compiler_params=_params("parallel"), name="post_mix",
    )(h, yf, yg, ym, w_out, g2, router_w, router_b, sgu, sd)


def _rank_body(sel_ref, rank_ref, cnt_ref, carry):
    @pl.when(pl.program_id(0) == 0)
    def _():
        carry[...] = jnp.zeros(carry.shape, F32)

    m = sel_ref[...]
    tm = m.shape[0]
    ri = lax.broadcasted_iota(I32, (tm, tm), 0)
    ci = lax.broadcasted_iota(I32, (tm, tm), 1)
    below = jnp.where(ci < ri, 1.0, 0.0).astype(BF16)
    rank = _dot(below, m.astype(BF16)) + carry[...]
    rank_ref[...] = rank.astype(I32)
    carry[...] = carry[...] + jnp.sum(m, axis=0, keepdims=True)
    cnt_ref[...] = carry[...].astype(I32)


def _rank(sel):
    t, e = sel.shape
    tm = min(TOKEN_TILE, t)
    return pl.pallas_call(
        _rank_body, grid=(t // tm,),
        in_specs=[pl.BlockSpec((tm, e), lambda i: (i, 0))],
        out_specs=[pl.BlockSpec((tm, e), lambda i: (i, 0)), _full((1, e))],
        out_shape=[jax.ShapeDtypeStruct((t, e), I32), jax.ShapeDtypeStruct((1, e), I32)],
        scratch_shapes=[pltpu.VMEM((1, e), F32)],
        compiler_params=_params("arbitrary"), name="rank",
    )(sel)


def _dest_body(rank_ref, idx_ref, gate_ref, offs_ref, dest_ref, g8_ref):
    pos = (rank_ref[...] + offs_ref[...]).astype(F32)
    gates = gate_ref[...]
    lane = lax.broadcasted_iota(I32, pos.shape, 1)
    dest = jnp.zeros(pos.shape, F32)
    g8 = jnp.zeros(pos.shape, F32)
    for k in range(TOP_K):
        hit = lane == idx_ref[:, k:k + 1]
        dk = jnp.sum(jnp.where(hit, pos, 0.0), axis=-1, keepdims=True)
        gk = jnp.sum(jnp.where(hit, gates, 0.0), axis=-1, keepdims=True)
        dest = jnp.where(lane == k, dk, dest)
        g8 = jnp.where(lane == k, gk, g8)
    dest_ref[...] = dest.astype(I32)
    g8_ref[...] = g8


def _dest(rank, idx, gates, offs):
    t, e = rank.shape
    tm = min(TOKEN_TILE, t)
    row = pl.BlockSpec((tm, e), lambda i: (i, 0))
    return pl.pallas_call(
        _dest_body, grid=(t // tm,),
        in_specs=[row, row, row, _full((1, e))],
        out_specs=[row, row],
        out_shape=[jax.ShapeDtypeStruct((t, e), I32), jax.ShapeDtypeStruct((t, e), F32)],
        compiler_params=_params("parallel"), name="dest",
    )(rank, idx, gates, offs)


def _dispatch_body(dest_ref, tail_ref, x_ref, xs_hbm, zero_scr, sem):
    n_grp = x_ref.shape[0]

    @pl.when(pl.program_id(0) == 0)
    def _():
        zero_scr[...] = jnp.zeros(zero_scr.shape, zero_scr.dtype)

        def tail_copy(e):
            start = pl.multiple_of(jnp.maximum(tail_ref[e], 0) * ROW_SUB, EXPERT_BLOCK * ROW_SUB)
            return pltpu.make_async_copy(zero_scr, xs_hbm.at[pl.ds(start, EXPERT_BLOCK * ROW_SUB), :], sem)

        def clear(e, carry):
            @pl.when(tail_ref[e] >= 0)
            def _():
                tail_copy(e).start()
            return carry

        def clear_done(e, carry):
            @pl.when(tail_ref[e] >= 0)
            def _():
                tail_copy(e).wait()
            return carry

        lax.fori_loop(0, N_EXPERTS, clear, 0)
        lax.fori_loop(0, N_EXPERTS, clear_done, 0)

    def row_copy(group, g, dst_row):
        dst = pl.ds(pl.multiple_of(dst_row * ROW_SUB, ROW_SUB), ROW_SUB)
        return pltpu.make_async_copy(x_ref.at[group, g], xs_hbm.at[dst, :], sem)

    def for_group(op):
        def body(gi, carry):
            p0 = gi * (ISSUE_GROUP * TOP_K)
            for g in range(ISSUE_GROUP):
                for k in range(TOP_K):
                    op(row_copy(gi, g, dest_ref[p0 + g * TOP_K + k]), k)
            return carry
        lax.fori_loop(0, n_grp, body, 0)

    for_group(lambda cp, k: cp.start(priority=k % 2))
    for_group(lambda cp, k: cp.wait())


def _dispatch(dest_flat, tail_start, xw, n_rows):
    t = xw.shape[0] // ROW_SUB
    tt = min(DISPATCH_TILE, t)
    x4 = xw.reshape(t // ISSUE_GROUP, ISSUE_GROUP, ROW_SUB, LANES)
    return pl.pallas_call(
        _dispatch_body, grid=(t // tt,),
        in_specs=[pl.BlockSpec((tt * TOP_K,), lambda i: (i,), memory_space=pltpu.SMEM),
                  pl.BlockSpec(memory_space=pltpu.SMEM),
                  pl.BlockSpec((tt // ISSUE_GROUP, ISSUE_GROUP, ROW_SUB, LANES), lambda i: (i, 0, 0, 0))],
        out_specs=pl.BlockSpec(memory_space=pl.ANY),
        out_shape=jax.ShapeDtypeStruct((n_rows * ROW_SUB, LANES), xw.dtype),
        scratch_shapes=[pltpu.VMEM((EXPERT_BLOCK * ROW_SUB, LANES), xw.dtype), pltpu.SemaphoreType.DMA(())],
        compiler_params=_params("arbitrary"), name="dispatch",
    )(dest_flat, tail_start, x4)


def _experts_body(layer, blk_e_ref, n_used_ref, first_ref, next_e_ref, xs_ref, wg_hbm, wu_hbm, wd_hbm, ys_ref,
                  wg_buf, wu_buf, wd_buf, wg_scr, wu_scr, wd_scr, slot_ref, sems):
    i = pl.program_id(0)
    used = i < n_used_ref[0]

    def weight_copies(e, slot):
        return [pltpu.make_async_copy(hbm.at[layer, e], buf.at[slot], sems.at[slot])
                for hbm, buf in ((wg_hbm, wg_buf), (wu_hbm, wu_buf), (wd_hbm, wd_buf))]

    @pl.when(i == 0)
    def _():
        slot_ref[0] = 0
        for cp in weight_copies(blk_e_ref[0], 0):
            cp.start()

    @pl.when(first_ref[i] == 1)
    def _():
        slot = slot_ref[0]
        for cp in weight_copies(blk_e_ref[i], slot):
            cp.wait()
        wg_scr[...] = wg_buf[slot].astype(BF16)
        wu_scr[...] = wu_buf[slot].astype(BF16)
        wd_scr[...] = wd_buf[slot].astype(BF16)

        @pl.when(next_e_ref[i] >= 0)
        def _():
            for cp in weight_copies(next_e_ref[i], 1 - slot):
                cp.start()

        slot_ref[0] = 1 - slot

    @pl.when(used)
    def _():
        halves = [_unpack_row(w) for w in _load_rows(xs_ref, 0, EXPERT_BLOCK)]
        x = jnp.concatenate([lo for lo, _ in halves] + [hi for _, hi in halves], axis=-1).astype(BF16)
        a = (jax.nn.silu(_dot(x, wg_scr[...])) * _dot(x, wu_scr[...])).astype(BF16)
        _store_rows(ys_ref, _pack_row(_dot(a, wd_scr[...])))

    @pl.when(jnp.logical_not(used))
    def _():
        ys_ref[...] = jnp.zeros(ys_ref.shape, ys_ref.dtype)


def _experts(layer, blk_e, n_used, first, next_e, xs, wg, wu, wd):
    blk = (EXPERT_BLOCK * ROW_SUB, LANES)
    nb = xs.shape[0] // blk[0]
    d, de = wg.shape[-2], wg.shape[-1]
    last_used = lambda i, nu: jnp.minimum(i, nu[0] - 1)
    hbm = pl.BlockSpec(memory_space=pl.ANY)
    grid_spec = pltpu.PrefetchScalarGridSpec(
        num_scalar_prefetch=4, grid=(nb,),
        in_specs=[pl.BlockSpec(blk, lambda i, be, nu, fi, ne: (last_used(i, nu), 0)), hbm, hbm, hbm],
        out_specs=pl.BlockSpec(blk, lambda i, be, nu, fi, ne: (i, 0)),
        scratch_shapes=[pltpu.VMEM((2, d, de), F32), pltpu.VMEM((2, d, de), F32), pltpu.VMEM((2, de, d), F32),
                        pltpu.VMEM((d, de), BF16), pltpu.VMEM((d, de), BF16), pltpu.VMEM((de, d), BF16),
                        pltpu.SMEM((1,), I32), pltpu.SemaphoreType.DMA((2,))])
    return pl.pallas_call(
        functools.partial(_experts_body, layer), grid_spec=grid_spec,
        out_shape=jax.ShapeDtypeStruct(xs.shape, xs.dtype),
        compiler_params=_params("arbitrary"), name="experts",
    )(blk_e, n_used, first, next_e, xs, wg, wu, wd)


def _combine_body(dest_ref, dest_next_ref, g8_ref, hs_ref, ys_hbm, o_ref, buf, sems):
    n_tok = hs_ref.shape[0]
    n_grp = n_tok // ISSUE_GROUP
    step = pl.program_id(0)
    cur = step % 2
    half_rows = TOP_K * n_tok

    def for_group(idx_ref, slot, op):
        base = slot * (half_rows * ROW_SUB)

        def row_copy(src_row, k, gi, g):
            src = pl.ds(pl.multiple_of(src_row * ROW_SUB, ROW_SUB), ROW_SUB)
            dst = base + (k * n_grp + gi) * (ISSUE_GROUP * ROW_SUB) + g * ROW_SUB
            return pltpu.make_async_copy(ys_hbm.at[src, :], buf.at[pl.ds(pl.multiple_of(dst, ROW_SUB), ROW_SUB), :],
                                         sems.at[slot])

        def body(gi, carry):
            p0 = gi * (ISSUE_GROUP * TOP_K)
            for g in range(ISSUE_GROUP):
                for k in range(TOP_K):
                    op(row_copy(idx_ref[p0 + g * TOP_K + k], k, gi, g), k)
            return carry
        lax.fori_loop(0, n_grp, body, 0)

    start = lambda cp, k: cp.start(priority=k % 2)

    @pl.when(step == 0)
    def _():
        for_group(dest_ref, 0, start)

    @pl.when(step + 1 < pl.num_programs(0))
    def _():
        for_group(dest_next_ref, 1 - cur, start)

    for_group(dest_ref, cur, lambda cp, k: cp.wait())

    half = ROW_SUB * LANES

    def reduce(gi, carry):
        r0 = pl.multiple_of(gi * ISSUE_GROUP, ISSUE_GROUP)
        rows = pl.ds(r0, ISSUE_GROUP)
        gates = [g8_ref[rows, k:k + 1] for k in range(TOP_K)]
        picked = [_load_rows(buf, cur * half_rows + k * n_tok + r0, ISSUE_GROUP) for k in range(TOP_K)]
        for c in range(ROW_SUB):
            lo_cols = slice(c * LANES, (c + 1) * LANES)
            hi_cols = slice(half + c * LANES, half + (c + 1) * LANES)
            acc_lo = hs_ref[rows, lo_cols]
            acc_hi = hs_ref[rows, hi_cols]
            for k in range(TOP_K):
                lo, hi = _unpack_row(picked[k][c])
                acc_lo = acc_lo + gates[k] * lo
                acc_hi = acc_hi + gates[k] * hi
            o_ref[rows, lo_cols] = acc_lo
            o_ref[rows, hi_cols] = acc_hi
        return carry

    lax.fori_loop(0, n_grp, reduce, 0)


def _combine(dest_flat, g8, hs, ys):
    t, d = hs.shape
    tt = min(COMBINE_TILE, t)
    n_steps = t // tt
    return pl.pallas_call(
        _combine_body, grid=(n_steps,),
        in_specs=[pl.BlockSpec((tt * TOP_K,), lambda i: (i,), memory_space=pltpu.SMEM),
                  pl.BlockSpec((tt * TOP_K,), lambda i: (jnp.minimum(i + 1, n_steps - 1),), memory_space=pltpu.SMEM),
                  pl.BlockSpec((tt, g8.shape[1]), lambda i: (i, 0)),
                  pl.BlockSpec((tt, d), lambda i: (i, 0)),
                  pl.BlockSpec(memory_space=pl.ANY)],
        out_specs=pl.BlockSpec((tt, d), lambda i: (i, 0)),
        out_shape=jax.ShapeDtypeStruct((t, d), F32),
        scratch_shapes=[pltpu.VMEM((2 * TOP_K * tt * ROW_SUB, LANES), ys.dtype), pltpu.SemaphoreType.DMA((2,))],
        compiler_params=_params("arbitrary"), name="combine",
    )(dest_flat, dest_flat, g8, hs, ys)


def _ple_body(final, h2_ref, p_ref, win_ref, wgate_ref, g_ref, gf_ref, o_ref):
    h2 = h2_ref[...]
    e = _dot(p_ref[...].astype(BF16), win_ref[...])
    gate = jax.nn.sigmoid(_dot(h2.astype(BF16), wgate_ref[...]))
    h3 = h2 + _rms(gate * e, g_ref[...])
    o_ref[...] = _rms(h3, gf_ref[...]) if final else h3


def _ple(h2, p, w_in, w_gate, g, g_final, final):
    t, d = h2.shape
    tm = min(TOKEN_TILE, t)
    row = lambda n: pl.BlockSpec((tm, n), lambda i: (i, 0))
    return pl.pallas_call(
        functools.partial(_ple_body, final), grid=(t // tm,),
        in_specs=[row(d), row(p.shape[1]), _full(w_in.shape), _full(w_gate.shape), _full((1, d)), _full((1, d))],
        out_specs=row(d), out_shape=jax.ShapeDtypeStruct((t, d), F32),
        compiler_params=_params("parallel"), name="ple",
    )(h2, p, w_in, w_gate, g, g_final)


def _mixers(h, bsz, seq, norm1_g, w_in, four_w, ln_g, ln_b, ws, bs, conv_w, gate_b, norm_g, tables):
    d = h.shape[1]
    n_main = w_in.shape[1] - N_GATES
    w_main = w_in[:, :n_main].astype(BF16)
    w_g = w_in[:, n_main:]
    w_gate = jnp.pad(w_g, ((0, 0), (0, LANES - N_GATES))).astype(BF16)
    w_gate_t = w_g.T.astype(BF16)
    zf, zu, zv, zqk, zmv, zo, zg, zgt = _in_proj(h, norm1_g.reshape(1, d), w_main, w_gate, w_gate_t)
    b3 = lambda a: a.reshape(bsz, seq, a.shape[-1])

    cs, ss, cm, sm = tables
    yf = _fourier(b3(zf), cs, ss, cm, sm, _block_diag(four_w).astype(BF16))

    bs_full = jnp.repeat(bs.T, GMLP_HD, axis=1)
    yg = _gmlp(b3(zu), b3(zv), ln_g.reshape(1, -1), ln_b.reshape(1, -1), ws.astype(BF16), bs_full)

    zqk_pad = jnp.pad(b3(zqk), ((0, 0), (CONV_PAD, CONV_PAD), (0, 0)))
    gate_b_c = jnp.pad(gate_b, (0, LANES - N_GATES)).reshape(1, LANES)
    gate_b_t = jnp.broadcast_to(gate_b.reshape(N_GATES, 1), (N_GATES, LANES))
    ym = _mlstm(zqk_pad, b3(zmv), b3(zo), b3(zg), zgt, conv_w, gate_b_c, gate_b_t, norm_g.reshape(1, -1))
    t = bsz * seq
    return yf.reshape(t, -1), yg.reshape(t, -1), ym.reshape(t, -1)


def _moe(layer, h, yf, yg, ym, w_out, norm2_g, router_w, router_b, wg, wu, wd, sg, su, sd):
    t, d = h.shape
    sgu = jnp.concatenate([sg, su], axis=1).astype(BF16)
    hs, xw, sel, gates, idx = _post_mix(h, yf, yg, ym, w_out.astype(BF16), norm2_g.reshape(1, d),
                                        router_w, router_b.reshape(1, -1), sgu, sd.astype(BF16))
    rank, counts = _rank(sel)
    counts = counts[0]
    pcounts = ((counts + EXPERT_BLOCK - 1) // EXPERT_BLOCK) * EXPERT_BLOCK
    pend = jnp.cumsum(pcounts)
    offs = (pend - pcounts).astype(I32)
    tail_start = jnp.where(pcounts > 0, pend - EXPERT_BLOCK, -1).astype(I32)
    n_blk = (t * TOP_K) // EXPERT_BLOCK + N_EXPERTS
    blk_start = jnp.arange(n_blk, dtype=I32) * EXPERT_BLOCK
    blk_e = jnp.minimum(jnp.sum(pend[None, :] <= blk_start[:, None], axis=1), N_EXPERTS - 1).astype(I32)
    n_used = (pend[-1:] // EXPERT_BLOCK).astype(I32)
    blk_i = jnp.arange(n_blk, dtype=I32)
    first = ((blk_i < n_used[0]) & ((blk_i == 0) | (blk_e != jnp.roll(blk_e, 1)))).astype(I32)
    after = pend[blk_e] // EXPERT_BLOCK
    next_e = jnp.where(after < n_used[0], blk_e[jnp.minimum(after, n_blk - 1)], -1).astype(I32)

    dest, g8 = _dest(rank, idx, gates, offs.reshape(1, -1))
    dest_flat = dest[:, :TOP_K].reshape(-1)
    xs = _dispatch(dest_flat, tail_start, xw, n_blk * EXPERT_BLOCK)
    ys = _experts(layer, blk_e, n_used, first, next_e, xs, wg, wu, wd)
    return _combine(dest_flat, g8, hs, ys)


def kernel(x, p, norm1_g, w_in, four_w, gmlp_ln_g, gmlp_ln_b, gmlp_ws, gmlp_bs, mlstm_conv_w,
           mlstm_gate_b, mlstm_norm_g, w_out, norm2_g, router_w, router_b, exp_w_gate, exp_w_up,
           exp_w_down, sh_w_gate, sh_w_up, sh_w_down, ple_w_in, ple_w_gate, ple_norm_g, final_norm_g):
    bsz, seq, d = x.shape
    depth = w_in.shape[0]
    t = bsz * seq
    cos_s, sin_s = _dft_tables(seq)
    cos_m, sin_m = _dft_tables(FOUR_HD)
    eye = np.eye(FOUR_HEADS)
    tables = (jnp.asarray(cos_s, BF16), jnp.asarray(sin_s, BF16),
              jnp.asarray(np.kron(eye, cos_m), BF16), jnp.asarray(np.kron(eye, sin_m), BF16))
    h = x.reshape(t, d)
    for i in range(depth):
        yf, yg, ym = _mixers(h, bsz, seq, norm1_g[i], w_in[i], four_w[i], gmlp_ln_g[i], gmlp_ln_b[i],
                             gmlp_ws[i], gmlp_bs[i], mlstm_conv_w[i], mlstm_gate_b[i], mlstm_norm_g[i], tables)
        h2 = _moe(i, h, yf, yg, ym, w_out[i], norm2_g[i], router_w[i], router_b[i],
                  exp_w_gate, exp_w_up, exp_w_down, sh_w_gate[i], sh_w_up[i], sh_w_down[i])
        h = _ple(h2, p[i].reshape(t, -1), ple_w_in[i].astype(BF16), ple_w_gate[i].astype(BF16),
                 ple_norm_g[i].reshape(1, d), final_norm_g.reshape(1, d), i == depth - 1)
    return h.reshape(bsz, seq, d)
```

```python
import functools
import math

import numpy as np
import jax
import jax.numpy as jnp
from jax import lax
from jax.experimental import pallas as pl
from jax.experimental.pallas import tpu as pltpu

F32, BF16, I32, U32 = jnp.float32, jnp.bfloat16, jnp.int32, jnp.uint32
HIGHEST = lax.Precision.HIGHEST

EPS = 1e-6
LANES = 128
VMEM_LIMIT_BYTES = 48 * 1024 * 1024

FOUR_HEADS, FOUR_HD = 4, 64
GMLP_HEADS, GMLP_HD, GMLP_CHUNK = 4, 64, 128
MLSTM_HEADS, MLSTM_DV, MLSTM_DQK, MLSTM_CHUNK = 4, 128, 64, 128
CONV_PAD = 16
N_GATES = 16
N_EXPERTS, TOP_K = 128, 8
EXPERT_BLOCK = 512
TOKEN_TILE = 512
DISPATCH_TILE = 256
COMBINE_TILE = 256
ISSUE_GROUP = 8


def _params(*sem):
    return pltpu.CompilerParams(dimension_semantics=sem, vmem_limit_bytes=VMEM_LIMIT_BYTES)


def _rms(x, g):
    return x * lax.rsqrt(jnp.mean(x * x, axis=-1, keepdims=True) + EPS) * g


def _full(shape):
    nd = len(shape)
    return pl.BlockSpec(shape, lambda *_: (0,) * nd)


def _dot(a, b):
    return jnp.dot(a, b, preferred_element_type=F32)


def _pack_row(x):
    n = x.shape[-1] // 2
    lo = lax.bitcast_convert_type(x[:, :n].astype(BF16).astype(F32), U32)
    hi = lax.bitcast_convert_type(x[:, n:].astype(BF16).astype(F32), U32)
    return hi | (lo >> 16)


def _unpack_row(w):
    lo = lax.bitcast_convert_type(w << 16, F32)
    hi = lax.bitcast_convert_type(w & jnp.uint32(0xFFFF0000), F32)
    return lo, hi


ROW_SUB = 4


def _store_rows(ref, w):
    m = w.shape[0]
    for c in range(ROW_SUB):
        ref[pl.ds(c, m, stride=ROW_SUB), :] = w[:, c * LANES:(c + 1) * LANES]


def _load_rows(ref, r0, m):
    return [ref[pl.ds(r0 * ROW_SUB + c, m, stride=ROW_SUB), :] for c in range(ROW_SUB)]


def _in_proj_body(h_ref, g_ref, w_ref, wg_ref, wgt_ref,
                  zf_ref, zu_ref, zv_ref, zqk_ref, zmv_ref, zo_ref, zg_ref, zgt_ref):
    ab = _rms(h_ref[...], g_ref[...]).astype(BF16)
    off = 0
    for o_ref in (zf_ref, zu_ref, zv_ref, zqk_ref, zmv_ref, zo_ref):
        n = o_ref.shape[-1]
        o_ref[...] = _dot(ab, w_ref[:, off:off + n]).astype(o_ref.dtype)
        off += n
    zg_ref[...] = _dot(ab, wg_ref[...])
    zgt_ref[...] = lax.dot_general(wgt_ref[...], ab, (((1,), (1,)), ((), ())),
                                   preferred_element_type=F32)


def _in_proj(h, g, w_main, w_gate, w_gate_t):
    t, d = h.shape
    tm = min(TOKEN_TILE, t)
    widths = (256, 256, 256, 512, 512, 512)
    row = lambda n: pl.BlockSpec((tm, n), lambda i: (i, 0))
    out_shape = [jax.ShapeDtypeStruct((t, n), BF16) for n in widths]
    out_shape += [jax.ShapeDtypeStruct((t, LANES), F32), jax.ShapeDtypeStruct((N_GATES, t), F32)]
    out_specs = [row(n) for n in widths]
    out_specs += [row(LANES), pl.BlockSpec((N_GATES, tm), lambda i: (0, i))]
    return pl.pallas_call(
        _in_proj_body, grid=(t // tm,),
        in_specs=[row(d), _full((1, d)), _full(w_main.shape), _full(w_gate.shape), _full(w_gate_t.shape)],
        out_specs=out_specs, out_shape=out_shape,
        compiler_params=_params("parallel"), name="in_proj",
    )(h, g, w_main, w_gate, w_gate_t)


def _dft_tables(n):
    k = np.arange(n, dtype=np.int64)
    ang = 2.0 * np.pi * ((k[:, None] * k[None, :]) % n).astype(np.float64) / n
    return np.cos(ang), np.sin(ang)


def _block_diag(blocks):
    h, a, b = blocks.shape
    eye = jnp.eye(h, dtype=blocks.dtype)
    return (eye[:, None, :, None] * blocks[:, :, None, :]).reshape(h * a, h * b)


def _fourier_body(scale, row_tile, z_ref, cs_ref, ss_ref, cm_ref, sm_ref, w_ref, o_ref, p_scr, q_scr):
    z = z_ref[0]
    p_scr[...] = _dot(z, cm_ref[...]).astype(BF16)
    q_scr[...] = _dot(z, sm_ref[...]).astype(BF16)
    s = z.shape[0]
    for r in range(s // row_tile):
        rs = slice(r * row_tile, (r + 1) * row_tile)
        re = (_dot(cs_ref[rs, :], p_scr[...]) - _dot(ss_ref[rs, :], q_scr[...])) * scale
        o_ref[0, rs, :] = _dot(re.astype(BF16), w_ref[...]).astype(o_ref.dtype)


def _fourier(zf, cs, ss, cm, sm, wbd):
    b, s, w = zf.shape
    row_tile = min(512, s)
    scale = 1.0 / math.sqrt(s * FOUR_HD)
    blk = pl.BlockSpec((1, s, w), lambda i: (i, 0, 0))
    return pl.pallas_call(
        functools.partial(_fourier_body, scale, row_tile), grid=(b,),
        in_specs=[blk, _full(cs.shape), _full(ss.shape), _full(cm.shape), _full(sm.shape), _full(wbd.shape)],
        out_specs=blk, out_shape=jax.ShapeDtypeStruct((b, s, w), BF16),
        scratch_shapes=[pltpu.VMEM((s, w), BF16), pltpu.VMEM((s, w), BF16)],
        compiler_params=_params("parallel"), name="fourier",
    )(zf, cs, ss, cm, sm, wbd)


def _gmlp_body(zu_ref, zv_ref, lg_ref, lb_ref, ws_ref, bs_ref, o_ref):
    s = zu_ref.shape[1]
    w = zu_ref.shape[2]
    lane = lax.broadcasted_iota(I32, (GMLP_CHUNK, w), 1)

    def chunk(c, carry):
        r0 = pl.multiple_of(c * GMLP_CHUNK, GMLP_CHUNK)
        rows = pl.ds(r0, GMLP_CHUNK)
        v = jax.nn.gelu(zv_ref[0, rows, :].astype(F32))
        vc = v - jnp.mean(v, axis=-1, keepdims=True)
        vn = vc * lax.rsqrt(jnp.mean(vc * vc, axis=-1, keepdims=True) + EPS) * lg_ref[...] + lb_ref[...]
        acc = bs_ref[...]
        for h in range(GMLP_HEADS):
            vh = jnp.where((lane >= h * GMLP_HD) & (lane < (h + 1) * GMLP_HD), vn, 0.0).astype(BF16)
            acc = acc + _dot(ws_ref[h], vh)
        u = jax.nn.gelu(zu_ref[0, rows, :].astype(F32))
        o_ref[0, rows, :] = (u * acc).astype(o_ref.dtype)
        return carry

    lax.fori_loop(0, s // GMLP_CHUNK, chunk, 0)


def _gmlp(zu, zv, ln_g, ln_b, ws, bs_full):
    b, s, w = zu.shape
    blk = pl.BlockSpec((1, s, w), lambda i: (i, 0, 0))
    return pl.pallas_call(
        _gmlp_body, grid=(b,),
        in_specs=[blk, blk, _full((1, w)), _full((1, w)), _full(ws.shape), _full(bs_full.shape)],
        out_specs=blk, out_shape=jax.ShapeDtypeStruct((b, s, w), BF16),
        compiler_params=_params("parallel"), name="gmlp",
    )(zu, zv, ln_g, ln_b, ws, bs_full)


def _mlstm_body(qkp_ref, v_ref, zo_ref, g_ref, gt_ref, cw_ref, gb_ref, gbt_ref, ng_ref, o_ref,
                qm_scr, km_scr, vt_scr, hf_scr, hb_scr, c_scr, n_scr, m_scr, gc_scr, gr_scr, cumc_scr, cumr_scr):
    L = MLSTM_CHUNK
    H = MLSTM_HEADS
    DV = MLSTM_DV
    s = v_ref.shape[1]
    nc = s // L
    qkw = H * MLSTM_DQK
    nt = (((1,), (1,)), ((), ()))

    ext = L + 2 * CONV_PAD
    r_i = lax.broadcasted_iota(I32, (L, ext), 0)
    c_i = lax.broadcasted_iota(I32, (L, ext), 1)
    sh_m1 = jnp.where(c_i == r_i + CONV_PAD - 1, 1.0, 0.0).astype(BF16)
    sh_0 = jnp.where(c_i == r_i + CONV_PAD, 1.0, 0.0).astype(BF16)
    sh_p1 = jnp.where(c_i == r_i + CONV_PAD + 1, 1.0, 0.0).astype(BF16)
    lane = lax.broadcasted_iota(I32, (L, LANES), 1)
    low_half = lane < MLSTM_DQK

    def conv_chunk(c, carry):
        r0 = pl.multiple_of(c * L, L)
        rows = pl.ds(r0, L)
        xe = qkp_ref[0, pl.ds(r0, ext), :]
        y = (cw_ref[0:1, :] * _dot(sh_m1, xe) + cw_ref[1:2, :] * _dot(sh_0, xe)
             + cw_ref[2:3, :] * _dot(sh_p1, xe))
        qk = y * jax.nn.sigmoid(y)
        for h in range(H):
            keep = low_half if h % 2 == 0 else jnp.logical_not(low_half)
            t0 = (h // 2) * LANES
            q_t = qk[:, t0:t0 + LANES] * (MLSTM_DQK ** -0.5)
            k_t = qk[:, qkw + t0:qkw + t0 + LANES]
            qm_scr[h, rows, :] = jnp.where(keep, q_t, 0.0).astype(BF16)
            km_scr[h, rows, :] = jnp.where(keep, k_t, 0.0).astype(BF16)
            vt_scr[h * DV:(h + 1) * DV, rows] = v_ref[0, rows, h * DV:(h + 1) * DV].astype(F32).T.astype(BF16)
        gc = g_ref[0, rows, :] + gb_ref[...]
        gr = gt_ref[:, rows] + gbt_ref[...]
        gc_scr[rows, :] = gc
        gr_scr[:, rows] = gr
        lf_c = jax.nn.log_sigmoid(gc)
        lf_r = jax.nn.log_sigmoid(gr)
        cumc_scr[0, rows, :] = jnp.dot(tri_l, lf_c, precision=HIGHEST, preferred_element_type=F32)
        cumc_scr[1, rows, :] = jnp.dot(tri_u, lf_c, precision=HIGHEST, preferred_element_type=F32)
        cumr_scr[0, :, rows] = jnp.dot(lf_r, tri_u, precision=HIGHEST, preferred_element_type=F32)
        cumr_scr[1, :, rows] = jnp.dot(lf_r, tri_l, precision=HIGHEST, preferred_element_type=F32)
        return carry

    ri = lax.broadcasted_iota(I32, (L, L), 0)
    ci = lax.broadcasted_iota(I32, (L, L), 1)
    tri_l = jnp.where(ci <= ri, 1.0, 0.0)
    tri_u = jnp.where(ci >= ri, 1.0, 0.0)
    lax.fori_loop(0, nc, conv_chunk, 0)

    c_scr[...] = jnp.zeros(c_scr.shape, F32)
    n_scr[...] = jnp.zeros(n_scr.shape, F32)
    m_scr[...] = jnp.zeros(m_scr.shape, F32)

    def rep8(row):
        return jnp.broadcast_to(row, (8, row.shape[-1])).astype(BF16)

    def direction(d, chunk):
        r0 = pl.multiple_of(chunk * L, L)
        rows = pl.ds(r0, L)
        gc = gc_scr[rows, :]
        gr = gr_scr[:, rows]
        cum_c = cumc_scr[d, rows, :]
        cum_r = cumr_scr[d, :, rows]
        mask = (ci >= ri) if d == 0 else (ci <= ri)
        i_lane = 2 * d * H
        f_lane = (2 * d + 1) * H
        for h in range(H):
            u = d * H + h
            b_r = cum_r[f_lane + h:f_lane + h + 1, :]
            li_r = gr[i_lane + h:i_lane + h + 1, :]
            key_c = gc[:, i_lane + h:i_lane + h + 1] - cum_c[:, f_lane + h:f_lane + h + 1]
            g_edge = b_r[:, L - 1:L] if d == 0 else b_r[:, 0:1]
            g_tot = jnp.broadcast_to(g_edge, (1, L))
            qh = qm_scr[h, rows, :]
            kh = km_scr[h, rows, :]
            vt = vt_scr[h * DV:(h + 1) * DV, rows]
            ct_prev = c_scr[u]
            n_prev = n_scr[u]
            m_prev = m_scr[u]
            dmat = jnp.where(mask, b_r + key_c, -jnp.inf)
            m_intra = jnp.max(dmat, axis=0, keepdims=True)
            m_inter = b_r + m_prev
            m_tot = jnp.maximum(m_intra, m_inter)
            kq = lax.dot_general(kh, qh, nt, preferred_element_type=F32)
            s_mat = jnp.exp(dmat - m_tot) * kq
            inter = jnp.exp(m_inter - m_tot)
            qn = lax.dot_general(rep8(n_prev), qh, nt, preferred_element_type=F32)[0:1, :]
            den = jnp.sum(s_mat, axis=0, keepdims=True) + inter * qn
            num = (_dot(vt, s_mat.astype(BF16))
                   + inter * lax.dot_general(ct_prev.astype(BF16), qh, nt, preferred_element_type=F32))
            dst = hf_scr if d == 0 else hb_scr
            dst[h * DV:(h + 1) * DV, rows] = num / jnp.maximum(jnp.abs(den), jnp.exp(-m_tot))
            a_r = g_tot - b_r + li_r
            m_loc = jnp.broadcast_to(jnp.max(a_r, axis=-1, keepdims=True), (1, L))
            w_r = jnp.exp(a_r - m_loc)
            ct_loc = _dot((vt.astype(F32) * w_r).astype(BF16), kh)
            n_loc = _dot(rep8(w_r), kh)[0:1, :]
            m_new = jnp.maximum(g_tot + m_prev, m_loc)
            a_old = jnp.exp(g_tot + m_prev - m_new)
            a_new = jnp.exp(m_loc - m_new)
            c_scr[u] = a_old * ct_prev + a_new * ct_loc
            n_scr[u] = a_old * n_prev + a_new * n_loc
            m_scr[u] = m_new

    def step(i, carry):
        direction(0, i)
        direction(1, nc - 1 - i)
        return carry

    lax.fori_loop(0, nc, step, 0)

    def finish(c, carry):
        r0 = pl.multiple_of(c * L, L)
        rows = pl.ds(r0, L)
        og = jax.nn.sigmoid(zo_ref[0, rows, :].astype(F32))
        for h in range(H):
            cols = slice(h * DV, (h + 1) * DV)
            x = og[:, cols] * (hf_scr[cols, rows] + hb_scr[cols, rows]).T
            xc = x - jnp.mean(x, axis=-1, keepdims=True)
            y = xc * lax.rsqrt(jnp.mean(xc * xc, axis=-1, keepdims=True) + EPS) * ng_ref[:, cols]
            o_ref[0, rows, cols] = y.astype(o_ref.dtype)
        return carry

    lax.fori_loop(0, nc, finish, 0)


def _mlstm(zqk_pad, zmv, zo, zg, zgt, conv_w, gate_b, gate_b_t, norm_g):
    b, s, w = zmv.shape
    H = MLSTM_HEADS
    blk = pl.BlockSpec((1, s, w), lambda i: (i, 0, 0))
    units = 2 * H
    return pl.pallas_call(
        _mlstm_body, grid=(b,),
        in_specs=[pl.BlockSpec((1, s + 2 * CONV_PAD, w), lambda i: (i, 0, 0)), blk, blk,
                  pl.BlockSpec((1, s, LANES), lambda i: (i, 0, 0)),
                  pl.BlockSpec((N_GATES, s), lambda i: (0, i)),
                  _full(conv_w.shape), _full(gate_b.shape), _full(gate_b_t.shape), _full(norm_g.shape)],
        out_specs=blk, out_shape=jax.ShapeDtypeStruct((b, s, w), BF16),
        scratch_shapes=[pltpu.VMEM((H, s, LANES), BF16), pltpu.VMEM((H, s, LANES), BF16),
                        pltpu.VMEM((w, s), BF16), pltpu.VMEM((w, s), F32), pltpu.VMEM((w, s), F32),
                        pltpu.VMEM((units, MLSTM_DV, LANES), F32), pltpu.VMEM((units, 1, LANES), F32),
                        pltpu.VMEM((units, 1, LANES), F32),
                        pltpu.VMEM((s, LANES), F32), pltpu.VMEM((N_GATES, s), F32),
                        pltpu.VMEM((2, s, LANES), F32), pltpu.VMEM((2, N_GATES, s), F32)],
        compiler_params=_params("parallel"), name="mlstm",
    )(zqk_pad, zmv, zo, zg, zgt, conv_w, gate_b, gate_b_t, norm_g)


def _post_mix_body(h_ref, yf_ref, yg_ref, ym_ref, wo_ref, g2_ref, rwt_ref, rbt_ref, sgu_ref, sd_ref,
                   hs_ref, xw_ref, sel_ref, gate_ref, idx_ref):
    wf, wg = yf_ref.shape[-1], yg_ref.shape[-1]
    h1 = (h_ref[...] + _dot(yf_ref[...], wo_ref[0:wf, :]) + _dot(yg_ref[...], wo_ref[wf:wf + wg, :])
          + _dot(ym_ref[...], wo_ref[wf + wg:, :]))
    xn = _rms(h1, g2_ref[...])
    xb = xn.astype(BF16)
    _store_rows(xw_ref, _pack_row(xn))
    gu = _dot(xb, sgu_ref[...])
    de = gu.shape[-1] // 2
    act = (jax.nn.silu(gu[:, :de]) * gu[:, de:]).astype(BF16)
    hs_ref[...] = h1 + _dot(act, sd_ref[...])
    scores = jax.nn.sigmoid(lax.dot_general(rwt_ref[...], xn, (((1,), (1,)), ((), ())),
                                            precision=HIGHEST, preferred_element_type=F32))
    work = scores + rbt_ref[...]
    expert = lax.broadcasted_iota(I32, scores.shape, 0)
    choice = lax.broadcasted_iota(I32, (TOP_K, scores.shape[1]), 0)
    sel = jnp.zeros(scores.shape, F32)
    idx = jnp.zeros(choice.shape, I32)
    for k in range(TOP_K):
        m = jnp.max(work, axis=0, keepdims=True)
        e = jnp.min(jnp.where(work == m, expert, N_EXPERTS), axis=0, keepdims=True)
        hit = expert == e
        sel = jnp.where(hit, 1.0, sel)
        idx = jnp.where(choice == k, e, idx)
        work = jnp.where(hit, -jnp.inf, work)
    picked = sel * scores
    sel_ref[...] = sel
    gate_ref[...] = picked / jnp.sum(picked, axis=0, keepdims=True)
    idx_ref[...] = idx


def _post_mix(h, yf, yg, ym, w_out, g2, router_w_t, router_b_t, sgu, sd):
    t, d = h.shape
    tm = min(TOKEN_TILE, t)
    row = lambda n: pl.BlockSpec((tm, n), lambda i: (i, 0))
    col = lambda n: pl.BlockSpec((n, tm), lambda i: (0, i))
    return pl.pallas_call(
        _post_mix_body, grid=(t // tm,),
        in_specs=[row(d), row(yf.shape[1]), row(yg.shape[1]), row(ym.shape[1]), _full(w_out.shape),
                  _full((1, d)), _full(router_w_t.shape), _full(router_b_t.shape), _full(sgu.shape), _full(sd.shape)],
        out_specs=[row(d), pl.BlockSpec((tm * ROW_SUB, LANES), lambda i: (i, 0)),
                   col(N_EXPERTS), col(N_EXPERTS), col(TOP_K)],
        out_shape=[jax.ShapeDtypeStruct((t, d), F32), jax.ShapeDtypeStruct((t * ROW_SUB, LANES), U32),
                   jax.ShapeDtypeStruct((N_EXPERTS, t), F32), jax.ShapeDtypeStruct((N_EXPERTS, t), F32),
                   jax.ShapeDtypeStruct((TOP_K, t), I32)],
        compiler_params=_params("parallel"), name="post_mix",
    )(h, yf, yg, ym, w_out, g2, router_w_t, router_b_t, sgu, sd)


def _rank_body(sel_ref, rank_ref, cnt_ref, carry):
    @pl.when(pl.program_id(0) == 0)
    def _():
        carry[...] = jnp.zeros(carry.shape, F32)

    m = sel_ref[...].astype(BF16)
    tm = m.shape[1]
    ri = lax.broadcasted_iota(I32, (tm, tm), 0)
    ci = lax.broadcasted_iota(I32, (tm, tm), 1)
    earlier = jnp.where(ri < ci, 1.0, 0.0).astype(BF16)
    seen = carry[...]
    rank = _dot(m, earlier) + jnp.concatenate([seen] * (tm // LANES), axis=1)
    rank_ref[...] = rank.astype(I32)
    seen = seen + _dot(m, jnp.ones((tm, LANES), BF16))
    carry[...] = seen
    cnt_ref[...] = seen.astype(I32)


def _rank(sel_t):
    e, t = sel_t.shape
    tm = min(TOKEN_TILE, t)
    return pl.pallas_call(
        _rank_body, grid=(t // tm,),
        in_specs=[pl.BlockSpec((e, tm), lambda i: (0, i))],
        out_specs=[pl.BlockSpec((e, tm), lambda i: (0, i)), _full((e, LANES))],
        out_shape=[jax.ShapeDtypeStruct((e, t), I32), jax.ShapeDtypeStruct((e, LANES), I32)],
        scratch_shapes=[pltpu.VMEM((e, LANES), F32)],
        compiler_params=_params("arbitrary"), name="rank",
    )(sel_t)


def _dest_body(rank_ref, idx_ref, gate_ref, offs_ref, dest_ref, g8_ref):
    tm = rank_ref.shape[1]
    offs = jnp.concatenate([offs_ref[...]] * (tm // LANES), axis=1)
    pos = (rank_ref[...] + offs).astype(F32)
    gates = gate_ref[...]
    expert = lax.broadcasted_iota(I32, pos.shape, 0)
    choice = lax.broadcasted_iota(I32, (TOP_K, tm), 0)
    dest = jnp.zeros(choice.shape, F32)
    g8 = jnp.zeros(choice.shape, F32)
    for k in range(TOP_K):
        hit = expert == idx_ref[k:k + 1, :]
        dk = jnp.sum(jnp.where(hit, pos, 0.0), axis=0, keepdims=True)
        gk = jnp.sum(jnp.where(hit, gates, 0.0), axis=0, keepdims=True)
        dest = jnp.where(choice == k, dk, dest)
        g8 = jnp.where(choice == k, gk, g8)
    dest_ref[...] = dest.astype(I32)
    g8_ref[...] = g8


def _dest(rank_t, idx_t, gates_t, offs_rep):
    e, t = rank_t.shape
    tm = min(TOKEN_TILE, t)
    col = lambda n: pl.BlockSpec((n, tm), lambda i: (0, i))
    return pl.pallas_call(
        _dest_body, grid=(t // tm,),
        in_specs=[col(e), col(TOP_K), col(e), _full((e, LANES))],
        out_specs=[col(TOP_K), col(TOP_K)],
        out_shape=[jax.ShapeDtypeStruct((TOP_K, t), I32), jax.ShapeDtypeStruct((TOP_K, t), F32)],
        compiler_params=_params("parallel"), name="dest",
    )(rank_t, idx_t, gates_t, offs_rep)


def _dispatch_body(dest_ref, tail_ref, x_ref, xs_hbm, zero_scr, sem):
    n_grp = x_ref.shape[0]

    @pl.when(pl.program_id(0) == 0)
    def _():
        zero_scr[...] = jnp.zeros(zero_scr.shape, zero_scr.dtype)

        def tail_copy(e):
            start = pl.multiple_of(jnp.maximum(tail_ref[e], 0) * ROW_SUB, EXPERT_BLOCK * ROW_SUB)
            return pltpu.make_async_copy(zero_scr, xs_hbm.at[pl.ds(start, EXPERT_BLOCK * ROW_SUB), :], sem)

        def clear(e, carry):
            @pl.when(tail_ref[e] >= 0)
            def _():
                tail_copy(e).start()
            return carry

        def clear_done(e, carry):
            @pl.when(tail_ref[e] >= 0)
            def _():
                tail_copy(e).wait()
            return carry

        lax.fori_loop(0, N_EXPERTS, clear, 0)
        lax.fori_loop(0, N_EXPERTS, clear_done, 0)

    def row_copy(group, g, dst_row):
        dst = pl.ds(pl.multiple_of(dst_row * ROW_SUB, ROW_SUB), ROW_SUB)
        return pltpu.make_async_copy(x_ref.at[group, g], xs_hbm.at[dst, :], sem)

    def for_group(op):
        def body(gi, carry):
            p0 = gi * (ISSUE_GROUP * TOP_K)
            for g in range(ISSUE_GROUP):
                for k in range(TOP_K):
                    op(row_copy(gi, g, dest_ref[p0 + g * TOP_K + k]), k)
            return carry
        lax.fori_loop(0, n_grp, body, 0)

    for_group(lambda cp, k: cp.start(priority=k % 2))
    for_group(lambda cp, k: cp.wait())


def _dispatch(dest_flat, tail_start, xw, n_rows):
    t = xw.shape[0] // ROW_SUB
    tt = min(DISPATCH_TILE, t)
    x4 = xw.reshape(t // ISSUE_GROUP, ISSUE_GROUP, ROW_SUB, LANES)
    return pl.pallas_call(
        _dispatch_body, grid=(t // tt,),
        in_specs=[pl.BlockSpec((tt * TOP_K,), lambda i: (i,), memory_space=pltpu.SMEM),
                  pl.BlockSpec(memory_space=pltpu.SMEM),
                  pl.BlockSpec((tt // ISSUE_GROUP, ISSUE_GROUP, ROW_SUB, LANES), lambda i: (i, 0, 0, 0))],
        out_specs=pl.BlockSpec(memory_space=pl.ANY),
        out_shape=jax.ShapeDtypeStruct((n_rows * ROW_SUB, LANES), xw.dtype),
        scratch_shapes=[pltpu.VMEM((EXPERT_BLOCK * ROW_SUB, LANES), xw.dtype), pltpu.SemaphoreType.DMA(())],
        compiler_params=_params("arbitrary"), name="dispatch",
    )(dest_flat, tail_start, x4)


def _experts_body(layer, blk_e_ref, n_used_ref, first_ref, next_e_ref, xs_ref, wg_hbm, wu_hbm, wd_hbm, ys_ref,
                  wg_buf, wu_buf, wd_buf, wg_scr, wu_scr, wd_scr, slot_ref, sems):
    i = pl.program_id(0)
    used = i < n_used_ref[0]

    def weight_copies(e, slot):
        return [pltpu.make_async_copy(hbm.at[layer, e], buf.at[slot], sems.at[slot])
                for hbm, buf in ((wg_hbm, wg_buf), (wu_hbm, wu_buf), (wd_hbm, wd_buf))]

    @pl.when(i == 0)
    def _():
        slot_ref[0] = 0
        for cp in weight_copies(blk_e_ref[0], 0):
            cp.start()

    @pl.when(first_ref[i] == 1)
    def _():
        slot = slot_ref[0]
        for cp in weight_copies(blk_e_ref[i], slot):
            cp.wait()
        wg_scr[...] = wg_buf[slot].astype(BF16)
        wu_scr[...] = wu_buf[slot].astype(BF16)
        wd_scr[...] = wd_buf[slot].astype(BF16)

        @pl.when(next_e_ref[i] >= 0)
        def _():
            for cp in weight_copies(next_e_ref[i], 1 - slot):
                cp.start()

        slot_ref[0] = 1 - slot

    @pl.when(used)
    def _():
        halves = [_unpack_row(w) for w in _load_rows(xs_ref, 0, EXPERT_BLOCK)]
        x = jnp.concatenate([lo for lo, _ in halves] + [hi for _, hi in halves], axis=-1).astype(BF16)
        a = (jax.nn.silu(_dot(x, wg_scr[...])) * _dot(x, wu_scr[...])).astype(BF16)
        _store_rows(ys_ref, _pack_row(_dot(a, wd_scr[...])))

    @pl.when(jnp.logical_not(used))
    def _():
        ys_ref[...] = jnp.zeros(ys_ref.shape, ys_ref.dtype)


def _experts(layer, blk_e, n_used, first, next_e, xs, wg, wu, wd):
    blk = (EXPERT_BLOCK * ROW_SUB, LANES)
    nb = xs.shape[0] // blk[0]
    d, de = wg.shape[-2], wg.shape[-1]
    last_used = lambda i, nu: jnp.minimum(i, nu[0] - 1)
    hbm = pl.BlockSpec(memory_space=pl.ANY)
    grid_spec = pltpu.PrefetchScalarGridSpec(
        num_scalar_prefetch=4, grid=(nb,),
        in_specs=[pl.BlockSpec(blk, lambda i, be, nu, fi, ne: (last_used(i, nu), 0)), hbm, hbm, hbm],
        out_specs=pl.BlockSpec(blk, lambda i, be, nu, fi, ne: (i, 0)),
        scratch_shapes=[pltpu.VMEM((2, d, de), F32), pltpu.VMEM((2, d, de), F32), pltpu.VMEM((2, de, d), F32),
                        pltpu.VMEM((d, de), BF16), pltpu.VMEM((d, de), BF16), pltpu.VMEM((de, d), BF16),
                        pltpu.SMEM((1,), I32), pltpu.SemaphoreType.DMA((2,))])
    return pl.pallas_call(
        functools.partial(_experts_body, layer), grid_spec=grid_spec,
        out_shape=jax.ShapeDtypeStruct(xs.shape, xs.dtype),
        compiler_params=_params("arbitrary"), name="experts",
    )(blk_e, n_used, first, next_e, xs, wg, wu, wd)


def _combine_body(dest_ref, dest_next_ref, g8_ref, hs_ref, ys_hbm, o_ref, buf, sems):
    n_tok = hs_ref.shape[0]
    n_grp = n_tok // ISSUE_GROUP
    step = pl.program_id(0)
    cur = step % 2
    half_rows = TOP_K * n_tok

    def for_group(idx_ref, slot, op):
        base = slot * (half_rows * ROW_SUB)

        def row_copy(src_row, k, gi, g):
            src = pl.ds(pl.multiple_of(src_row * ROW_SUB, ROW_SUB), ROW_SUB)
            dst = base + (k * n_grp + gi) * (ISSUE_GROUP * ROW_SUB) + g * ROW_SUB
            return pltpu.make_async_copy(ys_hbm.at[src, :], buf.at[pl.ds(pl.multiple_of(dst, ROW_SUB), ROW_SUB), :],
                                         sems.at[slot])

        def body(gi, carry):
            p0 = gi * (ISSUE_GROUP * TOP_K)
            for g in range(ISSUE_GROUP):
                for k in range(TOP_K):
                    op(row_copy(idx_ref[p0 + g * TOP_K + k], k, gi, g), k)
            return carry
        lax.fori_loop(0, n_grp, body, 0)

    start = lambda cp, k: cp.start(priority=k % 2)

    @pl.when(step == 0)
    def _():
        for_group(dest_ref, 0, start)

    @pl.when(step + 1 < pl.num_programs(0))
    def _():
        for_group(dest_next_ref, 1 - cur, start)

    for_group(dest_ref, cur, lambda cp, k: cp.wait())

    half = ROW_SUB * LANES

    def reduce(gi, carry):
        r0 = pl.multiple_of(gi * ISSUE_GROUP, ISSUE_GROUP)
        rows = pl.ds(r0, ISSUE_GROUP)
        gates = [g8_ref[rows, k:k + 1] for k in range(TOP_K)]
        picked = [_load_rows(buf, cur * half_rows + k * n_tok + r0, ISSUE_GROUP) for k in range(TOP_K)]
        for c in range(ROW_SUB):
            lo_cols = slice(c * LANES, (c + 1) * LANES)
            hi_cols = slice(half + c * LANES, half + (c + 1) * LANES)
            acc_lo = hs_ref[rows, lo_cols]
            acc_hi = hs_ref[rows, hi_cols]
            for k in range(TOP_K):
                lo, hi = _unpack_row(picked[k][c])
                acc_lo = acc_lo + gates[k] * lo
                acc_hi = acc_hi + gates[k] * hi
            o_ref[rows, lo_cols] = acc_lo
            o_ref[rows, hi_cols] = acc_hi
        return carry

    lax.fori_loop(0, n_grp, reduce, 0)


def _combine(dest_flat, g8, hs, ys):
    t, d = hs.shape
    tt = min(COMBINE_TILE, t)
    n_steps = t // tt
    return pl.pallas_call(
        _combine_body, grid=(n_steps,),
        in_specs=[pl.BlockSpec((tt * TOP_K,), lambda i: (i,), memory_space=pltpu.SMEM),
                  pl.BlockSpec((tt * TOP_K,), lambda i: (jnp.minimum(i + 1, n_steps - 1),), memory_space=pltpu.SMEM),
                  pl.BlockSpec((tt, g8.shape[1]), lambda i: (i, 0)),
                  pl.BlockSpec((tt, d), lambda i: (i, 0)),
                  pl.BlockSpec(memory_space=pl.ANY)],
        out_specs=pl.BlockSpec((tt, d), lambda i: (i, 0)),
        out_shape=jax.ShapeDtypeStruct((t, d), F32),
        scratch_shapes=[pltpu.VMEM((2 * TOP_K * tt * ROW_SUB, LANES), ys.dtype), pltpu.SemaphoreType.DMA((2,))],
        compiler_params=_params("arbitrary"), name="combine",
    )(dest_flat, dest_flat, g8, hs, ys)


def _ple_body(final, h2_ref, p_ref, win_ref, wgate_ref, g_ref, gf_ref, o_ref):
    h2 = h2_ref[...]
    e = _dot(p_ref[...].astype(BF16), win_ref[...])
    gate = jax.nn.sigmoid(_dot(h2.astype(BF16), wgate_ref[...]))
    h3 = h2 + _rms(gate * e, g_ref[...])
    o_ref[...] = _rms(h3, gf_ref[...]) if final else h3


def _ple(h2, p, w_in, w_gate, g, g_final, final):
    t, d = h2.shape
    tm = min(TOKEN_TILE, t)
    row = lambda n: pl.BlockSpec((tm, n), lambda i: (i, 0))
    return pl.pallas_call(
        functools.partial(_ple_body, final), grid=(t // tm,),
        in_specs=[row(d), row(p.shape[1]), _full(w_in.shape), _full(w_gate.shape), _full((1, d)), _full((1, d))],
        out_specs=row(d), out_shape=jax.ShapeDtypeStruct((t, d), F32),
        compiler_params=_params("parallel"), name="ple",
    )(h2, p, w_in, w_gate, g, g_final)


def _mixers(h, bsz, seq, norm1_g, w_in, four_w, ln_g, ln_b, ws, bs, conv_w, gate_b, norm_g, tables):
    d = h.shape[1]
    n_main = w_in.shape[1] - N_GATES
    w_main = w_in[:, :n_main].astype(BF16)
    w_g = w_in[:, n_main:]
    w_gate = jnp.pad(w_g, ((0, 0), (0, LANES - N_GATES))).astype(BF16)
    w_gate_t = w_g.T.astype(BF16)
    zf, zu, zv, zqk, zmv, zo, zg, zgt = _in_proj(h, norm1_g.reshape(1, d), w_main, w_gate, w_gate_t)
    b3 = lambda a: a.reshape(bsz, seq, a.shape[-1])

    cs, ss, cm, sm = tables
    yf = _fourier(b3(zf), cs, ss, cm, sm, _block_diag(four_w).astype(BF16))

    bs_full = jnp.repeat(bs.T, GMLP_HD, axis=1)
    yg = _gmlp(b3(zu), b3(zv), ln_g.reshape(1, -1), ln_b.reshape(1, -1), ws.astype(BF16), bs_full)

    zqk_pad = jnp.pad(b3(zqk), ((0, 0), (CONV_PAD, CONV_PAD), (0, 0)))
    gate_b_c = jnp.pad(gate_b, (0, LANES - N_GATES)).reshape(1, LANES)
    gate_b_t = jnp.broadcast_to(gate_b.reshape(N_GATES, 1), (N_GATES, LANES))
    ym = _mlstm(zqk_pad, b3(zmv), b3(zo), b3(zg), zgt, conv_w, gate_b_c, gate_b_t, norm_g.reshape(1, -1))
    t = bsz * seq
    return yf.reshape(t, -1), yg.reshape(t, -1), ym.reshape(t, -1)


def _moe(layer, h, yf, yg, ym, w_out, norm2_g, router_w, router_b, wg, wu, wd, sg, su, sd):
    t, d = h.shape
    sgu = jnp.concatenate([sg, su], axis=1).astype(BF16)
    tm = min(TOKEN_TILE, t)
    router_b_t = jnp.broadcast_to(router_b.reshape(-1, 1), (N_EXPERTS, tm))
    hs, xw, sel, gates, idx = _post_mix(h, yf, yg, ym, w_out.astype(BF16), norm2_g.reshape(1, d),
                                        router_w.T, router_b_t, sgu, sd.astype(BF16))
    rank, counts = _rank(sel)
    counts = counts[:, 0]
    pcounts = ((counts + EXPERT_BLOCK - 1) // EXPERT_BLOCK) * EXPERT_BLOCK
    pend = jnp.cumsum(pcounts)
    offs = (pend - pcounts).astype(I32)
    tail_start = jnp.where(pcounts > 0, pend - EXPERT_BLOCK, -1).astype(I32)
    n_blk = (t * TOP_K) // EXPERT_BLOCK + N_EXPERTS
    blk_start = jnp.arange(n_blk, dtype=I32) * EXPERT_BLOCK
    blk_e = jnp.minimum(jnp.sum(pend[None, :] <= blk_start[:, None], axis=1), N_EXPERTS - 1).astype(I32)
    n_used = (pend[-1:] // EXPERT_BLOCK).astype(I32)
    blk_i = jnp.arange(n_blk, dtype=I32)
    first = ((blk_i < n_used[0]) & ((blk_i == 0) | (blk_e != jnp.roll(blk_e, 1)))).astype(I32)
    after = pend[blk_e] // EXPERT_BLOCK
    next_e = jnp.where(after < n_used[0], blk_e[jnp.minimum(after, n_blk - 1)], -1).astype(I32)

    dest, g8 = _dest(rank, idx, gates, jnp.broadcast_to(offs.reshape(-1, 1), (N_EXPERTS, LANES)))
    dest_flat = dest.T.reshape(-1)
    g8 = jnp.pad(g8.T, ((0, 0), (0, LANES - TOP_K)))
    xs = _dispatch(dest_flat, tail_start, xw, n_blk * EXPERT_BLOCK)
    ys = _experts(layer, blk_e, n_used, first, next_e, xs, wg, wu, wd)
    return _combine(dest_flat, g8, hs, ys)


def kernel(x, p, norm1_g, w_in, four_w, gmlp_ln_g, gmlp_ln_b, gmlp_ws, gmlp_bs, mlstm_conv_w,
           mlstm_gate_b, mlstm_norm_g, w_out, norm2_g, router_w, router_b, exp_w_gate, exp_w_up,
           exp_w_down, sh_w_gate, sh_w_up, sh_w_down, ple_w_in, ple_w_gate, ple_norm_g, final_norm_g):
    bsz, seq, d = x.shape
    depth = w_in.shape[0]
    t = bsz * seq
    cos_s, sin_s = _dft_tables(seq)
    cos_m, sin_m = _dft_tables(FOUR_HD)
    eye = np.eye(FOUR_HEADS)
    tables = (jnp.asarray(cos_s, BF16), jnp.asarray(sin_s, BF16),
              jnp.asarray(np.kron(eye, cos_m), BF16), jnp.asarray(np.kron(eye, sin_m), BF16))
    h = x.reshape(t, d)
    for i in range(depth):
        yf, yg, ym = _mixers(h, bsz, seq, norm1_g[i], w_in[i], four_w[i], gmlp_ln_g[i], gmlp_ln_b[i],
                             gmlp_ws[i], gmlp_bs[i], mlstm_conv_w[i], mlstm_gate_b[i], mlstm_norm_g[i], tables)
        h2 = _moe(i, h, yf, yg, ym, w_out[i], norm2_g[i], router_w[i], router_b[i],
                  exp_w_gate, exp_w_up, exp_w_down, sh_w_gate[i], sh_w_up[i], sh_w_down[i])
        h = _ple(h2, p[i].reshape(t, -1), ple_w_in[i].astype(BF16), ple_w_gate[i].astype(BF16),
                 ple_norm_g[i].reshape(1, d), final_norm_g.reshape(1, d), i == depth - 1)
    return h.reshape(bsz, seq, d)
```

```python
import functools
import math

import numpy as np
import jax
import jax.numpy as jnp
from jax import lax
from jax.experimental import pallas as pl
from jax.experimental.pallas import tpu as pltpu

F32, BF16, I32, U32 = jnp.float32, jnp.bfloat16, jnp.int32, jnp.uint32
HIGHEST = lax.Precision.HIGHEST

EPS = 1e-6
LANES = 128
VMEM_LIMIT_BYTES = 48 * 1024 * 1024

FOUR_HEADS, FOUR_HD = 4, 64
GMLP_HEADS, GMLP_HD, GMLP_CHUNK = 4, 64, 128
MLSTM_HEADS, MLSTM_DV, MLSTM_DQK, MLSTM_CHUNK = 4, 128, 64, 128
CONV_PAD = 16
N_GATES = 16
N_EXPERTS, TOP_K = 128, 8
EXPERT_BLOCK = 512
TOKEN_TILE = 512
DISPATCH_TILE = 512
COMBINE_TILE = 512
ISSUE_GROUP = 8


def _params(*sem):
    return pltpu.CompilerParams(dimension_semantics=sem, vmem_limit_bytes=VMEM_LIMIT_BYTES)


def _rms(x, g):
    return x * lax.rsqrt(jnp.mean(x * x, axis=-1, keepdims=True) + EPS) * g


def _full(shape):
    nd = len(shape)
    return pl.BlockSpec(shape, lambda *_: (0,) * nd)


def _dot(a, b):
    return jnp.dot(a, b, preferred_element_type=F32)


def _pack_row(x):
    n = x.shape[-1] // 2
    lo = lax.bitcast_convert_type(x[:, :n].astype(BF16).astype(F32), U32)
    hi = lax.bitcast_convert_type(x[:, n:].astype(BF16).astype(F32), U32)
    return hi | (lo >> 16)


def _unpack_row(w):
    lo = lax.bitcast_convert_type(w << 16, F32)
    hi = lax.bitcast_convert_type(w & jnp.uint32(0xFFFF0000), F32)
    return lo, hi


ROW_SUB = 4


def _store_rows(ref, w):
    m = w.shape[0]
    for c in range(ROW_SUB):
        ref[pl.ds(c, m, stride=ROW_SUB), :] = w[:, c * LANES:(c + 1) * LANES]


def _load_rows(ref, r0, m):
    return [ref[pl.ds(r0 * ROW_SUB + c, m, stride=ROW_SUB), :] for c in range(ROW_SUB)]


def _in_proj_body(h_ref, g_ref, w_ref, wg_ref, wgt_ref,
                  zf_ref, zu_ref, zv_ref, zqk_ref, zmv_ref, zo_ref, zg_ref, zgt_ref):
    ab = _rms(h_ref[...], g_ref[...]).astype(BF16)
    off = 0
    for o_ref in (zf_ref, zu_ref, zv_ref, zqk_ref, zmv_ref, zo_ref):
        n = o_ref.shape[-1]
        o_ref[...] = _dot(ab, w_ref[:, off:off + n]).astype(o_ref.dtype)
        off += n
    zg_ref[...] = _dot(ab, wg_ref[...])
    zgt_ref[...] = lax.dot_general(wgt_ref[...], ab, (((1,), (1,)), ((), ())),
                                   preferred_element_type=F32)


def _in_proj(h, g, w_main, w_gate, w_gate_t):
    t, d = h.shape
    tm = min(TOKEN_TILE, t)
    widths = (256, 256, 256, 512, 512, 512)
    row = lambda n: pl.BlockSpec((tm, n), lambda i: (i, 0))
    out_shape = [jax.ShapeDtypeStruct((t, n), BF16) for n in widths]
    out_shape += [jax.ShapeDtypeStruct((t, LANES), F32), jax.ShapeDtypeStruct((N_GATES, t), F32)]
    out_specs = [row(n) for n in widths]
    out_specs += [row(LANES), pl.BlockSpec((N_GATES, tm), lambda i: (0, i))]
    return pl.pallas_call(
        _in_proj_body, grid=(t // tm,),
        in_specs=[row(d), _full((1, d)), _full(w_main.shape), _full(w_gate.shape), _full(w_gate_t.shape)],
        out_specs=out_specs, out_shape=out_shape,
        compiler_params=_params("parallel"), name="in_proj",
    )(h, g, w_main, w_gate, w_gate_t)


def _dft_tables(n):
    k = np.arange(n, dtype=np.int64)
    ang = 2.0 * np.pi * ((k[:, None] * k[None, :]) % n).astype(np.float64) / n
    return np.cos(ang), np.sin(ang)


def _block_diag(blocks):
    h, a, b = blocks.shape
    eye = jnp.eye(h, dtype=blocks.dtype)
    return (eye[:, None, :, None] * blocks[:, :, None, :]).reshape(h * a, h * b)


def _fourier_body(scale, row_tile, z_ref, cs_ref, ss_ref, cm_ref, sm_ref, w_ref, o_ref, p_scr, q_scr):
    z = z_ref[0]
    p_scr[...] = _dot(z, cm_ref[...]).astype(BF16)
    q_scr[...] = _dot(z, sm_ref[...]).astype(BF16)
    s = z.shape[0]
    for r in range(s // row_tile):
        rs = slice(r * row_tile, (r + 1) * row_tile)
        re = (_dot(cs_ref[rs, :], p_scr[...]) - _dot(ss_ref[rs, :], q_scr[...])) * scale
        o_ref[0, rs, :] = _dot(re.astype(BF16), w_ref[...]).astype(o_ref.dtype)


def _fourier(zf, cs, ss, cm, sm, wbd):
    b, s, w = zf.shape
    row_tile = min(512, s)
    scale = 1.0 / math.sqrt(s * FOUR_HD)
    blk = pl.BlockSpec((1, s, w), lambda i: (i, 0, 0))
    return pl.pallas_call(
        functools.partial(_fourier_body, scale, row_tile), grid=(b,),
        in_specs=[blk, _full(cs.shape), _full(ss.shape), _full(cm.shape), _full(sm.shape), _full(wbd.shape)],
        out_specs=blk, out_shape=jax.ShapeDtypeStruct((b, s, w), BF16),
        scratch_shapes=[pltpu.VMEM((s, w), BF16), pltpu.VMEM((s, w), BF16)],
        compiler_params=_params("parallel"), name="fourier",
    )(zf, cs, ss, cm, sm, wbd)


def _gmlp_body(zu_ref, zv_ref, lg_ref, lb_ref, ws_ref, bs_ref, o_ref):
    s = zu_ref.shape[1]
    w = zu_ref.shape[2]
    lane = lax.broadcasted_iota(I32, (GMLP_CHUNK, w), 1)

    def chunk(c, carry):
        r0 = pl.multiple_of(c * GMLP_CHUNK, GMLP_CHUNK)
        rows = pl.ds(r0, GMLP_CHUNK)
        v = jax.nn.gelu(zv_ref[0, rows, :].astype(F32))
        vc = v - jnp.mean(v, axis=-1, keepdims=True)
        vn = vc * lax.rsqrt(jnp.mean(vc * vc, axis=-1, keepdims=True) + EPS) * lg_ref[...] + lb_ref[...]
        acc = bs_ref[...]
        for h in range(GMLP_HEADS):
            vh = jnp.where((lane >= h * GMLP_HD) & (lane < (h + 1) * GMLP_HD), vn, 0.0).astype(BF16)
            acc = acc + _dot(ws_ref[h], vh)
        u = jax.nn.gelu(zu_ref[0, rows, :].astype(F32))
        o_ref[0, rows, :] = (u * acc).astype(o_ref.dtype)
        return carry

    lax.fori_loop(0, s // GMLP_CHUNK, chunk, 0)


def _gmlp(zu, zv, ln_g, ln_b, ws, bs_full):
    b, s, w = zu.shape
    blk = pl.BlockSpec((1, s, w), lambda i: (i, 0, 0))
    return pl.pallas_call(
        _gmlp_body, grid=(b,),
        in_specs=[blk, blk, _full((1, w)), _full((1, w)), _full(ws.shape), _full(bs_full.shape)],
        out_specs=blk, out_shape=jax.ShapeDtypeStruct((b, s, w), BF16),
        compiler_params=_params("parallel"), name="gmlp",
    )(zu, zv, ln_g, ln_b, ws, bs_full)


def _mlstm_body(qkp_ref, v_ref, zo_ref, g_ref, gt_ref, cw_ref, gb_ref, gbt_ref, ng_ref, o_ref,
                qm_scr, km_scr, vt_scr, hf_scr, hb_scr, c_scr, n_scr, m_scr, gc_scr, gr_scr, cumc_scr, cumr_scr):
    L = MLSTM_CHUNK
    H = MLSTM_HEADS
    DV = MLSTM_DV
    s = v_ref.shape[1]
    nc = s // L
    qkw = H * MLSTM_DQK
    nt = (((1,), (1,)), ((), ()))

    ext = L + 2 * CONV_PAD
    r_i = lax.broadcasted_iota(I32, (L, ext), 0)
    c_i = lax.broadcasted_iota(I32, (L, ext), 1)
    sh_m1 = jnp.where(c_i == r_i + CONV_PAD - 1, 1.0, 0.0).astype(BF16)
    sh_0 = jnp.where(c_i == r_i + CONV_PAD, 1.0, 0.0).astype(BF16)
    sh_p1 = jnp.where(c_i == r_i + CONV_PAD + 1, 1.0, 0.0).astype(BF16)
    lane = lax.broadcasted_iota(I32, (L, LANES), 1)
    low_half = lane < MLSTM_DQK

    def conv_chunk(c, carry):
        r0 = pl.multiple_of(c * L, L)
        rows = pl.ds(r0, L)
        xe = qkp_ref[0, pl.ds(r0, ext), :]
        y = (cw_ref[0:1, :] * _dot(sh_m1, xe) + cw_ref[1:2, :] * _dot(sh_0, xe)
             + cw_ref[2:3, :] * _dot(sh_p1, xe))
        qk = y * jax.nn.sigmoid(y)
        for h in range(H):
            keep = low_half if h % 2 == 0 else jnp.logical_not(low_half)
            t0 = (h // 2) * LANES
            q_t = qk[:, t0:t0 + LANES] * (MLSTM_DQK ** -0.5)
            k_t = qk[:, qkw + t0:qkw + t0 + LANES]
            qm_scr[h, rows, :] = jnp.where(keep, q_t, 0.0).astype(BF16)
            km_scr[h, rows, :] = jnp.where(keep, k_t, 0.0).astype(BF16)
            vt_scr[h * DV:(h + 1) * DV, rows] = v_ref[0, rows, h * DV:(h + 1) * DV].astype(F32).T.astype(BF16)
        gc = g_ref[0, rows, :] + gb_ref[...]
        gr = gt_ref[:, rows] + gbt_ref[...]
        gc_scr[rows, :] = gc
        gr_scr[:, rows] = gr
        lf_c = jax.nn.log_sigmoid(gc)
        lf_r = jax.nn.log_sigmoid(gr)
        cumc_scr[0, rows, :] = jnp.dot(tri_l, lf_c, precision=HIGHEST, preferred_element_type=F32)
        cumc_scr[1, rows, :] = jnp.dot(tri_u, lf_c, precision=HIGHEST, preferred_element_type=F32)
        cumr_scr[0, :, rows] = jnp.dot(lf_r, tri_u, precision=HIGHEST, preferred_element_type=F32)
        cumr_scr[1, :, rows] = jnp.dot(lf_r, tri_l, precision=HIGHEST, preferred_element_type=F32)
        return carry

    ri = lax.broadcasted_iota(I32, (L, L), 0)
    ci = lax.broadcasted_iota(I32, (L, L), 1)
    tri_l = jnp.where(ci <= ri, 1.0, 0.0)
    tri_u = jnp.where(ci >= ri, 1.0, 0.0)
    lax.fori_loop(0, nc, conv_chunk, 0)

    c_scr[...] = jnp.zeros(c_scr.shape, F32)
    n_scr[...] = jnp.zeros(n_scr.shape, F32)
    m_scr[...] = jnp.zeros(m_scr.shape, F32)

    def rep8(row):
        return jnp.broadcast_to(row, (8, row.shape[-1])).astype(BF16)

    def direction(d, chunk):
        r0 = pl.multiple_of(chunk * L, L)
        rows = pl.ds(r0, L)
        gc = gc_scr[rows, :]
        gr = gr_scr[:, rows]
        cum_c = cumc_scr[d, rows, :]
        cum_r = cumr_scr[d, :, rows]
        mask = (ci >= ri) if d == 0 else (ci <= ri)
        i_lane = 2 * d * H
        f_lane = (2 * d + 1) * H
        for h in range(H):
            u = d * H + h
            b_r = cum_r[f_lane + h:f_lane + h + 1, :]
            li_r = gr[i_lane + h:i_lane + h + 1, :]
            key_c = gc[:, i_lane + h:i_lane + h + 1] - cum_c[:, f_lane + h:f_lane + h + 1]
            g_edge = b_r[:, L - 1:L] if d == 0 else b_r[:, 0:1]
            g_tot = jnp.broadcast_to(g_edge, (1, L))
            qh = qm_scr[h, rows, :]
            kh = km_scr[h, rows, :]
            vt = vt_scr[h * DV:(h + 1) * DV, rows]
            ct_prev = c_scr[u]
            n_prev = n_scr[u]
            m_prev = m_scr[u]
            dmat = jnp.where(mask, b_r + key_c, -jnp.inf)
            m_intra = jnp.max(dmat, axis=0, keepdims=True)
            m_inter = b_r + m_prev
            m_tot = jnp.maximum(m_intra, m_inter)
            kq = lax.dot_general(kh, qh, nt, preferred_element_type=F32)
            s_mat = jnp.exp(dmat - m_tot) * kq
            inter = jnp.exp(m_inter - m_tot)
            qn = lax.dot_general(rep8(n_prev), qh, nt, preferred_element_type=F32)[0:1, :]
            den = jnp.sum(s_mat, axis=0, keepdims=True) + inter * qn
            num = (_dot(vt, s_mat.astype(BF16))
                   + inter * lax.dot_general(ct_prev.astype(BF16), qh, nt, preferred_element_type=F32))
            dst = hf_scr if d == 0 else hb_scr
            dst[h * DV:(h + 1) * DV, rows] = num / jnp.maximum(jnp.abs(den), jnp.exp(-m_tot))
            a_r = g_tot - b_r + li_r
            m_loc = jnp.broadcast_to(jnp.max(a_r, axis=-1, keepdims=True), (1, L))
            w_r = jnp.exp(a_r - m_loc)
            ct_loc = _dot((vt.astype(F32) * w_r).astype(BF16), kh)
            n_loc = _dot(rep8(w_r), kh)[0:1, :]
            m_new = jnp.maximum(g_tot + m_prev, m_loc)
            a_old = jnp.exp(g_tot + m_prev - m_new)
            a_new = jnp.exp(m_loc - m_new)
            c_scr[u] = a_old * ct_prev + a_new * ct_loc
            n_scr[u] = a_old * n_prev + a_new * n_loc
            m_scr[u] = m_new

    def step(i, carry):
        direction(0, i)
        direction(1, nc - 1 - i)
        return carry

    lax.fori_loop(0, nc, step, 0)

    def finish(c, carry):
        r0 = pl.multiple_of(c * L, L)
        rows = pl.ds(r0, L)
        og = jax.nn.sigmoid(zo_ref[0, rows, :].astype(F32))
        for h in range(H):
            cols = slice(h * DV, (h + 1) * DV)
            x = og[:, cols] * (hf_scr[cols, rows] + hb_scr[cols, rows]).T
            xc = x - jnp.mean(x, axis=-1, keepdims=True)
            y = xc * lax.rsqrt(jnp.mean(xc * xc, axis=-1, keepdims=True) + EPS) * ng_ref[:, cols]
            o_ref[0, rows, cols] = y.astype(o_ref.dtype)
        return carry

    lax.fori_loop(0, nc, finish, 0)


def _mlstm(zqk_pad, zmv, zo, zg, zgt, conv_w, gate_b, gate_b_t, norm_g):
    b, s, w = zmv.shape
    H = MLSTM_HEADS
    blk = pl.BlockSpec((1, s, w), lambda i: (i, 0, 0))
    units = 2 * H
    return pl.pallas_call(
        _mlstm_body, grid=(b,),
        in_specs=[pl.BlockSpec((1, s + 2 * CONV_PAD, w), lambda i: (i, 0, 0)), blk, blk,
                  pl.BlockSpec((1, s, LANES), lambda i: (i, 0, 0)),
                  pl.BlockSpec((N_GATES, s), lambda i: (0, i)),
                  _full(conv_w.shape), _full(gate_b.shape), _full(gate_b_t.shape), _full(norm_g.shape)],
        out_specs=blk, out_shape=jax.ShapeDtypeStruct((b, s, w), BF16),
        scratch_shapes=[pltpu.VMEM((H, s, LANES), BF16), pltpu.VMEM((H, s, LANES), BF16),
                        pltpu.VMEM((w, s), BF16), pltpu.VMEM((w, s), F32), pltpu.VMEM((w, s), F32),
                        pltpu.VMEM((units, MLSTM_DV, LANES), F32), pltpu.VMEM((units, 1, LANES), F32),
                        pltpu.VMEM((units, 1, LANES), F32),
                        pltpu.VMEM((s, LANES), F32), pltpu.VMEM((N_GATES, s), F32),
                        pltpu.VMEM((2, s, LANES), F32), pltpu.VMEM((2, N_GATES, s), F32)],
        compiler_params=_params("parallel"), name="mlstm",
    )(zqk_pad, zmv, zo, zg, zgt, conv_w, gate_b, gate_b_t, norm_g)


def _post_mix_body(h_ref, yf_ref, yg_ref, ym_ref, wo_ref, g2_ref, rwt_ref, rbt_ref, sgu_ref, sd_ref,
                   hs_ref, xw_ref, sel_ref, gate_ref, idx_ref):
    wf, wg = yf_ref.shape[-1], yg_ref.shape[-1]
    h1 = (h_ref[...] + _dot(yf_ref[...], wo_ref[0:wf, :]) + _dot(yg_ref[...], wo_ref[wf:wf + wg, :])
          + _dot(ym_ref[...], wo_ref[wf + wg:, :]))
    xn = _rms(h1, g2_ref[...])
    xb = xn.astype(BF16)
    _store_rows(xw_ref, _pack_row(xn))
    gu = _dot(xb, sgu_ref[...])
    de = gu.shape[-1] // 2
    act = (jax.nn.silu(gu[:, :de]) * gu[:, de:]).astype(BF16)
    hs_ref[...] = h1 + _dot(act, sd_ref[...])
    scores = jax.nn.sigmoid(lax.dot_general(rwt_ref[...], xn, (((1,), (1,)), ((), ())),
                                            precision=HIGHEST, preferred_element_type=F32))
    work = scores + rbt_ref[...]
    expert = lax.broadcasted_iota(I32, scores.shape, 0)
    choice = lax.broadcasted_iota(I32, (TOP_K, scores.shape[1]), 0)
    sel = jnp.zeros(scores.shape, F32)
    idx = jnp.zeros(choice.shape, I32)
    for k in range(TOP_K):
        m = jnp.max(work, axis=0, keepdims=True)
        e = jnp.min(jnp.where(work == m, expert, N_EXPERTS), axis=0, keepdims=True)
        hit = expert == e
        sel = jnp.where(hit, 1.0, sel)
        idx = jnp.where(choice == k, e, idx)
        work = jnp.where(hit, -jnp.inf, work)
    picked = sel * scores
    sel_ref[...] = sel
    gate_ref[...] = picked / jnp.sum(picked, axis=0, keepdims=True)
    idx_ref[...] = idx


def _post_mix(h, yf, yg, ym, w_out, g2, router_w_t, router_b_t, sgu, sd):
    t, d = h.shape
    tm = min(TOKEN_TILE, t)
    row = lambda n: pl.BlockSpec((tm, n), lambda i: (i, 0))
    col = lambda n: pl.BlockSpec((n, tm), lambda i: (0, i))
    return pl.pallas_call(
        _post_mix_body, grid=(t // tm,),
        in_specs=[row(d), row(yf.shape[1]), row(yg.shape[1]), row(ym.shape[1]), _full(w_out.shape),
                  _full((1, d)), _full(router_w_t.shape), _full(router_b_t.shape), _full(sgu.shape), _full(sd.shape)],
        out_specs=[row(d), pl.BlockSpec((tm * ROW_SUB, LANES), lambda i: (i, 0)),
                   col(N_EXPERTS), col(N_EXPERTS), col(TOP_K)],
        out_shape=[jax.ShapeDtypeStruct((t, d), F32), jax.ShapeDtypeStruct((t * ROW_SUB, LANES), U32),
                   jax.ShapeDtypeStruct((N_EXPERTS, t), F32), jax.ShapeDtypeStruct((N_EXPERTS, t), F32),
                   jax.ShapeDtypeStruct((TOP_K, t), I32)],
        compiler_params=_params("parallel"), name="post_mix",
    )(h, yf, yg, ym, w_out, g2, router_w_t, router_b_t, sgu, sd)


def _rank_body(sel_ref, rank_ref, cnt_ref, carry):
    @pl.when(pl.program_id(0) == 0)
    def _():
        carry[...] = jnp.zeros(carry.shape, F32)

    m = sel_ref[...].astype(BF16)
    tm = m.shape[1]
    ri = lax.broadcasted_iota(I32, (tm, tm), 0)
    ci = lax.broadcasted_iota(I32, (tm, tm), 1)
    earlier = jnp.where(ri < ci, 1.0, 0.0).astype(BF16)
    seen = carry[...]
    rank = _dot(m, earlier) + jnp.concatenate([seen] * (tm // LANES), axis=1)
    rank_ref[...] = rank.astype(I32)
    seen = seen + _dot(m, jnp.ones((tm, LANES), BF16))
    carry[...] = seen
    cnt_ref[...] = seen.astype(I32)


def _rank(sel_t):
    e, t = sel_t.shape
    tm = min(TOKEN_TILE, t)
    return pl.pallas_call(
        _rank_body, grid=(t // tm,),
        in_specs=[pl.BlockSpec((e, tm), lambda i: (0, i))],
        out_specs=[pl.BlockSpec((e, tm), lambda i: (0, i)), _full((e, LANES))],
        out_shape=[jax.ShapeDtypeStruct((e, t), I32), jax.ShapeDtypeStruct((e, LANES), I32)],
        scratch_shapes=[pltpu.VMEM((e, LANES), F32)],
        compiler_params=_params("arbitrary"), name="rank",
    )(sel_t)


def _dest_body(rank_ref, idx_ref, gate_ref, offs_ref, dest_ref, g8_ref):
    tm = rank_ref.shape[1]
    offs = jnp.concatenate([offs_ref[...]] * (tm // LANES), axis=1)
    pos = (rank_ref[...] + offs).astype(F32)
    gates = gate_ref[...]
    expert = lax.broadcasted_iota(I32, pos.shape, 0)
    choice = lax.broadcasted_iota(I32, (TOP_K, tm), 0)
    dest = jnp.zeros(choice.shape, F32)
    g8 = jnp.zeros(choice.shape, F32)
    for k in range(TOP_K):
        hit = expert == idx_ref[k:k + 1, :]
        dk = jnp.sum(jnp.where(hit, pos, 0.0), axis=0, keepdims=True)
        gk = jnp.sum(jnp.where(hit, gates, 0.0), axis=0, keepdims=True)
        dest = jnp.where(choice == k, dk, dest)
        g8 = jnp.where(choice == k, gk, g8)
    dest_ref[...] = dest.astype(I32)
    g8_ref[...] = g8


def _dest(rank_t, idx_t, gates_t, offs_rep):
    e, t = rank_t.shape
    tm = min(TOKEN_TILE, t)
    col = lambda n: pl.BlockSpec((n, tm), lambda i: (0, i))
    return pl.pallas_call(
        _dest_body, grid=(t // tm,),
        in_specs=[col(e), col(TOP_K), col(e), _full((e, LANES))],
        out_specs=[col(TOP_K), col(TOP_K)],
        out_shape=[jax.ShapeDtypeStruct((TOP_K, t), I32), jax.ShapeDtypeStruct((TOP_K, t), F32)],
        compiler_params=_params("parallel"), name="dest",
    )(rank_t, idx_t, gates_t, offs_rep)


def _dispatch_body(dest_ref, tail_ref, x_ref, xs_hbm, zero_scr, sem):
    n_grp = x_ref.shape[0]

    @pl.when(pl.program_id(0) == 0)
    def _():
        zero_scr[...] = jnp.zeros(zero_scr.shape, zero_scr.dtype)

        def tail_copy(e):
            start = pl.multiple_of(jnp.maximum(tail_ref[e], 0) * ROW_SUB, EXPERT_BLOCK * ROW_SUB)
            return pltpu.make_async_copy(zero_scr, xs_hbm.at[pl.ds(start, EXPERT_BLOCK * ROW_SUB), :], sem)

        def clear(e, carry):
            @pl.when(tail_ref[e] >= 0)
            def _():
                tail_copy(e).start()
            return carry

        def clear_done(e, carry):
            @pl.when(tail_ref[e] >= 0)
            def _():
                tail_copy(e).wait()
            return carry

        lax.fori_loop(0, N_EXPERTS, clear, 0)
        lax.fori_loop(0, N_EXPERTS, clear_done, 0)

    def row_copy(group, g, dst_row):
        dst = pl.ds(pl.multiple_of(dst_row * ROW_SUB, ROW_SUB), ROW_SUB)
        return pltpu.make_async_copy(x_ref.at[group, g], xs_hbm.at[dst, :], sem)

    def for_group(op):
        def body(gi, carry):
            p0 = gi * (ISSUE_GROUP * TOP_K)
            for g in range(ISSUE_GROUP):
                for k in range(TOP_K):
                    op(row_copy(gi, g, dest_ref[p0 + g * TOP_K + k]), k)
            return carry
        lax.fori_loop(0, n_grp, body, 0)

    for_group(lambda cp, k: cp.start(priority=k % 2))
    for_group(lambda cp, k: cp.wait())


def _dispatch(dest_flat, tail_start, xw, n_rows):
    t = xw.shape[0] // ROW_SUB
    tt = min(DISPATCH_TILE, t)
    x4 = xw.reshape(t // ISSUE_GROUP, ISSUE_GROUP, ROW_SUB, LANES)
    return pl.pallas_call(
        _dispatch_body, grid=(t // tt,),
        in_specs=[pl.BlockSpec((tt * TOP_K,), lambda i: (i,), memory_space=pltpu.SMEM),
                  pl.BlockSpec(memory_space=pltpu.SMEM),
                  pl.BlockSpec((tt // ISSUE_GROUP, ISSUE_GROUP, ROW_SUB, LANES), lambda i: (i, 0, 0, 0))],
        out_specs=pl.BlockSpec(memory_space=pl.ANY),
        out_shape=jax.ShapeDtypeStruct((n_rows * ROW_SUB, LANES), xw.dtype),
        scratch_shapes=[pltpu.VMEM((EXPERT_BLOCK * ROW_SUB, LANES), xw.dtype), pltpu.SemaphoreType.DMA(())],
        compiler_params=_params("arbitrary"), name="dispatch",
    )(dest_flat, tail_start, x4)


def _experts_body(layer, blk_e_ref, n_used_ref, first_ref, next_e_ref, xs_ref, wg_hbm, wu_hbm, wd_hbm, ys_ref,
                  wg_buf, wu_buf, wd_buf, wg_scr, wu_scr, wd_scr, slot_ref, sems):
    i = pl.program_id(0)
    used = i < n_used_ref[0]

    def weight_copies(e, slot):
        return [pltpu.make_async_copy(hbm.at[layer, e], buf.at[slot], sems.at[slot])
                for hbm, buf in ((wg_hbm, wg_buf), (wu_hbm, wu_buf), (wd_hbm, wd_buf))]

    @pl.when(i == 0)
    def _():
        slot_ref[0] = 0
        for cp in weight_copies(blk_e_ref[0], 0):
            cp.start()

    @pl.when(first_ref[i] == 1)
    def _():
        slot = slot_ref[0]
        for cp in weight_copies(blk_e_ref[i], slot):
            cp.wait()
        wg_scr[...] = wg_buf[slot].astype(BF16)
        wu_scr[...] = wu_buf[slot].astype(BF16)
        wd_scr[...] = wd_buf[slot].astype(BF16)

        @pl.when(next_e_ref[i] >= 0)
        def _():
            for cp in weight_copies(next_e_ref[i], 1 - slot):
                cp.start(priority=1)

        slot_ref[0] = 1 - slot

    @pl.when(used)
    def _():
        halves = [_unpack_row(w) for w in _load_rows(xs_ref, 0, EXPERT_BLOCK)]
        x = jnp.concatenate([lo for lo, _ in halves] + [hi for _, hi in halves], axis=-1).astype(BF16)
        a = (jax.nn.silu(_dot(x, wg_scr[...])) * _dot(x, wu_scr[...])).astype(BF16)
        _store_rows(ys_ref, _pack_row(_dot(a, wd_scr[...])))

    @pl.when(jnp.logical_not(used))
    def _():
        ys_ref[...] = jnp.zeros(ys_ref.shape, ys_ref.dtype)


def _experts(layer, blk_e, n_used, first, next_e, xs, wg, wu, wd):
    blk = (EXPERT_BLOCK * ROW_SUB, LANES)
    nb = xs.shape[0] // blk[0]
    d, de = wg.shape[-2], wg.shape[-1]
    last_used = lambda i, nu: jnp.minimum(i, nu[0] - 1)
    hbm = pl.BlockSpec(memory_space=pl.ANY)
    grid_spec = pltpu.PrefetchScalarGridSpec(
        num_scalar_prefetch=4, grid=(nb,),
        in_specs=[pl.BlockSpec(blk, lambda i, be, nu, fi, ne: (last_used(i, nu), 0)), hbm, hbm, hbm],
        out_specs=pl.BlockSpec(blk, lambda i, be, nu, fi, ne: (i, 0)),
        scratch_shapes=[pltpu.VMEM((2, d, de), F32), pltpu.VMEM((2, d, de), F32), pltpu.VMEM((2, de, d), F32),
                        pltpu.VMEM((d, de), BF16), pltpu.VMEM((d, de), BF16), pltpu.VMEM((de, d), BF16),
                        pltpu.SMEM((1,), I32), pltpu.SemaphoreType.DMA((2,))])
    return pl.pallas_call(
        functools.partial(_experts_body, layer), grid_spec=grid_spec,
        out_shape=jax.ShapeDtypeStruct(xs.shape, xs.dtype),
        compiler_params=_params("arbitrary"), name="experts",
    )(blk_e, n_used, first, next_e, xs, wg, wu, wd)


def _combine_body(dest_ref, dest_next_ref, g8_ref, hs_ref, ys_hbm, o_ref, buf, sems):
    n_tok = hs_ref.shape[0]
    n_grp = n_tok // ISSUE_GROUP
    step = pl.program_id(0)
    cur = step % 2
    half_rows = TOP_K * n_tok

    def for_group(idx_ref, slot, op):
        base = slot * (half_rows * ROW_SUB)

        def row_copy(src_row, k, gi, g):
            src = pl.ds(pl.multiple_of(src_row * ROW_SUB, ROW_SUB), ROW_SUB)
            dst = base + (k * n_grp + gi) * (ISSUE_GROUP * ROW_SUB) + g * ROW_SUB
            return pltpu.make_async_copy(ys_hbm.at[src, :], buf.at[pl.ds(pl.multiple_of(dst, ROW_SUB), ROW_SUB), :],
                                         sems.at[slot])

        def body(gi, carry):
            p0 = gi * (ISSUE_GROUP * TOP_K)
            for g in range(ISSUE_GROUP):
                for k in range(TOP_K):
                    op(row_copy(idx_ref[p0 + g * TOP_K + k], k, gi, g), k)
            return carry
        lax.fori_loop(0, n_grp, body, 0)

    start = lambda cp, k: cp.start(priority=k % 2)

    @pl.when(step == 0)
    def _():
        for_group(dest_ref, 0, start)

    @pl.when(step + 1 < pl.num_programs(0))
    def _():
        for_group(dest_next_ref, 1 - cur, start)

    for_group(dest_ref, cur, lambda cp, k: cp.wait())

    half = ROW_SUB * LANES

    def reduce_group(r0):
        rows = pl.ds(r0, ISSUE_GROUP)
        gates = [g8_ref[rows, k:k + 1] for k in range(TOP_K)]
        picked = [_load_rows(buf, cur * half_rows + k * n_tok + r0, ISSUE_GROUP) for k in range(TOP_K)]
        for c in range(ROW_SUB):
            lo_cols = slice(c * LANES, (c + 1) * LANES)
            hi_cols = slice(half + c * LANES, half + (c + 1) * LANES)
            acc_lo = hs_ref[rows, lo_cols]
            acc_hi = hs_ref[rows, hi_cols]
            for k in range(TOP_K):
                lo, hi = _unpack_row(picked[k][c])
                acc_lo = acc_lo + gates[k] * lo
                acc_hi = acc_hi + gates[k] * hi
            o_ref[rows, lo_cols] = acc_lo
            o_ref[rows, hi_cols] = acc_hi

    def reduce(gi, carry):
        r0 = pl.multiple_of(gi * (2 * ISSUE_GROUP), 2 * ISSUE_GROUP)
        reduce_group(r0)
        reduce_group(r0 + ISSUE_GROUP)
        return carry

    lax.fori_loop(0, n_grp // 2, reduce, 0)


def _combine(dest_flat, g8, hs, ys):
    t, d = hs.shape
    tt = min(COMBINE_TILE, t)
    n_steps = t // tt
    return pl.pallas_call(
        _combine_body, grid=(n_steps,),
        in_specs=[pl.BlockSpec((tt * TOP_K,), lambda i: (i,), memory_space=pltpu.SMEM),
                  pl.BlockSpec((tt * TOP_K,), lambda i: (jnp.minimum(i + 1, n_steps - 1),), memory_space=pltpu.SMEM),
                  pl.BlockSpec((tt, g8.shape[1]), lambda i: (i, 0)),
                  pl.BlockSpec((tt, d), lambda i: (i, 0)),
                  pl.BlockSpec(memory_space=pl.ANY)],
        out_specs=pl.BlockSpec((tt, d), lambda i: (i, 0)),
        out_shape=jax.ShapeDtypeStruct((t, d), F32),
        scratch_shapes=[pltpu.VMEM((2 * TOP_K * tt * ROW_SUB, LANES), ys.dtype), pltpu.SemaphoreType.DMA((2,))],
        compiler_params=_params("arbitrary"), name="combine",
    )(dest_flat, dest_flat, g8, hs, ys)


def _ple_body(final, h2_ref, p_ref, win_ref, wgate_ref, g_ref, gf_ref, o_ref):
    h2 = h2_ref[...]
    e = _dot(p_ref[...].astype(BF16), win_ref[...])
    gate = jax.nn.sigmoid(_dot(h2.astype(BF16), wgate_ref[...]))
    h3 = h2 + _rms(gate * e, g_ref[...])
    o_ref[...] = _rms(h3, gf_ref[...]) if final else h3


def _ple(h2, p, w_in, w_gate, g, g_final, final):
    t, d = h2.shape
    tm = min(TOKEN_TILE, t)
    row = lambda n: pl.BlockSpec((tm, n), lambda i: (i, 0))
    return pl.pallas_call(
        functools.partial(_ple_body, final), grid=(t // tm,),
        in_specs=[row(d), row(p.shape[1]), _full(w_in.shape), _full(w_gate.shape), _full((1, d)), _full((1, d))],
        out_specs=row(d), out_shape=jax.ShapeDtypeStruct((t, d), F32),
        compiler_params=_params("parallel"), name="ple",
    )(h2, p, w_in, w_gate, g, g_final)


def _mixers(h, bsz, seq, norm1_g, w_in, four_w, ln_g, ln_b, ws, bs, conv_w, gate_b, norm_g, tables):
    d = h.shape[1]
    n_main = w_in.shape[1] - N_GATES
    w_main = w_in[:, :n_main].astype(BF16)
    w_g = w_in[:, n_main:]
    w_gate = jnp.pad(w_g, ((0, 0), (0, LANES - N_GATES))).astype(BF16)
    w_gate_t = w_g.T.astype(BF16)
    zf, zu, zv, zqk, zmv, zo, zg, zgt = _in_proj(h, norm1_g.reshape(1, d), w_main, w_gate, w_gate_t)
    b3 = lambda a: a.reshape(bsz, seq, a.shape[-1])

    cs, ss, cm, sm = tables
    yf = _fourier(b3(zf), cs, ss, cm, sm, _block_diag(four_w).astype(BF16))

    bs_full = jnp.repeat(bs.T, GMLP_HD, axis=1)
    yg = _gmlp(b3(zu), b3(zv), ln_g.reshape(1, -1), ln_b.reshape(1, -1), ws.astype(BF16), bs_full)

    zqk_pad = jnp.pad(b3(zqk), ((0, 0), (CONV_PAD, CONV_PAD), (0, 0)))
    gate_b_c = jnp.pad(gate_b, (0, LANES - N_GATES)).reshape(1, LANES)
    gate_b_t = jnp.broadcast_to(gate_b.reshape(N_GATES, 1), (N_GATES, LANES))
    ym = _mlstm(zqk_pad, b3(zmv), b3(zo), b3(zg), zgt, conv_w, gate_b_c, gate_b_t, norm_g.reshape(1, -1))
    t = bsz * seq
    return yf.reshape(t, -1), yg.reshape(t, -1), ym.reshape(t, -1)


def _moe(layer, h, yf, yg, ym, w_out, norm2_g, router_w, router_b, wg, wu, wd, sg, su, sd):
    t, d = h.shape
    sgu = jnp.concatenate([sg, su], axis=1).astype(BF16)
    tm = min(TOKEN_TILE, t)
    router_b_t = jnp.broadcast_to(router_b.reshape(-1, 1), (N_EXPERTS, tm))
    hs, xw, sel, gates, idx = _post_mix(h, yf, yg, ym, w_out.astype(BF16), norm2_g.reshape(1, d),
                                        router_w.T, router_b_t, sgu, sd.astype(BF16))
    rank, counts = _rank(sel)
    counts = counts[:, 0]
    pcounts = ((counts + EXPERT_BLOCK - 1) // EXPERT_BLOCK) * EXPERT_BLOCK
    pend = jnp.cumsum(pcounts)
    offs = (pend - pcounts).astype(I32)
    tail_start = jnp.where(pcounts > 0, pend - EXPERT_BLOCK, -1).astype(I32)
    n_blk = (t * TOP_K) // EXPERT_BLOCK + N_EXPERTS
    blk_start = jnp.arange(n_blk, dtype=I32) * EXPERT_BLOCK
    blk_e = jnp.minimum(jnp.sum(pend[None, :] <= blk_start[:, None], axis=1), N_EXPERTS - 1).astype(I32)
    n_used = (pend[-1:] // EXPERT_BLOCK).astype(I32)
    blk_i = jnp.arange(n_blk, dtype=I32)
    first = ((blk_i < n_used[0]) & ((blk_i == 0) | (blk_e != jnp.roll(blk_e, 1)))).astype(I32)
    after = pend[blk_e] // EXPERT_BLOCK
    next_e = jnp.where(after < n_used[0], blk_e[jnp.minimum(after, n_blk - 1)], -1).astype(I32)

    dest, g8 = _dest(rank, idx, gates, jnp.broadcast_to(offs.reshape(-1, 1), (N_EXPERTS, LANES)))
    dest_flat = dest.T.reshape(-1)
    g8 = jnp.pad(g8.T, ((0, 0), (0, LANES - TOP_K)))
    xs = _dispatch(dest_flat, tail_start, xw, n_blk * EXPERT_BLOCK)
    ys = _experts(layer, blk_e, n_used, first, next_e, xs, wg, wu, wd)
    return _combine(dest_flat, g8, hs, ys)


def kernel(x, p, norm1_g, w_in, four_w, gmlp_ln_g, gmlp_ln_b, gmlp_ws, gmlp_bs, mlstm_conv_w,
           mlstm_gate_b, mlstm_norm_g, w_out, norm2_g, router_w, router_b, exp_w_gate, exp_w_up,
           exp_w_down, sh_w_gate, sh_w_up, sh_w_down, ple_w_in, ple_w_gate, ple_norm_g, final_norm_g):
    bsz, seq, d = x.shape
    depth = w_in.shape[0]
    t = bsz * seq
    cos_s, sin_s = _dft_tables(seq)
    cos_m, sin_m = _dft_tables(FOUR_HD)
    eye = np.eye(FOUR_HEADS)
    tables = (jnp.asarray(cos_s, BF16), jnp.asarray(sin_s, BF16),
              jnp.asarray(np.kron(eye, cos_m), BF16), jnp.asarray(np.kron(eye, sin_m), BF16))
    h = x.reshape(t, d)
    for i in range(depth):
        yf, yg, ym = _mixers(h, bsz, seq, norm1_g[i], w_in[i], four_w[i], gmlp_ln_g[i], gmlp_ln_b[i],
                             gmlp_ws[i], gmlp_bs[i], mlstm_conv_w[i], mlstm_gate_b[i], mlstm_norm_g[i], tables)
        h2 = _moe(i, h, yf, yg, ym, w_out[i], norm2_g[i], router_w[i], router_b[i],
                  exp_w_gate, exp_w_up, exp_w_down, sh_w_gate[i], sh_w_up[i], sh_w_down[i])
        h = _ple(h2, p[i].reshape(t, -1), ple_w_in[i].astype(BF16), ple_w_gate[i].astype(BF16),
                 ple_norm_g[i].reshape(1, d), final_norm_g.reshape(1, d), i == depth - 1)
    return h.reshape(bsz, seq, d)
```

```python
import functools
import math

import numpy as np
import jax
import jax.numpy as jnp
from jax import lax
from jax.experimental import pallas as pl
from jax.experimental.pallas import tpu as pltpu

F32, BF16, I32, U32 = jnp.float32, jnp.bfloat16, jnp.int32, jnp.uint32
HIGHEST = lax.Precision.HIGHEST

EPS = 1e-6
LANES = 128
VMEM_LIMIT_BYTES = 48 * 1024 * 1024

FOUR_HEADS, FOUR_HD = 4, 64
GMLP_HEADS, GMLP_HD, GMLP_CHUNK = 4, 64, 128
MLSTM_HEADS, MLSTM_DV, MLSTM_DQK, MLSTM_CHUNK = 4, 128, 64, 128
CONV_PAD = 16
N_GATES = 16
N_EXPERTS, TOP_K = 128, 8
EXPERT_BLOCK = 512
TOKEN_TILE = 512
DISPATCH_TILE = 512
COMBINE_TILE = 512
ROUTE_COLS = 128
ISSUE_GROUP = 8


def _params(*sem):
    return pltpu.CompilerParams(dimension_semantics=sem, vmem_limit_bytes=VMEM_LIMIT_BYTES)


def _rms(x, g):
    return x * lax.rsqrt(jnp.mean(x * x, axis=-1, keepdims=True) + EPS) * g


def _full(shape):
    nd = len(shape)
    return pl.BlockSpec(shape, lambda *_: (0,) * nd)


def _dot(a, b):
    return jnp.dot(a, b, preferred_element_type=F32)


def _pack_row(x):
    n = x.shape[-1] // 2
    lo = lax.bitcast_convert_type(x[:, :n].astype(BF16).astype(F32), U32)
    hi = lax.bitcast_convert_type(x[:, n:].astype(BF16).astype(F32), U32)
    return hi | (lo >> 16)


def _unpack_row(w):
    lo = lax.bitcast_convert_type(w << 16, F32)
    hi = lax.bitcast_convert_type(w & jnp.uint32(0xFFFF0000), F32)
    return lo, hi


ROW_SUB = 4


def _store_rows(ref, w):
    m = w.shape[0]
    for c in range(ROW_SUB):
        ref[pl.ds(c, m, stride=ROW_SUB), :] = w[:, c * LANES:(c + 1) * LANES]


def _load_rows(ref, r0, m):
    return [ref[pl.ds(r0 * ROW_SUB + c, m, stride=ROW_SUB), :] for c in range(ROW_SUB)]


def _in_proj_body(h_ref, g_ref, w_ref, wg_ref, wgt_ref,
                  zf_ref, zu_ref, zv_ref, zqk_ref, zmv_ref, zo_ref, zg_ref, zgt_ref):
    ab = _rms(h_ref[...], g_ref[...]).astype(BF16)
    off = 0
    for o_ref in (zf_ref, zu_ref, zv_ref, zqk_ref, zmv_ref, zo_ref):
        n = o_ref.shape[-1]
        o_ref[...] = _dot(ab, w_ref[:, off:off + n]).astype(o_ref.dtype)
        off += n
    zg_ref[...] = _dot(ab, wg_ref[...])
    zgt_ref[...] = lax.dot_general(wgt_ref[...], ab, (((1,), (1,)), ((), ())),
                                   preferred_element_type=F32)


def _in_proj(h, g, w_main, w_gate, w_gate_t):
    t, d = h.shape
    tm = min(TOKEN_TILE, t)
    widths = (256, 256, 256, 512, 512, 512)
    row = lambda n: pl.BlockSpec((tm, n), lambda i: (i, 0))
    out_shape = [jax.ShapeDtypeStruct((t, n), BF16) for n in widths]
    out_shape += [jax.ShapeDtypeStruct((t, LANES), F32), jax.ShapeDtypeStruct((N_GATES, t), F32)]
    out_specs = [row(n) for n in widths]
    out_specs += [row(LANES), pl.BlockSpec((N_GATES, tm), lambda i: (0, i))]
    return pl.pallas_call(
        _in_proj_body, grid=(t // tm,),
        in_specs=[row(d), _full((1, d)), _full(w_main.shape), _full(w_gate.shape), _full(w_gate_t.shape)],
        out_specs=out_specs, out_shape=out_shape,
        compiler_params=_params("parallel"), name="in_proj",
    )(h, g, w_main, w_gate, w_gate_t)


def _dft_tables(n):
    k = np.arange(n, dtype=np.int64)
    ang = 2.0 * np.pi * ((k[:, None] * k[None, :]) % n).astype(np.float64) / n
    return np.cos(ang), np.sin(ang)


def _block_diag(blocks):
    h, a, b = blocks.shape
    eye = jnp.eye(h, dtype=blocks.dtype)
    return (eye[:, None, :, None] * blocks[:, :, None, :]).reshape(h * a, h * b)


def _fourier_body(scale, row_tile, z_ref, cs_ref, ss_ref, cm_ref, sm_ref, w_ref, o_ref, p_scr, q_scr):
    z = z_ref[0]
    p_scr[...] = _dot(z, cm_ref[...]).astype(BF16)
    q_scr[...] = _dot(z, sm_ref[...]).astype(BF16)
    s = z.shape[0]
    for r in range(s // row_tile):
        rs = slice(r * row_tile, (r + 1) * row_tile)
        re = (_dot(cs_ref[rs, :], p_scr[...]) - _dot(ss_ref[rs, :], q_scr[...])) * scale
        o_ref[0, rs, :] = _dot(re.astype(BF16), w_ref[...]).astype(o_ref.dtype)


def _fourier(zf, cs, ss, cm, sm, wbd):
    b, s, w = zf.shape
    row_tile = min(512, s)
    scale = 1.0 / math.sqrt(s * FOUR_HD)
    blk = pl.BlockSpec((1, s, w), lambda i: (i, 0, 0))
    return pl.pallas_call(
        functools.partial(_fourier_body, scale, row_tile), grid=(b,),
        in_specs=[blk, _full(cs.shape), _full(ss.shape), _full(cm.shape), _full(sm.shape), _full(wbd.shape)],
        out_specs=blk, out_shape=jax.ShapeDtypeStruct((b, s, w), BF16),
        scratch_shapes=[pltpu.VMEM((s, w), BF16), pltpu.VMEM((s, w), BF16)],
        compiler_params=_params("parallel"), name="fourier",
    )(zf, cs, ss, cm, sm, wbd)


def _gmlp_body(zu_ref, zv_ref, lg_ref, lb_ref, ws_ref, bs_ref, o_ref):
    s = zu_ref.shape[1]
    w = zu_ref.shape[2]
    lane = lax.broadcasted_iota(I32, (GMLP_CHUNK, w), 1)

    def chunk(c, carry):
        r0 = pl.multiple_of(c * GMLP_CHUNK, GMLP_CHUNK)
        rows = pl.ds(r0, GMLP_CHUNK)
        v = jax.nn.gelu(zv_ref[0, rows, :].astype(F32))
        vc = v - jnp.mean(v, axis=-1, keepdims=True)
        vn = vc * lax.rsqrt(jnp.mean(vc * vc, axis=-1, keepdims=True) + EPS) * lg_ref[...] + lb_ref[...]
        acc = bs_ref[...]
        for h in range(GMLP_HEADS):
            vh = jnp.where((lane >= h * GMLP_HD) & (lane < (h + 1) * GMLP_HD), vn, 0.0).astype(BF16)
            acc = acc + _dot(ws_ref[h], vh)
        u = jax.nn.gelu(zu_ref[0, rows, :].astype(F32))
        o_ref[0, rows, :] = (u * acc).astype(o_ref.dtype)
        return carry

    lax.fori_loop(0, s // GMLP_CHUNK, chunk, 0)


def _gmlp(zu, zv, ln_g, ln_b, ws, bs_full):
    b, s, w = zu.shape
    blk = pl.BlockSpec((1, s, w), lambda i: (i, 0, 0))
    return pl.pallas_call(
        _gmlp_body, grid=(b,),
        in_specs=[blk, blk, _full((1, w)), _full((1, w)), _full(ws.shape), _full(bs_full.shape)],
        out_specs=blk, out_shape=jax.ShapeDtypeStruct((b, s, w), BF16),
        compiler_params=_params("parallel"), name="gmlp",
    )(zu, zv, ln_g, ln_b, ws, bs_full)


def _mlstm_body(qkp_ref, v_ref, zo_ref, g_ref, gt_ref, cw_ref, gb_ref, gbt_ref, ng_ref, o_ref,
                qm_scr, km_scr, vt_scr, hf_scr, hb_scr, c_scr, n_scr, m_scr, gc_scr, gr_scr, cumc_scr, cumr_scr):
    L = MLSTM_CHUNK
    H = MLSTM_HEADS
    DV = MLSTM_DV
    s = v_ref.shape[1]
    nc = s // L
    qkw = H * MLSTM_DQK
    nt = (((1,), (1,)), ((), ()))

    ext = L + 2 * CONV_PAD
    r_i = lax.broadcasted_iota(I32, (L, ext), 0)
    c_i = lax.broadcasted_iota(I32, (L, ext), 1)
    sh_m1 = jnp.where(c_i == r_i + CONV_PAD - 1, 1.0, 0.0).astype(BF16)
    sh_0 = jnp.where(c_i == r_i + CONV_PAD, 1.0, 0.0).astype(BF16)
    sh_p1 = jnp.where(c_i == r_i + CONV_PAD + 1, 1.0, 0.0).astype(BF16)
    lane = lax.broadcasted_iota(I32, (L, LANES), 1)
    low_half = lane < MLSTM_DQK

    def conv_chunk(c, carry):
        r0 = pl.multiple_of(c * L, L)
        rows = pl.ds(r0, L)
        xe = qkp_ref[0, pl.ds(r0, ext), :]
        y = (cw_ref[0:1, :] * _dot(sh_m1, xe) + cw_ref[1:2, :] * _dot(sh_0, xe)
             + cw_ref[2:3, :] * _dot(sh_p1, xe))
        qk = y * jax.nn.sigmoid(y)
        for h in range(H):
            keep = low_half if h % 2 == 0 else jnp.logical_not(low_half)
            t0 = (h // 2) * LANES
            q_t = qk[:, t0:t0 + LANES] * (MLSTM_DQK ** -0.5)
            k_t = qk[:, qkw + t0:qkw + t0 + LANES]
            qm_scr[h, rows, :] = jnp.where(keep, q_t, 0.0).astype(BF16)
            km_scr[h, rows, :] = jnp.where(keep, k_t, 0.0).astype(BF16)
            vt_scr[h * DV:(h + 1) * DV, rows] = v_ref[0, rows, h * DV:(h + 1) * DV].astype(F32).T.astype(BF16)
        gc = g_ref[0, rows, :] + gb_ref[...]
        gr = gt_ref[:, rows] + gbt_ref[...]
        gc_scr[rows, :] = gc
        gr_scr[:, rows] = gr
        lf_c = jax.nn.log_sigmoid(gc)
        lf_r = jax.nn.log_sigmoid(gr)
        cumc_scr[0, rows, :] = jnp.dot(tri_l, lf_c, precision=HIGHEST, preferred_element_type=F32)
        cumc_scr[1, rows, :] = jnp.dot(tri_u, lf_c, precision=HIGHEST, preferred_element_type=F32)
        cumr_scr[0, :, rows] = jnp.dot(lf_r, tri_u, precision=HIGHEST, preferred_element_type=F32)
        cumr_scr[1, :, rows] = jnp.dot(lf_r, tri_l, precision=HIGHEST, preferred_element_type=F32)
        return carry

    ri = lax.broadcasted_iota(I32, (L, L), 0)
    ci = lax.broadcasted_iota(I32, (L, L), 1)
    tri_l = jnp.where(ci <= ri, 1.0, 0.0)
    tri_u = jnp.where(ci >= ri, 1.0, 0.0)
    lax.fori_loop(0, nc, conv_chunk, 0)

    c_scr[...] = jnp.zeros(c_scr.shape, F32)
    n_scr[...] = jnp.zeros(n_scr.shape, F32)
    m_scr[...] = jnp.zeros(m_scr.shape, F32)

    def rep8(row):
        return jnp.broadcast_to(row, (8, row.shape[-1])).astype(BF16)

    def direction(d, chunk):
        r0 = pl.multiple_of(chunk * L, L)
        rows = pl.ds(r0, L)
        gc = gc_scr[rows, :]
        gr = gr_scr[:, rows]
        cum_c = cumc_scr[d, rows, :]
        cum_r = cumr_scr[d, :, rows]
        mask = (ci >= ri) if d == 0 else (ci <= ri)
        i_lane = 2 * d * H
        f_lane = (2 * d + 1) * H
        for h in range(H):
            u = d * H + h
            b_r = cum_r[f_lane + h:f_lane + h + 1, :]
            li_r = gr[i_lane + h:i_lane + h + 1, :]
            key_c = gc[:, i_lane + h:i_lane + h + 1] - cum_c[:, f_lane + h:f_lane + h + 1]
            g_edge = b_r[:, L - 1:L] if d == 0 else b_r[:, 0:1]
            g_tot = jnp.broadcast_to(g_edge, (1, L))
            qh = qm_scr[h, rows, :]
            kh = km_scr[h, rows, :]
            vt = vt_scr[h * DV:(h + 1) * DV, rows]
            ct_prev = c_scr[u]
            n_prev = n_scr[u]
            m_prev = m_scr[u]
            dmat = jnp.where(mask, b_r + key_c, -jnp.inf)
            m_intra = jnp.max(dmat, axis=0, keepdims=True)
            m_inter = b_r + m_prev
            m_tot = jnp.maximum(m_intra, m_inter)
            kq = lax.dot_general(kh, qh, nt, preferred_element_type=F32)
            s_mat = jnp.exp(dmat - m_tot) * kq
            inter = jnp.exp(m_inter - m_tot)
            qn = lax.dot_general(rep8(n_prev), qh, nt, preferred_element_type=F32)[0:1, :]
            den = jnp.sum(s_mat, axis=0, keepdims=True) + inter * qn
            num = (_dot(vt, s_mat.astype(BF16))
                   + inter * lax.dot_general(ct_prev.astype(BF16), qh, nt, preferred_element_type=F32))
            dst = hf_scr if d == 0 else hb_scr
            dst[h * DV:(h + 1) * DV, rows] = num / jnp.maximum(jnp.abs(den), jnp.exp(-m_tot))
            a_r = g_tot - b_r + li_r
            m_loc = jnp.broadcast_to(jnp.max(a_r, axis=-1, keepdims=True), (1, L))
            w_r = jnp.exp(a_r - m_loc)
            ct_loc = _dot((vt.astype(F32) * w_r).astype(BF16), kh)
            n_loc = _dot(rep8(w_r), kh)[0:1, :]
            m_new = jnp.maximum(g_tot + m_prev, m_loc)
            a_old = jnp.exp(g_tot + m_prev - m_new)
            a_new = jnp.exp(m_loc - m_new)
            c_scr[u] = a_old * ct_prev + a_new * ct_loc
            n_scr[u] = a_old * n_prev + a_new * n_loc
            m_scr[u] = m_new

    def step(i, carry):
        direction(0, i)
        direction(1, nc - 1 - i)
        return carry

    lax.fori_loop(0, nc, step, 0)

    def finish(c, carry):
        r0 = pl.multiple_of(c * L, L)
        rows = pl.ds(r0, L)
        og = jax.nn.sigmoid(zo_ref[0, rows, :].astype(F32))
        for h in range(H):
            cols = slice(h * DV, (h + 1) * DV)
            x = og[:, cols] * (hf_scr[cols, rows] + hb_scr[cols, rows]).T
            xc = x - jnp.mean(x, axis=-1, keepdims=True)
            y = xc * lax.rsqrt(jnp.mean(xc * xc, axis=-1, keepdims=True) + EPS) * ng_ref[:, cols]
            o_ref[0, rows, cols] = y.astype(o_ref.dtype)
        return carry

    lax.fori_loop(0, nc, finish, 0)


def _mlstm(zqk_pad, zmv, zo, zg, zgt, conv_w, gate_b, gate_b_t, norm_g):
    b, s, w = zmv.shape
    H = MLSTM_HEADS
    blk = pl.BlockSpec((1, s, w), lambda i: (i, 0, 0))
    units = 2 * H
    return pl.pallas_call(
        _mlstm_body, grid=(b,),
        in_specs=[pl.BlockSpec((1, s + 2 * CONV_PAD, w), lambda i: (i, 0, 0)), blk, blk,
                  pl.BlockSpec((1, s, LANES), lambda i: (i, 0, 0)),
                  pl.BlockSpec((N_GATES, s), lambda i: (0, i)),
                  _full(conv_w.shape), _full(gate_b.shape), _full(gate_b_t.shape), _full(norm_g.shape)],
        out_specs=blk, out_shape=jax.ShapeDtypeStruct((b, s, w), BF16),
        scratch_shapes=[pltpu.VMEM((H, s, LANES), BF16), pltpu.VMEM((H, s, LANES), BF16),
                        pltpu.VMEM((w, s), BF16), pltpu.VMEM((w, s), F32), pltpu.VMEM((w, s), F32),
                        pltpu.VMEM((units, MLSTM_DV, LANES), F32), pltpu.VMEM((units, 1, LANES), F32),
                        pltpu.VMEM((units, 1, LANES), F32),
                        pltpu.VMEM((s, LANES), F32), pltpu.VMEM((N_GATES, s), F32),
                        pltpu.VMEM((2, s, LANES), F32), pltpu.VMEM((2, N_GATES, s), F32)],
        compiler_params=_params("parallel"), name="mlstm",
    )(zqk_pad, zmv, zo, zg, zgt, conv_w, gate_b, gate_b_t, norm_g)


def _post_mix_body(h_ref, yf_ref, yg_ref, ym_ref, wo_ref, g2_ref, rwh_ref, rwl_ref, rbt_ref, sgu_ref, sd_ref,
                   hs_ref, xw_ref, sel_ref, gate_ref, idx_ref, sc_scr):
    wf, wg = yf_ref.shape[-1], yg_ref.shape[-1]
    h1 = (h_ref[...] + _dot(yf_ref[...], wo_ref[0:wf, :]) + _dot(yg_ref[...], wo_ref[wf:wf + wg, :])
          + _dot(ym_ref[...], wo_ref[wf + wg:, :]))
    xn = _rms(h1, g2_ref[...])
    xb = xn.astype(BF16)
    _store_rows(xw_ref, _pack_row(xn))
    gu = _dot(xb, sgu_ref[...])
    de = gu.shape[-1] // 2
    act = (jax.nn.silu(gu[:, :de]) * gu[:, de:]).astype(BF16)
    hs_ref[...] = h1 + _dot(act, sd_ref[...])
    nt = (((1,), (1,)), ((), ()))
    x_lo = (xn - xb.astype(F32)).astype(BF16)
    logits = (lax.dot_general(rwh_ref[...], xb, nt, preferred_element_type=F32)
              + lax.dot_general(rwh_ref[...], x_lo, nt, preferred_element_type=F32)
              + lax.dot_general(rwl_ref[...], xb, nt, preferred_element_type=F32))
    sc_scr[...] = jax.nn.sigmoid(logits)
    for c0 in range(0, sc_scr.shape[1], ROUTE_COLS):
        cols = slice(c0, c0 + ROUTE_COLS)
        scores = sc_scr[:, cols]
        work = scores + rbt_ref[:, cols]
        expert = lax.broadcasted_iota(I32, scores.shape, 0)
        choice = lax.broadcasted_iota(I32, (TOP_K, ROUTE_COLS), 0)
        sel = jnp.zeros(scores.shape, F32)
        idx = jnp.zeros(choice.shape, I32)
        for k in range(TOP_K):
            m = jnp.max(work, axis=0, keepdims=True)
            e = jnp.min(jnp.where(work == m, expert, N_EXPERTS), axis=0, keepdims=True)
            hit = expert == e
            sel = jnp.where(hit, 1.0, sel)
            idx = jnp.where(choice == k, e, idx)
            work = jnp.where(hit, -jnp.inf, work)
        picked = sel * scores
        sel_ref[:, cols] = sel
        gate_ref[:, cols] = picked / jnp.sum(picked, axis=0, keepdims=True)
        idx_ref[:, cols] = idx


def _post_mix(h, yf, yg, ym, w_out, g2, router_w_t, router_b_t, sgu, sd):
    t, d = h.shape
    tm = min(TOKEN_TILE, t)
    row = lambda n: pl.BlockSpec((tm, n), lambda i: (i, 0))
    col = lambda n: pl.BlockSpec((n, tm), lambda i: (0, i))
    rw_hi = router_w_t.astype(BF16)
    rw_lo = (router_w_t - rw_hi.astype(F32)).astype(BF16)
    return pl.pallas_call(
        _post_mix_body, grid=(t // tm,),
        in_specs=[row(d), row(yf.shape[1]), row(yg.shape[1]), row(ym.shape[1]), _full(w_out.shape),
                  _full((1, d)), _full(rw_hi.shape), _full(rw_lo.shape), _full(router_b_t.shape),
                  _full(sgu.shape), _full(sd.shape)],
        out_specs=[row(d), pl.BlockSpec((tm * ROW_SUB, LANES), lambda i: (i, 0)),
                   col(N_EXPERTS), col(N_EXPERTS), col(TOP_K)],
        out_shape=[jax.ShapeDtypeStruct((t, d), F32), jax.ShapeDtypeStruct((t * ROW_SUB, LANES), U32),
                   jax.ShapeDtypeStruct((N_EXPERTS, t), F32), jax.ShapeDtypeStruct((N_EXPERTS, t), F32),
                   jax.ShapeDtypeStruct((TOP_K, t), I32)],
        scratch_shapes=[pltpu.VMEM((N_EXPERTS, tm), F32)],
        compiler_params=_params("parallel"), name="post_mix",
    )(h, yf, yg, ym, w_out, g2, rw_hi, rw_lo, router_b_t, sgu, sd)


def _rank_body(sel_ref, rank_ref, cnt_ref, carry):
    @pl.when(pl.program_id(0) == 0)
    def _():
        carry[...] = jnp.zeros(carry.shape, F32)

    m = sel_ref[...].astype(BF16)
    tm = m.shape[1]
    ri = lax.broadcasted_iota(I32, (tm, tm), 0)
    ci = lax.broadcasted_iota(I32, (tm, tm), 1)
    earlier = jnp.where(ri < ci, 1.0, 0.0).astype(BF16)
    seen = carry[...]
    rank = _dot(m, earlier) + jnp.concatenate([seen] * (tm // LANES), axis=1)
    rank_ref[...] = rank.astype(I32)
    seen = seen + _dot(m, jnp.ones((tm, LANES), BF16))
    carry[...] = seen
    cnt_ref[...] = seen.astype(I32)


def _rank(sel_t):
    e, t = sel_t.shape
    tm = min(TOKEN_TILE, t)
    return pl.pallas_call(
        _rank_body, grid=(t // tm,),
        in_specs=[pl.BlockSpec((e, tm), lambda i: (0, i))],
        out_specs=[pl.BlockSpec((e, tm), lambda i: (0, i)), _full((e, LANES))],
        out_shape=[jax.ShapeDtypeStruct((e, t), I32), jax.ShapeDtypeStruct((e, LANES), I32)],
        scratch_shapes=[pltpu.VMEM((e, LANES), F32)],
        compiler_params=_params("arbitrary"), name="rank",
    )(sel_t)


def _dest_body(rank_ref, idx_ref, gate_ref, offs_ref, dest_ref, g8_ref):
    tm = rank_ref.shape[1]
    offs = jnp.concatenate([offs_ref[...]] * (tm // LANES), axis=1)
    pos = (rank_ref[...] + offs).astype(F32)
    gates = gate_ref[...]
    expert = lax.broadcasted_iota(I32, pos.shape, 0)
    choice = lax.broadcasted_iota(I32, (TOP_K, tm), 0)
    dest = jnp.zeros(choice.shape, F32)
    g8 = jnp.zeros(choice.shape, F32)
    for k in range(TOP_K):
        hit = expert == idx_ref[k:k + 1, :]
        dk = jnp.sum(jnp.where(hit, pos, 0.0), axis=0, keepdims=True)
        gk = jnp.sum(jnp.where(hit, gates, 0.0), axis=0, keepdims=True)
        dest = jnp.where(choice == k, dk, dest)
        g8 = jnp.where(choice == k, gk, g8)
    dest_ref[...] = dest.astype(I32)
    g8_ref[...] = g8


def _dest(rank_t, idx_t, gates_t, offs_rep):
    e, t = rank_t.shape
    tm = min(TOKEN_TILE, t)
    col = lambda n: pl.BlockSpec((n, tm), lambda i: (0, i))
    return pl.pallas_call(
        _dest_body, grid=(t // tm,),
        in_specs=[col(e), col(TOP_K), col(e), _full((e, LANES))],
        out_specs=[col(TOP_K), col(TOP_K)],
        out_shape=[jax.ShapeDtypeStruct((TOP_K, t), I32), jax.ShapeDtypeStruct((TOP_K, t), F32)],
        compiler_params=_params("parallel"), name="dest",
    )(rank_t, idx_t, gates_t, offs_rep)


def _dispatch_body(dest_ref, tail_ref, x_ref, xs_hbm, zero_scr, sem):
    n_grp = x_ref.shape[0]

    @pl.when(pl.program_id(0) == 0)
    def _():
        zero_scr[...] = jnp.zeros(zero_scr.shape, zero_scr.dtype)

        def tail_copy(e):
            start = pl.multiple_of(jnp.maximum(tail_ref[e], 0) * ROW_SUB, EXPERT_BLOCK * ROW_SUB)
            return pltpu.make_async_copy(zero_scr, xs_hbm.at[pl.ds(start, EXPERT_BLOCK * ROW_SUB), :], sem)

        def clear(e, carry):
            @pl.when(tail_ref[e] >= 0)
            def _():
                tail_copy(e).start()
            return carry

        def clear_done(e, carry):
            @pl.when(tail_ref[e] >= 0)
            def _():
                tail_copy(e).wait()
            return carry

        lax.fori_loop(0, N_EXPERTS, clear, 0)
        lax.fori_loop(0, N_EXPERTS, clear_done, 0)

    def row_copy(group, g, dst_row):
        dst = pl.ds(pl.multiple_of(dst_row * ROW_SUB, ROW_SUB), ROW_SUB)
        return pltpu.make_async_copy(x_ref.at[group, g], xs_hbm.at[dst, :], sem)

    def for_group(op):
        def body(gi, carry):
            p0 = gi * (ISSUE_GROUP * TOP_K)
            for g in range(ISSUE_GROUP):
                for k in range(TOP_K):
                    op(row_copy(gi, g, dest_ref[p0 + g * TOP_K + k]), k)
            return carry
        lax.fori_loop(0, n_grp, body, 0)

    for_group(lambda cp, k: cp.start(priority=k % 2))
    for_group(lambda cp, k: cp.wait())


def _dispatch(dest_flat, tail_start, xw, n_rows):
    t = xw.shape[0] // ROW_SUB
    tt = min(DISPATCH_TILE, t)
    x4 = xw.reshape(t // ISSUE_GROUP, ISSUE_GROUP, ROW_SUB, LANES)
    return pl.pallas_call(
        _dispatch_body, grid=(t // tt,),
        in_specs=[pl.BlockSpec((tt * TOP_K,), lambda i: (i,), memory_space=pltpu.SMEM),
                  pl.BlockSpec(memory_space=pltpu.SMEM),
                  pl.BlockSpec((tt // ISSUE_GROUP, ISSUE_GROUP, ROW_SUB, LANES), lambda i: (i, 0, 0, 0))],
        out_specs=pl.BlockSpec(memory_space=pl.ANY),
        out_shape=jax.ShapeDtypeStruct((n_rows * ROW_SUB, LANES), xw.dtype),
        scratch_shapes=[pltpu.VMEM((EXPERT_BLOCK * ROW_SUB, LANES), xw.dtype), pltpu.SemaphoreType.DMA(())],
        compiler_params=_params("arbitrary"), name="dispatch",
    )(dest_flat, tail_start, x4)


def _experts_body(layer, blk_e_ref, n_used_ref, first_ref, next_e_ref, xs_ref, wg_hbm, wu_hbm, wd_hbm, ys_ref,
                  wg_buf, wu_buf, wd_buf, wg_scr, wu_scr, wd_scr, slot_ref, sems):
    i = pl.program_id(0)
    used = i < n_used_ref[0]

    def weight_copies(e, slot):
        return [pltpu.make_async_copy(hbm.at[layer, e], buf.at[slot], sems.at[slot])
                for hbm, buf in ((wg_hbm, wg_buf), (wu_hbm, wu_buf), (wd_hbm, wd_buf))]

    @pl.when(i == 0)
    def _():
        slot_ref[0] = 0
        for cp in weight_copies(blk_e_ref[0], 0):
            cp.start()

    @pl.when(first_ref[i] == 1)
    def _():
        slot = slot_ref[0]
        for cp in weight_copies(blk_e_ref[i], slot):
            cp.wait()
        wg_scr[...] = wg_buf[slot].astype(BF16)
        wu_scr[...] = wu_buf[slot].astype(BF16)
        wd_scr[...] = wd_buf[slot].astype(BF16)

        @pl.when(next_e_ref[i] >= 0)
        def _():
            for cp in weight_copies(next_e_ref[i], 1 - slot):
                cp.start(priority=1)

        slot_ref[0] = 1 - slot

    @pl.when(used)
    def _():
        halves = [_unpack_row(w) for w in _load_rows(xs_ref, 0, EXPERT_BLOCK)]
        x = jnp.concatenate([lo for lo, _ in halves] + [hi for _, hi in halves], axis=-1).astype(BF16)
        a = (jax.nn.silu(_dot(x, wg_scr[...])) * _dot(x, wu_scr[...])).astype(BF16)
        _store_rows(ys_ref, _pack_row(_dot(a, wd_scr[...])))

    @pl.when(jnp.logical_not(used))
    def _():
        ys_ref[...] = jnp.zeros(ys_ref.shape, ys_ref.dtype)


def _experts(layer, blk_e, n_used, first, next_e, xs, wg, wu, wd):
    blk = (EXPERT_BLOCK * ROW_SUB, LANES)
    nb = xs.shape[0] // blk[0]
    d, de = wg.shape[-2], wg.shape[-1]
    last_used = lambda i, nu: jnp.minimum(i, nu[0] - 1)
    hbm = pl.BlockSpec(memory_space=pl.ANY)
    grid_spec = pltpu.PrefetchScalarGridSpec(
        num_scalar_prefetch=4, grid=(nb,),
        in_specs=[pl.BlockSpec(blk, lambda i, be, nu, fi, ne: (last_used(i, nu), 0)), hbm, hbm, hbm],
        out_specs=pl.BlockSpec(blk, lambda i, be, nu, fi, ne: (i, 0)),
        scratch_shapes=[pltpu.VMEM((2, d, de), F32), pltpu.VMEM((2, d, de), F32), pltpu.VMEM((2, de, d), F32),
                        pltpu.VMEM((d, de), BF16), pltpu.VMEM((d, de), BF16), pltpu.VMEM((de, d), BF16),
                        pltpu.SMEM((1,), I32), pltpu.SemaphoreType.DMA((2,))])
    return pl.pallas_call(
        functools.partial(_experts_body, layer), grid_spec=grid_spec,
        out_shape=jax.ShapeDtypeStruct(xs.shape, xs.dtype),
        compiler_params=_params("arbitrary"), name="experts",
    )(blk_e, n_used, first, next_e, xs, wg, wu, wd)


def _combine_body(dest_ref, dest_next_ref, g8_ref, hs_ref, ys_hbm, o_ref, buf, sems):
    n_tok = hs_ref.shape[0]
    n_grp = n_tok // ISSUE_GROUP
    step = pl.program_id(0)
    cur = step % 2
    half_rows = TOP_K * n_tok

    def for_group(idx_ref, slot, op):
        base = slot * (half_rows * ROW_SUB)

        def row_copy(src_row, k, gi, g):
            src = pl.ds(pl.multiple_of(src_row * ROW_SUB, ROW_SUB), ROW_SUB)
            dst = base + (k * n_grp + gi) * (ISSUE_GROUP * ROW_SUB) + g * ROW_SUB
            return pltpu.make_async_copy(ys_hbm.at[src, :], buf.at[pl.ds(pl.multiple_of(dst, ROW_SUB), ROW_SUB), :],
                                         sems.at[slot])

        def body(gi, carry):
            p0 = gi * (ISSUE_GROUP * TOP_K)
            for g in range(ISSUE_GROUP):
                for k in range(TOP_K):
                    op(row_copy(idx_ref[p0 + g * TOP_K + k], k, gi, g), k)
            return carry
        lax.fori_loop(0, n_grp, body, 0)

    start = lambda cp, k: cp.start(priority=k % 2)

    @pl.when(step == 0)
    def _():
        for_group(dest_ref, 0, start)

    @pl.when(step + 1 < pl.num_programs(0))
    def _():
        for_group(dest_next_ref, 1 - cur, start)

    for_group(dest_ref, cur, lambda cp, k: cp.wait())

    half = ROW_SUB * LANES

    def reduce_group(r0):
        rows = pl.ds(r0, ISSUE_GROUP)
        gates = [g8_ref[rows, k:k + 1] for k in range(TOP_K)]
        picked = [_load_rows(buf, cur * half_rows + k * n_tok + r0, ISSUE_GROUP) for k in range(TOP_K)]
        for c in range(ROW_SUB):
            lo_cols = slice(c * LANES, (c + 1) * LANES)
            hi_cols = slice(half + c * LANES, half + (c + 1) * LANES)
            acc_lo = hs_ref[rows, lo_cols]
            acc_hi = hs_ref[rows, hi_cols]
            for k in range(TOP_K):
                lo, hi = _unpack_row(picked[k][c])
                acc_lo = acc_lo + gates[k] * lo
                acc_hi = acc_hi + gates[k] * hi
            o_ref[rows, lo_cols] = acc_lo
            o_ref[rows, hi_cols] = acc_hi

    def reduce(gi, carry):
        r0 = pl.multiple_of(gi * (2 * ISSUE_GROUP), 2 * ISSUE_GROUP)
        reduce_group(r0)
        reduce_group(r0 + ISSUE_GROUP)
        return carry

    lax.fori_loop(0, n_grp // 2, reduce, 0)


def _combine(dest_flat, g8, hs, ys):
    t, d = hs.shape
    tt = min(COMBINE_TILE, t)
    n_steps = t // tt
    return pl.pallas_call(
        _combine_body, grid=(n_steps,),
        in_specs=[pl.BlockSpec((tt * TOP_K,), lambda i: (i,), memory_space=pltpu.SMEM),
                  pl.BlockSpec((tt * TOP_K,), lambda i: (jnp.minimum(i + 1, n_steps - 1),), memory_space=pltpu.SMEM),
                  pl.BlockSpec((tt, g8.shape[1]), lambda i: (i, 0)),
                  pl.BlockSpec((tt, d), lambda i: (i, 0)),
                  pl.BlockSpec(memory_space=pl.ANY)],
        out_specs=pl.BlockSpec((tt, d), lambda i: (i, 0)),
        out_shape=jax.ShapeDtypeStruct((t, d), F32),
        scratch_shapes=[pltpu.VMEM((2 * TOP_K * tt * ROW_SUB, LANES), ys.dtype), pltpu.SemaphoreType.DMA((2,))],
        compiler_params=_params("arbitrary"), name="combine",
    )(dest_flat, dest_flat, g8, hs, ys)


def _ple_body(final, h2_ref, p_ref, win_ref, wgate_ref, g_ref, gf_ref, o_ref):
    h2 = h2_ref[...]
    e = _dot(p_ref[...].astype(BF16), win_ref[...])
    gate = jax.nn.sigmoid(_dot(h2.astype(BF16), wgate_ref[...]))
    h3 = h2 + _rms(gate * e, g_ref[...])
    o_ref[...] = _rms(h3, gf_ref[...]) if final else h3


def _ple(h2, p, w_in, w_gate, g, g_final, final):
    t, d = h2.shape
    tm = min(TOKEN_TILE, t)
    row = lambda n: pl.BlockSpec((tm, n), lambda i: (i, 0))
    return pl.pallas_call(
        functools.partial(_ple_body, final), grid=(t // tm,),
        in_specs=[row(d), row(p.shape[1]), _full(w_in.shape), _full(w_gate.shape), _full((1, d)), _full((1, d))],
        out_specs=row(d), out_shape=jax.ShapeDtypeStruct((t, d), F32),
        compiler_params=_params("parallel"), name="ple",
    )(h2, p, w_in, w_gate, g, g_final)


def _mixers(h, bsz, seq, norm1_g, w_in, four_w, ln_g, ln_b, ws, bs, conv_w, gate_b, norm_g, tables):
    d = h.shape[1]
    n_main = w_in.shape[1] - N_GATES
    w_main = w_in[:, :n_main].astype(BF16)
    w_g = w_in[:, n_main:]
    w_gate = jnp.pad(w_g, ((0, 0), (0, LANES - N_GATES))).astype(BF16)
    w_gate_t = w_g.T.astype(BF16)
    zf, zu, zv, zqk, zmv, zo, zg, zgt = _in_proj(h, norm1_g.reshape(1, d), w_main, w_gate, w_gate_t)
    b3 = lambda a: a.reshape(bsz, seq, a.shape[-1])

    cs, ss, cm, sm = tables
    yf = _fourier(b3(zf), cs, ss, cm, sm, _block_diag(four_w).astype(BF16))

    bs_full = jnp.repeat(bs.T, GMLP_HD, axis=1)
    yg = _gmlp(b3(zu), b3(zv), ln_g.reshape(1, -1), ln_b.reshape(1, -1), ws.astype(BF16), bs_full)

    zqk_pad = jnp.pad(b3(zqk), ((0, 0), (CONV_PAD, CONV_PAD), (0, 0)))
    gate_b_c = jnp.pad(gate_b, (0, LANES - N_GATES)).reshape(1, LANES)
    gate_b_t = jnp.broadcast_to(gate_b.reshape(N_GATES, 1), (N_GATES, LANES))
    ym = _mlstm(zqk_pad, b3(zmv), b3(zo), b3(zg), zgt, conv_w, gate_b_c, gate_b_t, norm_g.reshape(1, -1))
    t = bsz * seq
    return yf.reshape(t, -1), yg.reshape(t, -1), ym.reshape(t, -1)


def _moe(layer, h, yf, yg, ym, w_out, norm2_g, router_w, router_b, wg, wu, wd, sg, su, sd):
    t, d = h.shape
    sgu = jnp.concatenate([sg, su], axis=1).astype(BF16)
    tm = min(TOKEN_TILE, t)
    router_b_t = jnp.broadcast_to(router_b.reshape(-1, 1), (N_EXPERTS, tm))
    hs, xw, sel, gates, idx = _post_mix(h, yf, yg, ym, w_out.astype(BF16), norm2_g.reshape(1, d),
                                        router_w.T, router_b_t, sgu, sd.astype(BF16))
    rank, counts = _rank(sel)
    counts = counts[:, 0]
    pcounts = ((counts + EXPERT_BLOCK - 1) // EXPERT_BLOCK) * EXPERT_BLOCK
    pend = jnp.cumsum(pcounts)
    offs = (pend - pcounts).astype(I32)
    tail_start = jnp.where(pcounts > 0, pend - EXPERT_BLOCK, -1).astype(I32)
    n_blk = (t * TOP_K) // EXPERT_BLOCK + N_EXPERTS
    blk_start = jnp.arange(n_blk, dtype=I32) * EXPERT_BLOCK
    blk_e = jnp.minimum(jnp.sum(pend[None, :] <= blk_start[:, None], axis=1), N_EXPERTS - 1).astype(I32)
    n_used = (pend[-1:] // EXPERT_BLOCK).astype(I32)
    blk_i = jnp.arange(n_blk, dtype=I32)
    first = ((blk_i < n_used[0]) & ((blk_i == 0) | (blk_e != jnp.roll(blk_e, 1)))).astype(I32)
    after = pend[blk_e] // EXPERT_BLOCK
    next_e = jnp.where(after < n_used[0], blk_e[jnp.minimum(after, n_blk - 1)], -1).astype(I32)

    dest, g8 = _dest(rank, idx, gates, jnp.broadcast_to(offs.reshape(-1, 1), (N_EXPERTS, LANES)))
    dest_flat = dest.T.reshape(-1)
    g8 = jnp.pad(g8.T, ((0, 0), (0, LANES - TOP_K)))
    xs = _dispatch(dest_flat, tail_start, xw, n_blk * EXPERT_BLOCK)
    ys = _experts(layer, blk_e, n_used, first, next_e, xs, wg, wu, wd)
    return _combine(dest_flat, g8, hs, ys)


def kernel(x, p, norm1_g, w_in, four_w, gmlp_ln_g, gmlp_ln_b, gmlp_ws, gmlp_bs, mlstm_conv_w,
           mlstm_gate_b, mlstm_norm_g, w_out, norm2_g, router_w, router_b, exp_w_gate, exp_w_up,
           exp_w_down, sh_w_gate, sh_w_up, sh_w_down, ple_w_in, ple_w_gate, ple_norm_g, final_norm_g):
    bsz, seq, d = x.shape
    depth = w_in.shape[0]
    t = bsz * seq
    cos_s, sin_s = _dft_tables(seq)
    cos_m, sin_m = _dft_tables(FOUR_HD)
    eye = np.eye(FOUR_HEADS)
    tables = (jnp.asarray(cos_s, BF16), jnp.asarray(sin_s, BF16),
              jnp.asarray(np.kron(eye, cos_m), BF16), jnp.asarray(np.kron(eye, sin_m), BF16))
    h = x.reshape(t, d)
    for i in range(depth):
        yf, yg, ym = _mixers(h, bsz, seq, norm1_g[i], w_in[i], four_w[i], gmlp_ln_g[i], gmlp_ln_b[i],
                             gmlp_ws[i], gmlp_bs[i], mlstm_conv_w[i], mlstm_gate_b[i], mlstm_norm_g[i], tables)
        h2 = _moe(i, h, yf, yg, ym, w_out[i], norm2_g[i], router_w[i], router_b[i],
                  exp_w_gate, exp_w_up, exp_w_down, sh_w_gate[i], sh_w_up[i], sh_w_down[i])
        h = _ple(h2, p[i].reshape(t, -1), ple_w_in[i].astype(BF16), ple_w_gate[i].astype(BF16),
                 ple_norm_g[i].reshape(1, d), final_norm_g.reshape(1, d), i == depth - 1)
    return h.reshape(bsz, seq, d)
```

```python
import functools
import math

import numpy as np
import jax
import jax.numpy as jnp
from jax import lax
from jax.experimental import pallas as pl
from jax.experimental.pallas import tpu as pltpu

F32, BF16, I32, U32 = jnp.float32, jnp.bfloat16, jnp.int32, jnp.uint32
HIGHEST = lax.Precision.HIGHEST

EPS = 1e-6
LANES = 128
VMEM_LIMIT_BYTES = 48 * 1024 * 1024

FOUR_HEADS, FOUR_HD = 4, 64
GMLP_HEADS, GMLP_HD, GMLP_CHUNK = 4, 64, 128
MLSTM_HEADS, MLSTM_DV, MLSTM_DQK, MLSTM_CHUNK = 4, 128, 64, 128
CONV_PAD = 16
N_GATES = 16
N_EXPERTS, TOP_K = 128, 8
EXPERT_BLOCK = 512
TOKEN_TILE = 512
DISPATCH_TILE = 512
COMBINE_TILE = 512
ROUTE_COLS = 128
ISSUE_GROUP = 8


def _params(*sem):
    return pltpu.CompilerParams(dimension_semantics=sem, vmem_limit_bytes=VMEM_LIMIT_BYTES)


def _rms(x, g):
    return x * lax.rsqrt(jnp.mean(x * x, axis=-1, keepdims=True) + EPS) * g


def _full(shape):
    nd = len(shape)
    return pl.BlockSpec(shape, lambda *_: (0,) * nd)


def _dot(a, b):
    return jnp.dot(a, b, preferred_element_type=F32)


def _pack_row(x):
    n = x.shape[-1] // 2
    lo = lax.bitcast_convert_type(x[:, :n].astype(BF16).astype(F32), U32)
    hi = lax.bitcast_convert_type(x[:, n:].astype(BF16).astype(F32), U32)
    return hi | (lo >> 16)


def _unpack_row(w):
    lo = lax.bitcast_convert_type(w << 16, F32)
    hi = lax.bitcast_convert_type(w & jnp.uint32(0xFFFF0000), F32)
    return lo, hi


ROW_SUB = 4


def _store_rows(ref, w):
    m = w.shape[0]
    for c in range(ROW_SUB):
        ref[pl.ds(c, m, stride=ROW_SUB), :] = w[:, c * LANES:(c + 1) * LANES]


def _load_rows(ref, r0, m):
    return [ref[pl.ds(r0 * ROW_SUB + c, m, stride=ROW_SUB), :] for c in range(ROW_SUB)]


def _in_proj_body(h_ref, g_ref, w_ref, wg_ref, wgt_ref,
                  zf_ref, zu_ref, zv_ref, zqk_ref, zmv_ref, zo_ref, zg_ref, zgt_ref):
    ab = _rms(h_ref[...], g_ref[...]).astype(BF16)
    off = 0
    for o_ref in (zf_ref, zu_ref, zv_ref, zqk_ref, zmv_ref, zo_ref):
        n = o_ref.shape[-1]
        o_ref[...] = _dot(ab, w_ref[:, off:off + n]).astype(o_ref.dtype)
        off += n
    zg_ref[...] = _dot(ab, wg_ref[...])
    zgt_ref[...] = lax.dot_general(wgt_ref[...], ab, (((1,), (1,)), ((), ())),
                                   preferred_element_type=F32)


def _in_proj(h, g, w_main, w_gate, w_gate_t):
    t, d = h.shape
    tm = min(TOKEN_TILE, t)
    widths = (256, 256, 256, 512, 512, 512)
    row = lambda n: pl.BlockSpec((tm, n), lambda i: (i, 0))
    out_shape = [jax.ShapeDtypeStruct((t, n), BF16) for n in widths]
    out_shape += [jax.ShapeDtypeStruct((t, LANES), F32), jax.ShapeDtypeStruct((N_GATES, t), F32)]
    out_specs = [row(n) for n in widths]
    out_specs += [row(LANES), pl.BlockSpec((N_GATES, tm), lambda i: (0, i))]
    return pl.pallas_call(
        _in_proj_body, grid=(t // tm,),
        in_specs=[row(d), _full((1, d)), _full(w_main.shape), _full(w_gate.shape), _full(w_gate_t.shape)],
        out_specs=out_specs, out_shape=out_shape,
        compiler_params=_params("parallel"), name="in_proj",
    )(h, g, w_main, w_gate, w_gate_t)


def _dft_tables(n):
    k = np.arange(n, dtype=np.int64)
    ang = 2.0 * np.pi * ((k[:, None] * k[None, :]) % n).astype(np.float64) / n
    return np.cos(ang), np.sin(ang)


def _block_diag(blocks):
    h, a, b = blocks.shape
    eye = jnp.eye(h, dtype=blocks.dtype)
    return (eye[:, None, :, None] * blocks[:, :, None, :]).reshape(h * a, h * b)


def _fourier_body(scale, row_tile, z_ref, cs_ref, ss_ref, cm_ref, sm_ref, w_ref, o_ref, p_scr, q_scr):
    z = z_ref[0]
    p_scr[...] = _dot(z, cm_ref[...]).astype(BF16)
    q_scr[...] = _dot(z, sm_ref[...]).astype(BF16)
    s = z.shape[0]
    for r in range(s // row_tile):
        rs = slice(r * row_tile, (r + 1) * row_tile)
        re = (_dot(cs_ref[rs, :], p_scr[...]) - _dot(ss_ref[rs, :], q_scr[...])) * scale
        o_ref[0, rs, :] = _dot(re.astype(BF16), w_ref[...]).astype(o_ref.dtype)


def _fourier(zf, cs, ss, cm, sm, wbd):
    b, s, w = zf.shape
    row_tile = min(512, s)
    scale = 1.0 / math.sqrt(s * FOUR_HD)
    blk = pl.BlockSpec((1, s, w), lambda i: (i, 0, 0))
    return pl.pallas_call(
        functools.partial(_fourier_body, scale, row_tile), grid=(b,),
        in_specs=[blk, _full(cs.shape), _full(ss.shape), _full(cm.shape), _full(sm.shape), _full(wbd.shape)],
        out_specs=blk, out_shape=jax.ShapeDtypeStruct((b, s, w), BF16),
        scratch_shapes=[pltpu.VMEM((s, w), BF16), pltpu.VMEM((s, w), BF16)],
        compiler_params=_params("parallel"), name="fourier",
    )(zf, cs, ss, cm, sm, wbd)


def _gmlp_body(zu_ref, zv_ref, lg_ref, lb_ref, ws_ref, bs_ref, o_ref):
    s = zu_ref.shape[1]
    w = zu_ref.shape[2]
    lane = lax.broadcasted_iota(I32, (GMLP_CHUNK, w), 1)

    def chunk(c, carry):
        r0 = pl.multiple_of(c * GMLP_CHUNK, GMLP_CHUNK)
        rows = pl.ds(r0, GMLP_CHUNK)
        v = jax.nn.gelu(zv_ref[0, rows, :].astype(F32))
        vc = v - jnp.mean(v, axis=-1, keepdims=True)
        vn = vc * lax.rsqrt(jnp.mean(vc * vc, axis=-1, keepdims=True) + EPS) * lg_ref[...] + lb_ref[...]
        acc = bs_ref[...]
        for h in range(GMLP_HEADS):
            vh = jnp.where((lane >= h * GMLP_HD) & (lane < (h + 1) * GMLP_HD), vn, 0.0).astype(BF16)
            acc = acc + _dot(ws_ref[h], vh)
        u = jax.nn.gelu(zu_ref[0, rows, :].astype(F32))
        o_ref[0, rows, :] = (u * acc).astype(o_ref.dtype)
        return carry

    lax.fori_loop(0, s // GMLP_CHUNK, chunk, 0)


def _gmlp(zu, zv, ln_g, ln_b, ws, bs_full):
    b, s, w = zu.shape
    blk = pl.BlockSpec((1, s, w), lambda i: (i, 0, 0))
    return pl.pallas_call(
        _gmlp_body, grid=(b,),
        in_specs=[blk, blk, _full((1, w)), _full((1, w)), _full(ws.shape), _full(bs_full.shape)],
        out_specs=blk, out_shape=jax.ShapeDtypeStruct((b, s, w), BF16),
        compiler_params=_params("parallel"), name="gmlp",
    )(zu, zv, ln_g, ln_b, ws, bs_full)


def _mlstm_body(qkp_ref, v_ref, zo_ref, g_ref, gt_ref, cw_ref, gb_ref, gbt_ref, ng_ref, o_ref,
                qm_scr, km_scr, vt_scr, hf_scr, hb_scr, c_scr, m_scr, gc_scr, gr_scr, cumc_scr, cumr_scr):
    L = MLSTM_CHUNK
    H = MLSTM_HEADS
    DV = MLSTM_DV
    s = v_ref.shape[1]
    nc = s // L
    qkw = H * MLSTM_DQK
    nt = (((1,), (1,)), ((), ()))

    ext = L + 2 * CONV_PAD
    r_i = lax.broadcasted_iota(I32, (L, ext), 0)
    c_i = lax.broadcasted_iota(I32, (L, ext), 1)
    sh_m1 = jnp.where(c_i == r_i + CONV_PAD - 1, 1.0, 0.0).astype(BF16)
    sh_0 = jnp.where(c_i == r_i + CONV_PAD, 1.0, 0.0).astype(BF16)
    sh_p1 = jnp.where(c_i == r_i + CONV_PAD + 1, 1.0, 0.0).astype(BF16)
    lane = lax.broadcasted_iota(I32, (L, LANES), 1)
    low_half = lane < MLSTM_DQK

    def conv_chunk(c, carry):
        r0 = pl.multiple_of(c * L, L)
        rows = pl.ds(r0, L)
        xe = qkp_ref[0, pl.ds(r0, ext), :]
        y = (cw_ref[0:1, :] * _dot(sh_m1, xe) + cw_ref[1:2, :] * _dot(sh_0, xe)
             + cw_ref[2:3, :] * _dot(sh_p1, xe))
        qk = y * jax.nn.sigmoid(y)
        for h in range(H):
            keep = low_half if h % 2 == 0 else jnp.logical_not(low_half)
            t0 = (h // 2) * LANES
            q_t = qk[:, t0:t0 + LANES] * (MLSTM_DQK ** -0.5)
            k_t = qk[:, qkw + t0:qkw + t0 + LANES]
            qm_scr[h, rows, :] = jnp.where(keep, q_t, 0.0).astype(BF16)
            km_scr[h, rows, :] = jnp.where(keep, k_t, 0.0).astype(BF16)
            vt_scr[h * DV:(h + 1) * DV, rows] = v_ref[0, rows, h * DV:(h + 1) * DV].astype(F32).T.astype(BF16)
        gc = g_ref[0, rows, :] + gb_ref[...]
        gr = gt_ref[:, rows] + gbt_ref[...]
        gc_scr[rows, :] = gc
        gr_scr[:, rows] = gr
        lf_c = jax.nn.log_sigmoid(gc)
        lf_r = jax.nn.log_sigmoid(gr)
        cumc_scr[0, rows, :] = jnp.dot(tri_l, lf_c, precision=HIGHEST, preferred_element_type=F32)
        cumc_scr[1, rows, :] = jnp.dot(tri_u, lf_c, precision=HIGHEST, preferred_element_type=F32)
        cumr_scr[0, :, rows] = jnp.dot(lf_r, tri_u, precision=HIGHEST, preferred_element_type=F32)
        cumr_scr[1, :, rows] = jnp.dot(lf_r, tri_l, precision=HIGHEST, preferred_element_type=F32)
        return carry

    ri = lax.broadcasted_iota(I32, (L, L), 0)
    ci = lax.broadcasted_iota(I32, (L, L), 1)
    tri_l = jnp.where(ci <= ri, 1.0, 0.0)
    tri_u = jnp.where(ci >= ri, 1.0, 0.0)
    lax.fori_loop(0, nc, conv_chunk, 0)

    c_scr[...] = jnp.zeros(c_scr.shape, F32)
    m_scr[...] = jnp.zeros(m_scr.shape, F32)

    def direction(d, chunk):
        r0 = pl.multiple_of(chunk * L, L)
        rows = pl.ds(r0, L)
        gc = gc_scr[rows, :]
        gr = gr_scr[:, rows]
        cum_c = cumc_scr[d, rows, :]
        cum_r = cumr_scr[d, :, rows]
        mask = (ci >= ri) if d == 0 else (ci <= ri)
        i_lane = 2 * d * H
        f_lane = (2 * d + 1) * H
        for h in range(H):
            u = d * H + h
            b_r = cum_r[f_lane + h:f_lane + h + 1, :]
            li_r = gr[i_lane + h:i_lane + h + 1, :]
            key_c = gc[:, i_lane + h:i_lane + h + 1] - cum_c[:, f_lane + h:f_lane + h + 1]
            g_edge = b_r[:, L - 1:L] if d == 0 else b_r[:, 0:1]
            g_tot = jnp.broadcast_to(g_edge, (1, L))
            qh = qm_scr[h, rows, :]
            kh = km_scr[h, rows, :]
            vt = vt_scr[h * DV:(h + 1) * DV, rows]
            cn_prev = c_scr[u]
            m_prev = m_scr[u]
            dmat = jnp.where(mask, b_r + key_c, -jnp.inf)
            m_intra = jnp.max(dmat, axis=0, keepdims=True)
            m_inter = b_r + m_prev
            m_tot = jnp.maximum(m_intra, m_inter)
            kq = lax.dot_general(kh, qh, nt, preferred_element_type=F32)
            s_mat = jnp.exp(dmat - m_tot) * kq
            inter = jnp.exp(m_inter - m_tot)
            state_q = lax.dot_general(cn_prev.astype(BF16), qh, nt, preferred_element_type=F32)
            den = jnp.sum(s_mat, axis=0, keepdims=True) + inter * state_q[DV:DV + 1, :]
            num = _dot(vt, s_mat.astype(BF16)) + inter * state_q[0:DV, :]
            dst = hf_scr if d == 0 else hb_scr
            dst[h * DV:(h + 1) * DV, rows] = num / jnp.maximum(jnp.abs(den), jnp.exp(-m_tot))
            a_r = g_tot - b_r + li_r
            m_loc = jnp.broadcast_to(jnp.max(a_r, axis=-1, keepdims=True), (1, L))
            w_r = jnp.exp(a_r - m_loc)
            weighted = jnp.concatenate([vt.astype(F32) * w_r, jnp.broadcast_to(w_r, (8, L))], axis=0)
            cn_loc = _dot(weighted.astype(BF16), kh)
            m_new = jnp.maximum(g_tot + m_prev, m_loc)
            a_old = jnp.exp(g_tot + m_prev - m_new)
            a_new = jnp.exp(m_loc - m_new)
            c_scr[u] = a_old * cn_prev + a_new * cn_loc
            m_scr[u] = m_new

    def step(i, carry):
        direction(0, i)
        direction(1, nc - 1 - i)
        return carry

    lax.fori_loop(0, nc, step, 0)

    def finish(c, carry):
        r0 = pl.multiple_of(c * L, L)
        rows = pl.ds(r0, L)
        og = jax.nn.sigmoid(zo_ref[0, rows, :].astype(F32))
        for h in range(H):
            cols = slice(h * DV, (h + 1) * DV)
            x = og[:, cols] * (hf_scr[cols, rows] + hb_scr[cols, rows]).T
            xc = x - jnp.mean(x, axis=-1, keepdims=True)
            y = xc * lax.rsqrt(jnp.mean(xc * xc, axis=-1, keepdims=True) + EPS) * ng_ref[:, cols]
            o_ref[0, rows, cols] = y.astype(o_ref.dtype)
        return carry

    lax.fori_loop(0, nc, finish, 0)


def _mlstm(zqk_pad, zmv, zo, zg, zgt, conv_w, gate_b, gate_b_t, norm_g):
    b, s, w = zmv.shape
    H = MLSTM_HEADS
    blk = pl.BlockSpec((1, s, w), lambda i: (i, 0, 0))
    units = 2 * H
    return pl.pallas_call(
        _mlstm_body, grid=(b,),
        in_specs=[pl.BlockSpec((1, s + 2 * CONV_PAD, w), lambda i: (i, 0, 0)), blk, blk,
                  pl.BlockSpec((1, s, LANES), lambda i: (i, 0, 0)),
                  pl.BlockSpec((N_GATES, s), lambda i: (0, i)),
                  _full(conv_w.shape), _full(gate_b.shape), _full(gate_b_t.shape), _full(norm_g.shape)],
        out_specs=blk, out_shape=jax.ShapeDtypeStruct((b, s, w), BF16),
        scratch_shapes=[pltpu.VMEM((H, s, LANES), BF16), pltpu.VMEM((H, s, LANES), BF16),
                        pltpu.VMEM((w, s), BF16), pltpu.VMEM((w, s), F32), pltpu.VMEM((w, s), F32),
                        pltpu.VMEM((units, MLSTM_DV + 8, LANES), F32), pltpu.VMEM((units, 1, LANES), F32),
                        pltpu.VMEM((s, LANES), F32), pltpu.VMEM((N_GATES, s), F32),
                        pltpu.VMEM((2, s, LANES), F32), pltpu.VMEM((2, N_GATES, s), F32)],
        compiler_params=_params("parallel"), name="mlstm",
    )(zqk_pad, zmv, zo, zg, zgt, conv_w, gate_b, gate_b_t, norm_g)


def _post_mix_body(h_ref, yf_ref, yg_ref, ym_ref, wo_ref, g2_ref, rwh_ref, rwl_ref, rbt_ref, sgu_ref, sd_ref,
                   hs_ref, xw_ref, sel_ref, gate_ref, idx_ref, sc_scr):
    wf, wg = yf_ref.shape[-1], yg_ref.shape[-1]
    h1 = (h_ref[...] + _dot(yf_ref[...], wo_ref[0:wf, :]) + _dot(yg_ref[...], wo_ref[wf:wf + wg, :])
          + _dot(ym_ref[...], wo_ref[wf + wg:, :]))
    xn = _rms(h1, g2_ref[...])
    xb = xn.astype(BF16)
    _store_rows(xw_ref, _pack_row(xn))
    gu = _dot(xb, sgu_ref[...])
    de = gu.shape[-1] // 2
    act = (jax.nn.silu(gu[:, :de]) * gu[:, de:]).astype(BF16)
    hs_ref[...] = h1 + _dot(act, sd_ref[...])
    nt = (((1,), (1,)), ((), ()))
    x_lo = (xn - xb.astype(F32)).astype(BF16)
    logits = (lax.dot_general(rwh_ref[...], xb, nt, preferred_element_type=F32)
              + lax.dot_general(rwh_ref[...], x_lo, nt, preferred_element_type=F32)
              + lax.dot_general(rwl_ref[...], xb, nt, preferred_element_type=F32))
    sc_scr[...] = jax.nn.sigmoid(logits)
    for c0 in range(0, sc_scr.shape[1], ROUTE_COLS):
        cols = slice(c0, c0 + ROUTE_COLS)
        scores = sc_scr[:, cols]
        work = scores + rbt_ref[:, cols]
        expert = lax.broadcasted_iota(I32, scores.shape, 0)
        choice = lax.broadcasted_iota(I32, (TOP_K, ROUTE_COLS), 0)
        sel = jnp.zeros(scores.shape, F32)
        idx = jnp.zeros(choice.shape, I32)
        for k in range(TOP_K):
            m = jnp.max(work, axis=0, keepdims=True)
            e = jnp.min(jnp.where(work == m, expert, N_EXPERTS), axis=0, keepdims=True)
            hit = expert == e
            sel = jnp.where(hit, 1.0, sel)
            idx = jnp.where(choice == k, e, idx)
            work = jnp.where(hit, -jnp.inf, work)
        picked = sel * scores
        sel_ref[:, cols] = sel
        gate_ref[:, cols] = picked / jnp.sum(picked, axis=0, keepdims=True)
        idx_ref[:, cols] = idx


def _post_mix(h, yf, yg, ym, w_out, g2, router_w_t, router_b_t, sgu, sd):
    t, d = h.shape
    tm = min(TOKEN_TILE, t)
    row = lambda n: pl.BlockSpec((tm, n), lambda i: (i, 0))
    col = lambda n: pl.BlockSpec((n, tm), lambda i: (0, i))
    rw_hi = router_w_t.astype(BF16)
    rw_lo = (router_w_t - rw_hi.astype(F32)).astype(BF16)
    return pl.pallas_call(
        _post_mix_body, grid=(t // tm,),
        in_specs=[row(d), row(yf.shape[1]), row(yg.shape[1]), row(ym.shape[1]), _full(w_out.shape),
                  _full((1, d)), _full(rw_hi.shape), _full(rw_lo.shape), _full(router_b_t.shape),
                  _full(sgu.shape), _full(sd.shape)],
        out_specs=[row(d), pl.BlockSpec((tm * ROW_SUB, LANES), lambda i: (i, 0)),
                   col(N_EXPERTS), col(N_EXPERTS), col(TOP_K)],
        out_shape=[jax.ShapeDtypeStruct((t, d), F32), jax.ShapeDtypeStruct((t * ROW_SUB, LANES), U32),
                   jax.ShapeDtypeStruct((N_EXPERTS, t), F32), jax.ShapeDtypeStruct((N_EXPERTS, t), F32),
                   jax.ShapeDtypeStruct((TOP_K, t), I32)],
        scratch_shapes=[pltpu.VMEM((N_EXPERTS, tm), F32)],
        compiler_params=_params("parallel"), name="post_mix",
    )(h, yf, yg, ym, w_out, g2, rw_hi, rw_lo, router_b_t, sgu, sd)


def _rank_body(sel_ref, rank_ref, cnt_ref, carry):
    @pl.when(pl.program_id(0) == 0)
    def _():
        carry[...] = jnp.zeros(carry.shape, F32)

    m = sel_ref[...].astype(BF16)
    tm = m.shape[1]
    ri = lax.broadcasted_iota(I32, (tm, tm), 0)
    ci = lax.broadcasted_iota(I32, (tm, tm), 1)
    earlier = jnp.where(ri < ci, 1.0, 0.0).astype(BF16)
    seen = carry[...]
    rank = _dot(m, earlier) + jnp.concatenate([seen] * (tm // LANES), axis=1)
    rank_ref[...] = rank.astype(I32)
    seen = seen + _dot(m, jnp.ones((tm, LANES), BF16))
    carry[...] = seen
    cnt_ref[...] = seen.astype(I32)


def _rank(sel_t):
    e, t = sel_t.shape
    tm = min(TOKEN_TILE, t)
    return pl.pallas_call(
        _rank_body, grid=(t // tm,),
        in_specs=[pl.BlockSpec((e, tm), lambda i: (0, i))],
        out_specs=[pl.BlockSpec((e, tm), lambda i: (0, i)), _full((e, LANES))],
        out_shape=[jax.ShapeDtypeStruct((e, t), I32), jax.ShapeDtypeStruct((e, LANES), I32)],
        scratch_shapes=[pltpu.VMEM((e, LANES), F32)],
        compiler_params=_params("arbitrary"), name="rank",
    )(sel_t)


def _dest_body(rank_ref, idx_ref, gate_ref, offs_ref, dest_ref, g8_ref):
    tm = rank_ref.shape[1]
    offs = jnp.concatenate([offs_ref[...]] * (tm // LANES), axis=1)
    pos = (rank_ref[...] + offs).astype(F32)
    gates = gate_ref[...]
    expert = lax.broadcasted_iota(I32, pos.shape, 0)
    choice = lax.broadcasted_iota(I32, (TOP_K, tm), 0)
    dest = jnp.zeros(choice.shape, F32)
    g8 = jnp.zeros(choice.shape, F32)
    for k in range(TOP_K):
        hit = expert == idx_ref[k:k + 1, :]
        dk = jnp.sum(jnp.where(hit, pos, 0.0), axis=0, keepdims=True)
        gk = jnp.sum(jnp.where(hit, gates, 0.0), axis=0, keepdims=True)
        dest = jnp.where(choice == k, dk, dest)
        g8 = jnp.where(choice == k, gk, g8)
    dest_ref[...] = dest.astype(I32)
    g8_ref[...] = g8


def _dest(rank_t, idx_t, gates_t, offs_rep):
    e, t = rank_t.shape
    tm = min(TOKEN_TILE, t)
    col = lambda n: pl.BlockSpec((n, tm), lambda i: (0, i))
    return pl.pallas_call(
        _dest_body, grid=(t // tm,),
        in_specs=[col(e), col(TOP_K), col(e), _full((e, LANES))],
        out_specs=[col(TOP_K), col(TOP_K)],
        out_shape=[jax.ShapeDtypeStruct((TOP_K, t), I32), jax.ShapeDtypeStruct((TOP_K, t), F32)],
        compiler_params=_params("parallel"), name="dest",
    )(rank_t, idx_t, gates_t, offs_rep)


def _dispatch_body(dest_ref, tail_ref, x_ref, xs_hbm, zero_scr, sem):
    n_grp = x_ref.shape[0]

    @pl.when(pl.program_id(0) == 0)
    def _():
        zero_scr[...] = jnp.zeros(zero_scr.shape, zero_scr.dtype)

        def tail_copy(e):
            start = pl.multiple_of(jnp.maximum(tail_ref[e], 0) * ROW_SUB, EXPERT_BLOCK * ROW_SUB)
            return pltpu.make_async_copy(zero_scr, xs_hbm.at[pl.ds(start, EXPERT_BLOCK * ROW_SUB), :], sem)

        def clear(e, carry):
            @pl.when(tail_ref[e] >= 0)
            def _():
                tail_copy(e).start()
            return carry

        def clear_done(e, carry):
            @pl.when(tail_ref[e] >= 0)
            def _():
                tail_copy(e).wait()
            return carry

        lax.fori_loop(0, N_EXPERTS, clear, 0)
        lax.fori_loop(0, N_EXPERTS, clear_done, 0)

    def row_copy(group, g, dst_row):
        dst = pl.ds(pl.multiple_of(dst_row * ROW_SUB, ROW_SUB), ROW_SUB)
        return pltpu.make_async_copy(x_ref.at[group, g], xs_hbm.at[dst, :], sem)

    def for_group(op):
        def body(gi, carry):
            p0 = gi * (ISSUE_GROUP * TOP_K)
            for g in range(ISSUE_GROUP):
                for k in range(TOP_K):
                    op(row_copy(gi, g, dest_ref[p0 + g * TOP_K + k]), k)
            return carry
        lax.fori_loop(0, n_grp, body, 0)

    for_group(lambda cp, k: cp.start(priority=k % 2))
    for_group(lambda cp, k: cp.wait())


def _dispatch(dest_flat, tail_start, xw, n_rows):
    t = xw.shape[0] // ROW_SUB
    tt = min(DISPATCH_TILE, t)
    x4 = xw.reshape(t // ISSUE_GROUP, ISSUE_GROUP, ROW_SUB, LANES)
    return pl.pallas_call(
        _dispatch_body, grid=(t // tt,),
        in_specs=[pl.BlockSpec((tt * TOP_K,), lambda i: (i,), memory_space=pltpu.SMEM),
                  pl.BlockSpec(memory_space=pltpu.SMEM),
                  pl.BlockSpec((tt // ISSUE_GROUP, ISSUE_GROUP, ROW_SUB, LANES), lambda i: (i, 0, 0, 0))],
        out_specs=pl.BlockSpec(memory_space=pl.ANY),
        out_shape=jax.ShapeDtypeStruct((n_rows * ROW_SUB, LANES), xw.dtype),
        scratch_shapes=[pltpu.VMEM((EXPERT_BLOCK * ROW_SUB, LANES), xw.dtype), pltpu.SemaphoreType.DMA(())],
        compiler_params=_params("arbitrary"), name="dispatch",
    )(dest_flat, tail_start, x4)


def _experts_body(layer, blk_e_ref, n_used_ref, first_ref, next_e_ref, xs_ref, wg_hbm, wu_hbm, wd_hbm, ys_ref,
                  wg_buf, wu_buf, wd_buf, wg_scr, wu_scr, wd_scr, slot_ref, sems):
    i = pl.program_id(0)
    used = i < n_used_ref[0]

    def weight_copies(e, slot):
        return [pltpu.make_async_copy(hbm.at[layer, e], buf.at[slot], sems.at[slot])
                for hbm, buf in ((wg_hbm, wg_buf), (wu_hbm, wu_buf), (wd_hbm, wd_buf))]

    @pl.when(i == 0)
    def _():
        slot_ref[0] = 0
        for cp in weight_copies(blk_e_ref[0], 0):
            cp.start()

    @pl.when(first_ref[i] == 1)
    def _():
        slot = slot_ref[0]
        for cp in weight_copies(blk_e_ref[i], slot):
            cp.wait()
        wg_scr[...] = wg_buf[slot].astype(BF16)
        wu_scr[...] = wu_buf[slot].astype(BF16)
        wd_scr[...] = wd_buf[slot].astype(BF16)

        @pl.when(next_e_ref[i] >= 0)
        def _():
            for cp in weight_copies(next_e_ref[i], 1 - slot):
                cp.start(priority=1)

        slot_ref[0] = 1 - slot

    @pl.when(used)
    def _():
        halves = [_unpack_row(w) for w in _load_rows(xs_ref, 0, EXPERT_BLOCK)]
        x = jnp.concatenate([lo for lo, _ in halves] + [hi for _, hi in halves], axis=-1).astype(BF16)
        a = (jax.nn.silu(_dot(x, wg_scr[...])) * _dot(x, wu_scr[...])).astype(BF16)
        _store_rows(ys_ref, _pack_row(_dot(a, wd_scr[...])))


def _experts(layer, blk_e, n_used, first, next_e, xs, wg, wu, wd):
    blk = (EXPERT_BLOCK * ROW_SUB, LANES)
    nb = xs.shape[0] // blk[0]
    d, de = wg.shape[-2], wg.shape[-1]
    last_used = lambda i, nu: jnp.minimum(i, nu[0] - 1)
    hbm = pl.BlockSpec(memory_space=pl.ANY)
    grid_spec = pltpu.PrefetchScalarGridSpec(
        num_scalar_prefetch=4, grid=(nb,),
        in_specs=[pl.BlockSpec(blk, lambda i, be, nu, fi, ne: (last_used(i, nu), 0)), hbm, hbm, hbm],
        out_specs=pl.BlockSpec(blk, lambda i, be, nu, fi, ne: (last_used(i, nu), 0)),
        scratch_shapes=[pltpu.VMEM((2, d, de), F32), pltpu.VMEM((2, d, de), F32), pltpu.VMEM((2, de, d), F32),
                        pltpu.VMEM((d, de), BF16), pltpu.VMEM((d, de), BF16), pltpu.VMEM((de, d), BF16),
                        pltpu.SMEM((1,), I32), pltpu.SemaphoreType.DMA((2,))])
    return pl.pallas_call(
        functools.partial(_experts_body, layer), grid_spec=grid_spec,
        out_shape=jax.ShapeDtypeStruct(xs.shape, xs.dtype),
        compiler_params=_params("arbitrary"), name="experts",
    )(blk_e, n_used, first, next_e, xs, wg, wu, wd)


def _combine_body(dest_ref, dest_next_ref, g8_ref, hs_ref, ys_hbm, o_ref, buf, sems):
    n_tok = hs_ref.shape[0]
    n_grp = n_tok // ISSUE_GROUP
    step = pl.program_id(0)
    cur = step % 2
    half_rows = TOP_K * n_tok

    def for_group(idx_ref, slot, op):
        base = slot * (half_rows * ROW_SUB)

        def row_copy(src_row, k, gi, g):
            src = pl.ds(pl.multiple_of(src_row * ROW_SUB, ROW_SUB), ROW_SUB)
            dst = base + (k * n_grp + gi) * (ISSUE_GROUP * ROW_SUB) + g * ROW_SUB
            return pltpu.make_async_copy(ys_hbm.at[src, :], buf.at[pl.ds(pl.multiple_of(dst, ROW_SUB), ROW_SUB), :],
                                         sems.at[slot])

        def body(gi, carry):
            p0 = gi * (ISSUE_GROUP * TOP_K)
            for g in range(ISSUE_GROUP):
                for k in range(TOP_K):
                    op(row_copy(idx_ref[p0 + g * TOP_K + k], k, gi, g), k)
            return carry
        lax.fori_loop(0, n_grp, body, 0)

    start = lambda cp, k: cp.start(priority=k % 2)

    @pl.when(step == 0)
    def _():
        for_group(dest_ref, 0, start)

    @pl.when(step + 1 < pl.num_programs(0))
    def _():
        for_group(dest_next_ref, 1 - cur, start)

    for_group(dest_ref, cur, lambda cp, k: cp.wait())

    half = ROW_SUB * LANES

    def reduce_group(r0):
        rows = pl.ds(r0, ISSUE_GROUP)
        gates = [g8_ref[rows, k:k + 1] for k in range(TOP_K)]
        picked = [_load_rows(buf, cur * half_rows + k * n_tok + r0, ISSUE_GROUP) for k in range(TOP_K)]
        for c in range(ROW_SUB):
            lo_cols = slice(c * LANES, (c + 1) * LANES)
            hi_cols = slice(half + c * LANES, half + (c + 1) * LANES)
            acc_lo = hs_ref[rows, lo_cols]
            acc_hi = hs_ref[rows, hi_cols]
            for k in range(TOP_K):
                lo, hi = _unpack_row(picked[k][c])
                acc_lo = acc_lo + gates[k] * lo
                acc_hi = acc_hi + gates[k] * hi
            o_ref[rows, lo_cols] = acc_lo
            o_ref[rows, hi_cols] = acc_hi

    def reduce(gi, carry):
        r0 = pl.multiple_of(gi * (2 * ISSUE_GROUP), 2 * ISSUE_GROUP)
        reduce_group(r0)
        reduce_group(r0 + ISSUE_GROUP)
        return carry

    lax.fori_loop(0, n_grp // 2, reduce, 0)


def _combine(dest_flat, g8, hs, ys):
    t, d = hs.shape
    tt = min(COMBINE_TILE, t)
    n_steps = t // tt
    return pl.pallas_call(
        _combine_body, grid=(n_steps,),
        in_specs=[pl.BlockSpec((tt * TOP_K,), lambda i: (i,), memory_space=pltpu.SMEM),
                  pl.BlockSpec((tt * TOP_K,), lambda i: (jnp.minimum(i + 1, n_steps - 1),), memory_space=pltpu.SMEM),
                  pl.BlockSpec((tt, g8.shape[1]), lambda i: (i, 0)),
                  pl.BlockSpec((tt, d), lambda i: (i, 0)),
                  pl.BlockSpec(memory_space=pl.ANY)],
        out_specs=pl.BlockSpec((tt, d), lambda i: (i, 0)),
        out_shape=jax.ShapeDtypeStruct((t, d), F32),
        scratch_shapes=[pltpu.VMEM((2 * TOP_K * tt * ROW_SUB, LANES), ys.dtype), pltpu.SemaphoreType.DMA((2,))],
        compiler_params=_params("arbitrary"), name="combine",
    )(dest_flat, dest_flat, g8, hs, ys)


def _ple_body(final, h2_ref, p_ref, win_ref, wgate_ref, g_ref, gf_ref, o_ref):
    h2 = h2_ref[...]
    e = _dot(p_ref[...].astype(BF16), win_ref[...])
    gate = jax.nn.sigmoid(_dot(h2.astype(BF16), wgate_ref[...]))
    h3 = h2 + _rms(gate * e, g_ref[...])
    o_ref[...] = _rms(h3, gf_ref[...]) if final else h3


def _ple(h2, p, w_in, w_gate, g, g_final, final):
    t, d = h2.shape
    tm = min(TOKEN_TILE, t)
    row = lambda n: pl.BlockSpec((tm, n), lambda i: (i, 0))
    return pl.pallas_call(
        functools.partial(_ple_body, final), grid=(t // tm,),
        in_specs=[row(d), row(p.shape[1]), _full(w_in.shape), _full(w_gate.shape), _full((1, d)), _full((1, d))],
        out_specs=row(d), out_shape=jax.ShapeDtypeStruct((t, d), F32),
        compiler_params=_params("parallel"), name="ple",
    )(h2, p, w_in, w_gate, g, g_final)


def _mixers(h, bsz, seq, norm1_g, w_in, four_w, ln_g, ln_b, ws, bs, conv_w, gate_b, norm_g, tables):
    d = h.shape[1]
    n_main = w_in.shape[1] - N_GATES
    w_main = w_in[:, :n_main].astype(BF16)
    w_g = w_in[:, n_main:]
    w_gate = jnp.pad(w_g, ((0, 0), (0, LANES - N_GATES))).astype(BF16)
    w_gate_t = w_g.T.astype(BF16)
    zf, zu, zv, zqk, zmv, zo, zg, zgt = _in_proj(h, norm1_g.reshape(1, d), w_main, w_gate, w_gate_t)
    b3 = lambda a: a.reshape(bsz, seq, a.shape[-1])

    cs, ss, cm, sm = tables
    yf = _fourier(b3(zf), cs, ss, cm, sm, _block_diag(four_w).astype(BF16))

    bs_full = jnp.repeat(bs.T, GMLP_HD, axis=1)
    yg = _gmlp(b3(zu), b3(zv), ln_g.reshape(1, -1), ln_b.reshape(1, -1), ws.astype(BF16), bs_full)

    zqk_pad = jnp.pad(b3(zqk), ((0, 0), (CONV_PAD, CONV_PAD), (0, 0)))
    gate_b_c = jnp.pad(gate_b, (0, LANES - N_GATES)).reshape(1, LANES)
    gate_b_t = jnp.broadcast_to(gate_b.reshape(N_GATES, 1), (N_GATES, LANES))
    ym = _mlstm(zqk_pad, b3(zmv), b3(zo), b3(zg), zgt, conv_w, gate_b_c, gate_b_t, norm_g.reshape(1, -1))
    t = bsz * seq
    return yf.reshape(t, -1), yg.reshape(t, -1), ym.reshape(t, -1)


def _moe(layer, h, yf, yg, ym, w_out, norm2_g, router_w, router_b, wg, wu, wd, sg, su, sd):
    t, d = h.shape
    sgu = jnp.concatenate([sg, su], axis=1).astype(BF16)
    tm = min(TOKEN_TILE, t)
    router_b_t = jnp.broadcast_to(router_b.reshape(-1, 1), (N_EXPERTS, tm))
    hs, xw, sel, gates, idx = _post_mix(h, yf, yg, ym, w_out.astype(BF16), norm2_g.reshape(1, d),
                                        router_w.T, router_b_t, sgu, sd.astype(BF16))
    rank, counts = _rank(sel)
    counts = counts[:, 0]
    pcounts = ((counts + EXPERT_BLOCK - 1) // EXPERT_BLOCK) * EXPERT_BLOCK
    pend = jnp.cumsum(pcounts)
    offs = (pend - pcounts).astype(I32)
    tail_start = jnp.where(pcounts > 0, pend - EXPERT_BLOCK, -1).astype(I32)
    n_blk = (t * TOP_K) // EXPERT_BLOCK + N_EXPERTS
    blk_start = jnp.arange(n_blk, dtype=I32) * EXPERT_BLOCK
    blk_e = jnp.minimum(jnp.sum(pend[None, :] <= blk_start[:, None], axis=1), N_EXPERTS - 1).astype(I32)
    n_used = (pend[-1:] // EXPERT_BLOCK).astype(I32)
    blk_i = jnp.arange(n_blk, dtype=I32)
    first = ((blk_i < n_used[0]) & ((blk_i == 0) | (blk_e != jnp.roll(blk_e, 1)))).astype(I32)
    after = pend[blk_e] // EXPERT_BLOCK
    next_e = jnp.where(after < n_used[0], blk_e[jnp.minimum(after, n_blk - 1)], -1).astype(I32)

    dest, g8 = _dest(rank, idx, gates, jnp.broadcast_to(offs.reshape(-1, 1), (N_EXPERTS, LANES)))
    dest_flat = dest.T.reshape(-1)
    g8 = jnp.pad(g8.T, ((0, 0), (0, LANES - TOP_K)))
    xs = _dispatch(dest_flat, tail_start, xw, n_blk * EXPERT_BLOCK)
    ys = _experts(layer, blk_e, n_used, first, next_e, xs, wg, wu, wd)
    return _combine(dest_flat, g8, hs, ys)


def kernel(x, p, norm1_g, w_in, four_w, gmlp_ln_g, gmlp_ln_b, gmlp_ws, gmlp_bs, mlstm_conv_w,
           mlstm_gate_b, mlstm_norm_g, w_out, norm2_g, router_w, router_b, exp_w_gate, exp_w_up,
           exp_w_down, sh_w_gate, sh_w_up, sh_w_down, ple_w_in, ple_w_gate, ple_norm_g, final_norm_g):
    bsz, seq, d = x.shape
    depth = w_in.shape[0]
    t = bsz * seq
    cos_s, sin_s = _dft_tables(seq)
    cos_m, sin_m = _dft_tables(FOUR_HD)
    eye = np.eye(FOUR_HEADS)
    tables = (jnp.asarray(cos_s, BF16), jnp.asarray(sin_s, BF16),
              jnp.asarray(np.kron(eye, cos_m), BF16), jnp.asarray(np.kron(eye, sin_m), BF16))
    h = x.reshape(t, d)
    for i in range(depth):
        yf, yg, ym = _mixers(h, bsz, seq, norm1_g[i], w_in[i], four_w[i], gmlp_ln_g[i], gmlp_ln_b[i],
                             gmlp_ws[i], gmlp_bs[i], mlstm_conv_w[i], mlstm_gate_b[i], mlstm_norm_g[i], tables)
        h2 = _moe(i, h, yf, yg, ym, w_out[i], norm2_g[i], router_w[i], router_b[i],
                  exp_w_gate, exp_w_up, exp_w_down, sh_w_gate[i], sh_w_up[i], sh_w_down[i])
        h = _ple(h2, p[i].reshape(t, -1), ple_w_in[i].astype(BF16), ple_w_gate[i].astype(BF16),
                 ple_norm_g[i].reshape(1, d), final_norm_g.reshape(1, d), i == depth - 1)
    return h.reshape(bsz, seq, d)
```

```python
import functools
import math

import numpy as np
import jax
import jax.numpy as jnp
from jax import lax
from jax.experimental import pallas as pl
from jax.experimental.pallas import tpu as pltpu

F32, BF16, I32, U32 = jnp.float32, jnp.bfloat16, jnp.int32, jnp.uint32

EPS = 1e-6
LANES = 128
VMEM_LIMIT_BYTES = 48 * 1024 * 1024

FOUR_HEADS, FOUR_HD = 4, 64
GMLP_HEADS, GMLP_HD, GMLP_CHUNK = 4, 64, 128
MLSTM_HEADS, MLSTM_DV, MLSTM_DQK, MLSTM_CHUNK = 4, 128, 64, 128
CONV_PAD = 16
MLSTM_N_ROWS = 8
N_GATES = 16
N_EXPERTS, TOP_K = 128, 8
EXPERT_BLOCK = 512
TOKEN_TILE = 512
DISPATCH_TILE = 512
COMBINE_TILE = 512
ROUTE_COLS = 128
ISSUE_GROUP = 8


def _params(*sem):
    return pltpu.CompilerParams(dimension_semantics=sem, vmem_limit_bytes=VMEM_LIMIT_BYTES)


def _rms(x, g):
    return x * lax.rsqrt(jnp.mean(x * x, axis=-1, keepdims=True) + EPS) * g


def _full(shape):
    nd = len(shape)
    return pl.BlockSpec(shape, lambda *_: (0,) * nd)


def _dot(a, b):
    return jnp.dot(a, b, preferred_element_type=F32)


def _split3(x):
    hi = x.astype(BF16)
    rest = x - hi.astype(F32)
    mid = rest.astype(BF16)
    return hi, mid, (rest - mid.astype(F32)).astype(BF16)


def _pack_row(x):
    n = x.shape[-1] // 2
    lo = lax.bitcast_convert_type(x[:, :n].astype(BF16).astype(F32), U32)
    hi = lax.bitcast_convert_type(x[:, n:].astype(BF16).astype(F32), U32)
    return hi | (lo >> 16)


def _unpack_row(w):
    lo = lax.bitcast_convert_type(w << 16, F32)
    hi = lax.bitcast_convert_type(w & jnp.uint32(0xFFFF0000), F32)
    return lo, hi


ROW_SUB = 4


def _store_rows(ref, w):
    m = w.shape[0]
    for c in range(ROW_SUB):
        ref[pl.ds(c, m, stride=ROW_SUB), :] = w[:, c * LANES:(c + 1) * LANES]


def _load_rows(ref, r0, m):
    return [ref[pl.ds(r0 * ROW_SUB + c, m, stride=ROW_SUB), :] for c in range(ROW_SUB)]


def _in_proj_body(h_ref, g_ref, w_ref, wg_ref, wgt_ref,
                  zf_ref, zu_ref, zv_ref, zqk_ref, zmv_ref, zo_ref, zg_ref, zgt_ref):
    ab = _rms(h_ref[...], g_ref[...]).astype(BF16)
    off = 0
    for o_ref in (zf_ref, zu_ref, zv_ref, zqk_ref, zmv_ref, zo_ref):
        n = o_ref.shape[-1]
        o_ref[...] = _dot(ab, w_ref[:, off:off + n]).astype(o_ref.dtype)
        off += n
    zg_ref[...] = _dot(ab, wg_ref[...])
    zgt_ref[...] = lax.dot_general(wgt_ref[...], ab, (((1,), (1,)), ((), ())),
                                   preferred_element_type=F32)


def _in_proj(h, g, w_main, w_gate, w_gate_t):
    t, d = h.shape
    tm = min(TOKEN_TILE, t)
    widths = (256, 256, 256, 512, 512, 512)
    row = lambda n: pl.BlockSpec((tm, n), lambda i: (i, 0))
    out_shape = [jax.ShapeDtypeStruct((t, n), BF16) for n in widths]
    out_shape += [jax.ShapeDtypeStruct((t, LANES), F32), jax.ShapeDtypeStruct((N_GATES, t), F32)]
    out_specs = [row(n) for n in widths]
    out_specs += [row(LANES), pl.BlockSpec((N_GATES, tm), lambda i: (0, i))]
    return pl.pallas_call(
        _in_proj_body, grid=(t // tm,),
        in_specs=[row(d), _full((1, d)), _full(w_main.shape), _full(w_gate.shape), _full(w_gate_t.shape)],
        out_specs=out_specs, out_shape=out_shape,
        compiler_params=_params("parallel"), name="in_proj",
    )(h, g, w_main, w_gate, w_gate_t)


def _dft_tables(n):
    k = np.arange(n, dtype=np.int64)
    ang = 2.0 * np.pi * ((k[:, None] * k[None, :]) % n).astype(np.float64) / n
    return np.cos(ang), np.sin(ang)


def _block_diag(blocks):
    h, a, b = blocks.shape
    eye = jnp.eye(h, dtype=blocks.dtype)
    return (eye[:, None, :, None] * blocks[:, :, None, :]).reshape(h * a, h * b)


def _fourier_body(scale, row_tile, z_ref, cs_ref, ss_ref, cm_ref, sm_ref, w_ref, o_ref, p_scr, q_scr):
    z = z_ref[0]
    p_scr[...] = _dot(z, cm_ref[...]).astype(BF16)
    q_scr[...] = _dot(z, sm_ref[...]).astype(BF16)
    s = z.shape[0]
    for r in range(s // row_tile):
        rs = slice(r * row_tile, (r + 1) * row_tile)
        re = (_dot(cs_ref[rs, :], p_scr[...]) - _dot(ss_ref[rs, :], q_scr[...])) * scale
        o_ref[0, rs, :] = _dot(re.astype(BF16), w_ref[...]).astype(o_ref.dtype)


def _fourier(zf, cs, ss, cm, sm, wbd):
    b, s, w = zf.shape
    row_tile = min(512, s)
    scale = 1.0 / math.sqrt(s * FOUR_HD)
    blk = pl.BlockSpec((1, s, w), lambda i: (i, 0, 0))
    return pl.pallas_call(
        functools.partial(_fourier_body, scale, row_tile), grid=(b,),
        in_specs=[blk, _full(cs.shape), _full(ss.shape), _full(cm.shape), _full(sm.shape), _full(wbd.shape)],
        out_specs=blk, out_shape=jax.ShapeDtypeStruct((b, s, w), BF16),
        scratch_shapes=[pltpu.VMEM((s, w), BF16), pltpu.VMEM((s, w), BF16)],
        compiler_params=_params("parallel"), name="fourier",
    )(zf, cs, ss, cm, sm, wbd)


def _gmlp_body(zu_ref, zv_ref, lg_ref, lb_ref, ws_ref, bs_ref, o_ref):
    s = zu_ref.shape[1]
    w = zu_ref.shape[2]
    lane = lax.broadcasted_iota(I32, (GMLP_CHUNK, w), 1)

    def chunk(r0):
        rows = pl.ds(r0, GMLP_CHUNK)
        v = jax.nn.gelu(zv_ref[0, rows, :].astype(F32))
        vc = v - jnp.mean(v, axis=-1, keepdims=True)
        vn = vc * lax.rsqrt(jnp.mean(vc * vc, axis=-1, keepdims=True) + EPS) * lg_ref[...] + lb_ref[...]
        acc = bs_ref[...]
        for h in range(GMLP_HEADS):
            vh = jnp.where((lane >= h * GMLP_HD) & (lane < (h + 1) * GMLP_HD), vn, 0.0).astype(BF16)
            acc = acc + _dot(ws_ref[h], vh)
        u = jax.nn.gelu(zu_ref[0, rows, :].astype(F32))
        o_ref[0, rows, :] = (u * acc).astype(o_ref.dtype)

    def pair(c, carry):
        r0 = pl.multiple_of(c * (2 * GMLP_CHUNK), 2 * GMLP_CHUNK)
        chunk(r0)
        chunk(r0 + GMLP_CHUNK)
        return carry

    lax.fori_loop(0, s // (2 * GMLP_CHUNK), pair, 0)


def _gmlp(zu, zv, ln_g, ln_b, ws, bs_full):
    b, s, w = zu.shape
    blk = pl.BlockSpec((1, s, w), lambda i: (i, 0, 0))
    return pl.pallas_call(
        _gmlp_body, grid=(b,),
        in_specs=[blk, blk, _full((1, w)), _full((1, w)), _full(ws.shape), _full(bs_full.shape)],
        out_specs=blk, out_shape=jax.ShapeDtypeStruct((b, s, w), BF16),
        compiler_params=_params("parallel"), name="gmlp",
    )(zu, zv, ln_g, ln_b, ws, bs_full)


def _mlstm_body(qkp_ref, v_ref, zo_ref, g_ref, gt_ref, cw_ref, gb_ref, gbt_ref, ng_ref, o_ref,
                qm_scr, km_scr, vt_scr, hf_scr, hb_scr, c_scr, m_scr, gc_scr, gr_scr, cumc_scr, cumr_scr):
    L = MLSTM_CHUNK
    H = MLSTM_HEADS
    DV = MLSTM_DV
    N_ROWS = MLSTM_N_ROWS
    s = v_ref.shape[1]
    nc = s // L
    qkw = H * MLSTM_DQK
    nt = (((1,), (1,)), ((), ()))

    ext = L + 2 * CONV_PAD
    r_i = lax.broadcasted_iota(I32, (L, ext), 0)
    c_i = lax.broadcasted_iota(I32, (L, ext), 1)
    sh_m1 = jnp.where(c_i == r_i + CONV_PAD - 1, 1.0, 0.0).astype(BF16)
    sh_0 = jnp.where(c_i == r_i + CONV_PAD, 1.0, 0.0).astype(BF16)
    sh_p1 = jnp.where(c_i == r_i + CONV_PAD + 1, 1.0, 0.0).astype(BF16)
    lane = lax.broadcasted_iota(I32, (L, LANES), 1)
    low_half = lane < MLSTM_DQK

    def conv_chunk(c, carry):
        r0 = pl.multiple_of(c * L, L)
        rows = pl.ds(r0, L)
        xe = qkp_ref[0, pl.ds(r0, ext), :]
        y = (cw_ref[0:1, :] * _dot(sh_m1, xe) + cw_ref[1:2, :] * _dot(sh_0, xe)
             + cw_ref[2:3, :] * _dot(sh_p1, xe))
        qk = y * jax.nn.sigmoid(y)
        for h in range(H):
            keep = low_half if h % 2 == 0 else jnp.logical_not(low_half)
            t0 = (h // 2) * LANES
            q_t = qk[:, t0:t0 + LANES] * (MLSTM_DQK ** -0.5)
            k_t = qk[:, qkw + t0:qkw + t0 + LANES]
            qm_scr[h, rows, :] = jnp.where(keep, q_t, 0.0).astype(BF16)
            km_scr[h, rows, :] = jnp.where(keep, k_t, 0.0).astype(BF16)
            vt_scr[h * DV:(h + 1) * DV, rows] = v_ref[0, rows, h * DV:(h + 1) * DV].astype(F32).T.astype(BF16)
        gc = g_ref[0, rows, :] + gb_ref[...]
        gr = gt_ref[:, rows] + gbt_ref[...]
        gc_scr[rows, :] = gc
        gr_scr[:, rows] = gr
        lf_c = jax.nn.log_sigmoid(gc)
        lf_r = jax.nn.log_sigmoid(gr)
        parts_c = _split3(lf_c)
        parts_r = _split3(lf_r)
        cumc_scr[0, rows, :] = sum(_dot(tri_l, p) for p in parts_c)
        cumc_scr[1, rows, :] = sum(_dot(tri_u, p) for p in parts_c)
        cumr_scr[0, :, rows] = sum(_dot(p, tri_u) for p in parts_r)
        cumr_scr[1, :, rows] = sum(_dot(p, tri_l) for p in parts_r)
        return carry

    ri = lax.broadcasted_iota(I32, (L, L), 0)
    ci = lax.broadcasted_iota(I32, (L, L), 1)
    tri_l = jnp.where(ci <= ri, 1.0, 0.0).astype(BF16)
    tri_u = jnp.where(ci >= ri, 1.0, 0.0).astype(BF16)
    lax.fori_loop(0, nc, conv_chunk, 0)

    c_scr[...] = jnp.zeros(c_scr.shape, F32)
    m_scr[...] = jnp.zeros(m_scr.shape, F32)

    def direction(d, chunk):
        r0 = pl.multiple_of(chunk * L, L)
        rows = pl.ds(r0, L)
        gc = gc_scr[rows, :]
        gr = gr_scr[:, rows]
        cum_c = cumc_scr[d, rows, :]
        cum_r = cumr_scr[d, :, rows]
        mask = (ci >= ri) if d == 0 else (ci <= ri)
        i_lane = 2 * d * H
        f_lane = (2 * d + 1) * H
        for h in range(H):
            u = d * H + h
            b_r = cum_r[f_lane + h:f_lane + h + 1, :]
            li_r = gr[i_lane + h:i_lane + h + 1, :]
            key_c = gc[:, i_lane + h:i_lane + h + 1] - cum_c[:, f_lane + h:f_lane + h + 1]
            g_edge = b_r[:, L - 1:L] if d == 0 else b_r[:, 0:1]
            g_tot = jnp.broadcast_to(g_edge, (1, L))
            qh = qm_scr[h, rows, :]
            kh = km_scr[h, rows, :]
            vt = vt_scr[h * DV:(h + 1) * DV, rows]
            cn_prev = c_scr[u]
            m_prev = m_scr[u]
            dmat = jnp.where(mask, b_r + key_c, -jnp.inf)
            m_intra = jnp.max(dmat, axis=0, keepdims=True)
            m_inter = b_r + m_prev
            m_tot = jnp.maximum(m_intra, m_inter)
            kq = lax.dot_general(kh, qh, nt, preferred_element_type=F32)
            s_mat = jnp.exp(dmat - m_tot) * kq
            inter = jnp.exp(m_inter - m_tot)
            state_q = lax.dot_general(cn_prev.astype(BF16), qh, nt, preferred_element_type=F32)
            den = jnp.sum(s_mat, axis=0, keepdims=True) + inter * state_q[DV:DV + 1, :]
            num = _dot(vt, s_mat.astype(BF16)) + inter * state_q[0:DV, :]
            dst = hf_scr if d == 0 else hb_scr
            dst[h * DV:(h + 1) * DV, rows] = num / jnp.maximum(jnp.abs(den), jnp.exp(-m_tot))
            a_r = g_tot - b_r + li_r
            m_loc = jnp.broadcast_to(jnp.max(a_r, axis=-1, keepdims=True), (1, L))
            w_r = jnp.exp(a_r - m_loc)
            weighted = jnp.concatenate([vt.astype(F32) * w_r, jnp.broadcast_to(w_r, (N_ROWS, L))], axis=0)
            cn_loc = _dot(weighted.astype(BF16), kh)
            m_new = jnp.maximum(g_tot + m_prev, m_loc)
            a_old = jnp.exp(g_tot + m_prev - m_new)
            a_new = jnp.exp(m_loc - m_new)
            c_scr[u] = a_old * cn_prev + a_new * cn_loc
            m_scr[u] = m_new

    def step(i, carry):
        direction(0, i)
        direction(1, nc - 1 - i)
        return carry

    lax.fori_loop(0, nc, step, 0)

    def finish(c, carry):
        r0 = pl.multiple_of(c * L, L)
        rows = pl.ds(r0, L)
        og = jax.nn.sigmoid(zo_ref[0, rows, :].astype(F32))
        for h in range(H):
            cols = slice(h * DV, (h + 1) * DV)
            x = og[:, cols] * (hf_scr[cols, rows] + hb_scr[cols, rows]).T
            xc = x - jnp.mean(x, axis=-1, keepdims=True)
            y = xc * lax.rsqrt(jnp.mean(xc * xc, axis=-1, keepdims=True) + EPS) * ng_ref[:, cols]
            o_ref[0, rows, cols] = y.astype(o_ref.dtype)
        return carry

    lax.fori_loop(0, nc, finish, 0)


def _mlstm(zqk_pad, zmv, zo, zg, zgt, conv_w, gate_b, gate_b_t, norm_g):
    b, s, w = zmv.shape
    H = MLSTM_HEADS
    blk = pl.BlockSpec((1, s, w), lambda i: (i, 0, 0))
    units = 2 * H
    return pl.pallas_call(
        _mlstm_body, grid=(b,),
        in_specs=[pl.BlockSpec((1, s + 2 * CONV_PAD, w), lambda i: (i, 0, 0)), blk, blk,
                  pl.BlockSpec((1, s, LANES), lambda i: (i, 0, 0)),
                  pl.BlockSpec((N_GATES, s), lambda i: (0, i)),
                  _full(conv_w.shape), _full(gate_b.shape), _full(gate_b_t.shape), _full(norm_g.shape)],
        out_specs=blk, out_shape=jax.ShapeDtypeStruct((b, s, w), BF16),
        scratch_shapes=[pltpu.VMEM((H, s, LANES), BF16), pltpu.VMEM((H, s, LANES), BF16),
                        pltpu.VMEM((w, s), BF16), pltpu.VMEM((w, s), F32), pltpu.VMEM((w, s), F32),
                        pltpu.VMEM((units, MLSTM_DV + MLSTM_N_ROWS, LANES), F32), pltpu.VMEM((units, 1, LANES), F32),
                        pltpu.VMEM((s, LANES), F32), pltpu.VMEM((N_GATES, s), F32),
                        pltpu.VMEM((2, s, LANES), F32), pltpu.VMEM((2, N_GATES, s), F32)],
        compiler_params=_params("parallel"), name="mlstm",
    )(zqk_pad, zmv, zo, zg, zgt, conv_w, gate_b, gate_b_t, norm_g)


def _post_mix_body(h_ref, yf_ref, yg_ref, ym_ref, wo_ref, g2_ref, rwh_ref, rwl_ref, rbt_ref, sgu_ref, sd_ref,
                   hs_ref, xw_ref, sel_ref, gate_ref, idx_ref, sc_scr):
    wf, wg = yf_ref.shape[-1], yg_ref.shape[-1]
    h1 = (h_ref[...] + _dot(yf_ref[...], wo_ref[0:wf, :]) + _dot(yg_ref[...], wo_ref[wf:wf + wg, :])
          + _dot(ym_ref[...], wo_ref[wf + wg:, :]))
    xn = _rms(h1, g2_ref[...])
    xb = xn.astype(BF16)
    _store_rows(xw_ref, _pack_row(xn))
    gu = _dot(xb, sgu_ref[...])
    de = gu.shape[-1] // 2
    act = (jax.nn.silu(gu[:, :de]) * gu[:, de:]).astype(BF16)
    hs_ref[...] = h1 + _dot(act, sd_ref[...])
    nt = (((1,), (1,)), ((), ()))
    x_lo = (xn - xb.astype(F32)).astype(BF16)
    logits = (lax.dot_general(rwh_ref[...], xb, nt, preferred_element_type=F32)
              + lax.dot_general(rwh_ref[...], x_lo, nt, preferred_element_type=F32)
              + lax.dot_general(rwl_ref[...], xb, nt, preferred_element_type=F32))
    sc_scr[...] = jax.nn.sigmoid(logits)
    for c0 in range(0, sc_scr.shape[1], ROUTE_COLS):
        cols = slice(c0, c0 + ROUTE_COLS)
        scores = sc_scr[:, cols]
        work = scores + rbt_ref[:, cols]
        expert = lax.broadcasted_iota(I32, scores.shape, 0)
        choice = lax.broadcasted_iota(I32, (TOP_K, ROUTE_COLS), 0)
        sel = jnp.zeros(scores.shape, F32)
        idx = jnp.zeros(choice.shape, I32)
        for k in range(TOP_K):
            m = jnp.max(work, axis=0, keepdims=True)
            e = jnp.min(jnp.where(work == m, expert, N_EXPERTS), axis=0, keepdims=True)
            hit = expert == e
            sel = jnp.where(hit, 1.0, sel)
            idx = jnp.where(choice == k, e, idx)
            work = jnp.where(hit, -jnp.inf, work)
        picked = sel * scores
        sel_ref[:, cols] = sel
        gate_ref[:, cols] = picked / jnp.sum(picked, axis=0, keepdims=True)
        idx_ref[:, cols] = idx


def _post_mix(h, yf, yg, ym, w_out, g2, router_w_t, router_b_t, sgu, sd):
    t, d = h.shape
    tm = min(TOKEN_TILE, t)
    row = lambda n: pl.BlockSpec((tm, n), lambda i: (i, 0))
    col = lambda n: pl.BlockSpec((n, tm), lambda i: (0, i))
    rw_hi = router_w_t.astype(BF16)
    rw_lo = (router_w_t - rw_hi.astype(F32)).astype(BF16)
    return pl.pallas_call(
        _post_mix_body, grid=(t // tm,),
        in_specs=[row(d), row(yf.shape[1]), row(yg.shape[1]), row(ym.shape[1]), _full(w_out.shape),
                  _full((1, d)), _full(rw_hi.shape), _full(rw_lo.shape), _full(router_b_t.shape),
                  _full(sgu.shape), _full(sd.shape)],
        out_specs=[row(d), pl.BlockSpec((tm * ROW_SUB, LANES), lambda i: (i, 0)),
                   col(N_EXPERTS), col(N_EXPERTS), col(TOP_K)],
        out_shape=[jax.ShapeDtypeStruct((t, d), F32), jax.ShapeDtypeStruct((t * ROW_SUB, LANES), U32),
                   jax.ShapeDtypeStruct((N_EXPERTS, t), F32), jax.ShapeDtypeStruct((N_EXPERTS, t), F32),
                   jax.ShapeDtypeStruct((TOP_K, t), I32)],
        scratch_shapes=[pltpu.VMEM((N_EXPERTS, tm), F32)],
        compiler_params=_params("parallel"), name="post_mix",
    )(h, yf, yg, ym, w_out, g2, rw_hi, rw_lo, router_b_t, sgu, sd)


def _rank_body(sel_ref, rank_ref, cnt_ref, carry):
    @pl.when(pl.program_id(0) == 0)
    def _():
        carry[...] = jnp.zeros(carry.shape, F32)

    m = sel_ref[...].astype(BF16)
    tm = m.shape[1]
    ri = lax.broadcasted_iota(I32, (tm, tm), 0)
    ci = lax.broadcasted_iota(I32, (tm, tm), 1)
    earlier = jnp.where(ri < ci, 1.0, 0.0).astype(BF16)
    seen = carry[...]
    rank = _dot(m, earlier) + jnp.concatenate([seen] * (tm // LANES), axis=1)
    rank_ref[...] = rank.astype(I32)
    seen = seen + _dot(m, jnp.ones((tm, LANES), BF16))
    carry[...] = seen
    cnt_ref[...] = seen.astype(I32)


def _rank(sel_t):
    e, t = sel_t.shape
    tm = min(TOKEN_TILE, t)
    return pl.pallas_call(
        _rank_body, grid=(t // tm,),
        in_specs=[pl.BlockSpec((e, tm), lambda i: (0, i))],
        out_specs=[pl.BlockSpec((e, tm), lambda i: (0, i)), _full((e, LANES))],
        out_shape=[jax.ShapeDtypeStruct((e, t), I32), jax.ShapeDtypeStruct((e, LANES), I32)],
        scratch_shapes=[pltpu.VMEM((e, LANES), F32)],
        compiler_params=_params("arbitrary"), name="rank",
    )(sel_t)


def _dest_body(rank_ref, idx_ref, gate_ref, offs_ref, dest_ref, g8_ref):
    tm = rank_ref.shape[1]
    offs = jnp.concatenate([offs_ref[...]] * (tm // LANES), axis=1)
    pos = (rank_ref[...] + offs).astype(F32)
    gates = gate_ref[...]
    expert = lax.broadcasted_iota(I32, pos.shape, 0)
    choice = lax.broadcasted_iota(I32, (TOP_K, tm), 0)
    dest = jnp.zeros(choice.shape, F32)
    g8 = jnp.zeros(choice.shape, F32)
    for k in range(TOP_K):
        hit = expert == idx_ref[k:k + 1, :]
        dk = jnp.sum(jnp.where(hit, pos, 0.0), axis=0, keepdims=True)
        gk = jnp.sum(jnp.where(hit, gates, 0.0), axis=0, keepdims=True)
        dest = jnp.where(choice == k, dk, dest)
        g8 = jnp.where(choice == k, gk, g8)
    dest_ref[...] = dest.astype(I32)
    g8_ref[...] = g8


def _dest(rank_t, idx_t, gates_t, offs_rep):
    e, t = rank_t.shape
    tm = min(TOKEN_TILE, t)
    col = lambda n: pl.BlockSpec((n, tm), lambda i: (0, i))
    return pl.pallas_call(
        _dest_body, grid=(t // tm,),
        in_specs=[col(e), col(TOP_K), col(e), _full((e, LANES))],
        out_specs=[col(TOP_K), col(TOP_K)],
        out_shape=[jax.ShapeDtypeStruct((TOP_K, t), I32), jax.ShapeDtypeStruct((TOP_K, t), F32)],
        compiler_params=_params("parallel"), name="dest",
    )(rank_t, idx_t, gates_t, offs_rep)


def _dispatch_body(dest_ref, tail_ref, x_ref, xs_hbm, zero_scr, sem):
    n_grp = x_ref.shape[0]

    @pl.when(pl.program_id(0) == 0)
    def _():
        zero_scr[...] = jnp.zeros(zero_scr.shape, zero_scr.dtype)

        def tail_copy(e):
            start = pl.multiple_of(jnp.maximum(tail_ref[e], 0) * ROW_SUB, EXPERT_BLOCK * ROW_SUB)
            return pltpu.make_async_copy(zero_scr, xs_hbm.at[pl.ds(start, EXPERT_BLOCK * ROW_SUB), :], sem)

        def clear(e, carry):
            @pl.when(tail_ref[e] >= 0)
            def _():
                tail_copy(e).start()
            return carry

        def clear_done(e, carry):
            @pl.when(tail_ref[e] >= 0)
            def _():
                tail_copy(e).wait()
            return carry

        lax.fori_loop(0, N_EXPERTS, clear, 0)
        lax.fori_loop(0, N_EXPERTS, clear_done, 0)

    def row_copy(group, g, dst_row):
        dst = pl.ds(pl.multiple_of(dst_row * ROW_SUB, ROW_SUB), ROW_SUB)
        return pltpu.make_async_copy(x_ref.at[group, g], xs_hbm.at[dst, :], sem)

    def for_group(op):
        def body(gi, carry):
            p0 = gi * (ISSUE_GROUP * TOP_K)
            for g in range(ISSUE_GROUP):
                for k in range(TOP_K):
                    op(row_copy(gi, g, dest_ref[p0 + g * TOP_K + k]), k)
            return carry
        lax.fori_loop(0, n_grp, body, 0)

    for_group(lambda cp, k: cp.start(priority=k % 2))
    for_group(lambda cp, k: cp.wait())


def _dispatch(dest_flat, tail_start, xw, n_rows):
    t = xw.shape[0] // ROW_SUB
    tt = min(DISPATCH_TILE, t)
    x4 = xw.reshape(t // ISSUE_GROUP, ISSUE_GROUP, ROW_SUB, LANES)
    return pl.pallas_call(
        _dispatch_body, grid=(t // tt,),
        in_specs=[pl.BlockSpec((tt * TOP_K,), lambda i: (i,), memory_space=pltpu.SMEM),
                  pl.BlockSpec(memory_space=pltpu.SMEM),
                  pl.BlockSpec((tt // ISSUE_GROUP, ISSUE_GROUP, ROW_SUB, LANES), lambda i: (i, 0, 0, 0))],
        out_specs=pl.BlockSpec(memory_space=pl.ANY),
        out_shape=jax.ShapeDtypeStruct((n_rows * ROW_SUB, LANES), xw.dtype),
        scratch_shapes=[pltpu.VMEM((EXPERT_BLOCK * ROW_SUB, LANES), xw.dtype), pltpu.SemaphoreType.DMA(())],
        compiler_params=_params("arbitrary"), name="dispatch",
    )(dest_flat, tail_start, x4)


def _experts_body(layer, blk_e_ref, n_used_ref, first_ref, next_e_ref, xs_ref, wg_hbm, wu_hbm, wd_hbm, ys_ref,
                  wg_buf, wu_buf, wd_buf, wg_scr, wu_scr, wd_scr, slot_ref, sems):
    i = pl.program_id(0)
    used = i < n_used_ref[0]

    def weight_copies(e, slot):
        return [pltpu.make_async_copy(hbm.at[layer, e], buf.at[slot], sems.at[slot])
                for hbm, buf in ((wg_hbm, wg_buf), (wu_hbm, wu_buf), (wd_hbm, wd_buf))]

    @pl.when(i == 0)
    def _():
        slot_ref[0] = 0
        for cp in weight_copies(blk_e_ref[0], 0):
            cp.start()

    @pl.when(first_ref[i] == 1)
    def _():
        slot = slot_ref[0]
        for cp in weight_copies(blk_e_ref[i], slot):
            cp.wait()
        wg_scr[...] = wg_buf[slot].astype(BF16)
        wu_scr[...] = wu_buf[slot].astype(BF16)
        wd_scr[...] = wd_buf[slot].astype(BF16)

        @pl.when(next_e_ref[i] >= 0)
        def _():
            for cp in weight_copies(next_e_ref[i], 1 - slot):
                cp.start(priority=1)

        slot_ref[0] = 1 - slot

    @pl.when(used)
    def _():
        halves = [_unpack_row(w) for w in _load_rows(xs_ref, 0, EXPERT_BLOCK)]
        x = jnp.concatenate([lo for lo, _ in halves] + [hi for _, hi in halves], axis=-1).astype(BF16)
        a = (jax.nn.silu(_dot(x, wg_scr[...])) * _dot(x, wu_scr[...])).astype(BF16)
        _store_rows(ys_ref, _pack_row(_dot(a, wd_scr[...])))


def _experts(layer, blk_e, n_used, first, next_e, xs, wg, wu, wd):
    blk = (EXPERT_BLOCK * ROW_SUB, LANES)
    nb = xs.shape[0] // blk[0]
    d, de = wg.shape[-2], wg.shape[-1]
    last_used = lambda i, nu: jnp.minimum(i, nu[0] - 1)
    hbm = pl.BlockSpec(memory_space=pl.ANY)
    grid_spec = pltpu.PrefetchScalarGridSpec(
        num_scalar_prefetch=4, grid=(nb,),
        in_specs=[pl.BlockSpec(blk, lambda i, be, nu, fi, ne: (last_used(i, nu), 0)), hbm, hbm, hbm],
        out_specs=pl.BlockSpec(blk, lambda i, be, nu, fi, ne: (last_used(i, nu), 0)),
        scratch_shapes=[pltpu.VMEM((2, d, de), F32), pltpu.VMEM((2, d, de), F32), pltpu.VMEM((2, de, d), F32),
                        pltpu.VMEM((d, de), BF16), pltpu.VMEM((d, de), BF16), pltpu.VMEM((de, d), BF16),
                        pltpu.SMEM((1,), I32), pltpu.SemaphoreType.DMA((2,))])
    return pl.pallas_call(
        functools.partial(_experts_body, layer), grid_spec=grid_spec,
        out_shape=jax.ShapeDtypeStruct(xs.shape, xs.dtype),
        compiler_params=_params("arbitrary"), name="experts",
    )(blk_e, n_used, first, next_e, xs, wg, wu, wd)


def _combine_body(dest_ref, dest_next_ref, g8_ref, hs_ref, ys_hbm, o_ref, buf, sems):
    n_tok = hs_ref.shape[0]
    n_grp = n_tok // ISSUE_GROUP
    step = pl.program_id(0)
    cur = step % 2
    half_rows = TOP_K * n_tok

    def for_group(idx_ref, slot, op):
        base = slot * (half_rows * ROW_SUB)

        def row_copy(src_row, k, gi, g):
            src = pl.ds(pl.multiple_of(src_row * ROW_SUB, ROW_SUB), ROW_SUB)
            dst = base + (k * n_grp + gi) * (ISSUE_GROUP * ROW_SUB) + g * ROW_SUB
            return pltpu.make_async_copy(ys_hbm.at[src, :], buf.at[pl.ds(pl.multiple_of(dst, ROW_SUB), ROW_SUB), :],
                                         sems.at[slot])

        def body(gi, carry):
            p0 = gi * (ISSUE_GROUP * TOP_K)
            for g in range(ISSUE_GROUP):
                for k in range(TOP_K):
                    op(row_copy(idx_ref[p0 + g * TOP_K + k], k, gi, g), k)
            return carry
        lax.fori_loop(0, n_grp, body, 0)

    start = lambda cp, k: cp.start(priority=k % 2)

    @pl.when(step == 0)
    def _():
        for_group(dest_ref, 0, start)

    @pl.when(step + 1 < pl.num_programs(0))
    def _():
        for_group(dest_next_ref, 1 - cur, start)

    for_group(dest_ref, cur, lambda cp, k: cp.wait())

    half = ROW_SUB * LANES

    def reduce_group(r0):
        rows = pl.ds(r0, ISSUE_GROUP)
        gates = [g8_ref[rows, k:k + 1] for k in range(TOP_K)]
        picked = [_load_rows(buf, cur * half_rows + k * n_tok + r0, ISSUE_GROUP) for k in range(TOP_K)]
        for c in range(ROW_SUB):
            lo_cols = slice(c * LANES, (c + 1) * LANES)
            hi_cols = slice(half + c * LANES, half + (c + 1) * LANES)
            acc_lo = hs_ref[rows, lo_cols]
            acc_hi = hs_ref[rows, hi_cols]
            for k in range(TOP_K):
                lo, hi = _unpack_row(picked[k][c])
                acc_lo = acc_lo + gates[k] * lo
                acc_hi = acc_hi + gates[k] * hi
            o_ref[rows, lo_cols] = acc_lo
            o_ref[rows, hi_cols] = acc_hi

    def reduce(gi, carry):
        r0 = pl.multiple_of(gi * (2 * ISSUE_GROUP), 2 * ISSUE_GROUP)
        reduce_group(r0)
        reduce_group(r0 + ISSUE_GROUP)
        return carry

    lax.fori_loop(0, n_grp // 2, reduce, 0)


def _combine(dest_flat, g8, hs, ys):
    t, d = hs.shape
    tt = min(COMBINE_TILE, t)
    n_steps = t // tt
    return pl.pallas_call(
        _combine_body, grid=(n_steps,),
        in_specs=[pl.BlockSpec((tt * TOP_K,), lambda i: (i,), memory_space=pltpu.SMEM),
                  pl.BlockSpec((tt * TOP_K,), lambda i: (jnp.minimum(i + 1, n_steps - 1),), memory_space=pltpu.SMEM),
                  pl.BlockSpec((tt, g8.shape[1]), lambda i: (i, 0)),
                  pl.BlockSpec((tt, d), lambda i: (i, 0)),
                  pl.BlockSpec(memory_space=pl.ANY)],
        out_specs=pl.BlockSpec((tt, d), lambda i: (i, 0)),
        out_shape=jax.ShapeDtypeStruct((t, d), F32),
        scratch_shapes=[pltpu.VMEM((2 * TOP_K * tt * ROW_SUB, LANES), ys.dtype), pltpu.SemaphoreType.DMA((2,))],
        compiler_params=_params("arbitrary"), name="combine",
    )(dest_flat, dest_flat, g8, hs, ys)


def _ple_body(final, h2_ref, p_ref, win_ref, wgate_ref, g_ref, gf_ref, o_ref):
    h2 = h2_ref[...]
    e = _dot(p_ref[...].astype(BF16), win_ref[...])
    gate = jax.nn.sigmoid(_dot(h2.astype(BF16), wgate_ref[...]))
    h3 = h2 + _rms(gate * e, g_ref[...])
    o_ref[...] = _rms(h3, gf_ref[...]) if final else h3


def _ple(h2, p, w_in, w_gate, g, g_final, final):
    t, d = h2.shape
    tm = min(TOKEN_TILE, t)
    row = lambda n: pl.BlockSpec((tm, n), lambda i: (i, 0))
    return pl.pallas_call(
        functools.partial(_ple_body, final), grid=(t // tm,),
        in_specs=[row(d), row(p.shape[1]), _full(w_in.shape), _full(w_gate.shape), _full((1, d)), _full((1, d))],
        out_specs=row(d), out_shape=jax.ShapeDtypeStruct((t, d), F32),
        compiler_params=_params("parallel"), name="ple",
    )(h2, p, w_in, w_gate, g, g_final)


def _mixers(h, bsz, seq, norm1_g, w_in, four_w, ln_g, ln_b, ws, bs, conv_w, gate_b, norm_g, tables):
    d = h.shape[1]
    n_main = w_in.shape[1] - N_GATES
    w_main = w_in[:, :n_main].astype(BF16)
    w_g = w_in[:, n_main:]
    w_gate = jnp.pad(w_g, ((0, 0), (0, LANES - N_GATES))).astype(BF16)
    w_gate_t = w_g.T.astype(BF16)
    zf, zu, zv, zqk, zmv, zo, zg, zgt = _in_proj(h, norm1_g.reshape(1, d), w_main, w_gate, w_gate_t)
    b3 = lambda a: a.reshape(bsz, seq, a.shape[-1])

    cs, ss, cm, sm = tables
    yf = _fourier(b3(zf), cs, ss, cm, sm, _block_diag(four_w).astype(BF16))

    bs_full = jnp.repeat(bs.T, GMLP_HD, axis=1)
    yg = _gmlp(b3(zu), b3(zv), ln_g.reshape(1, -1), ln_b.reshape(1, -1), ws.astype(BF16), bs_full)

    zqk_pad = jnp.pad(b3(zqk), ((0, 0), (CONV_PAD, CONV_PAD), (0, 0)))
    gate_b_c = jnp.pad(gate_b, (0, LANES - N_GATES)).reshape(1, LANES)
    gate_b_t = jnp.broadcast_to(gate_b.reshape(N_GATES, 1), (N_GATES, LANES))
    ym = _mlstm(zqk_pad, b3(zmv), b3(zo), b3(zg), zgt, conv_w, gate_b_c, gate_b_t, norm_g.reshape(1, -1))
    t = bsz * seq
    return yf.reshape(t, -1), yg.reshape(t, -1), ym.reshape(t, -1)


def _moe(layer, h, yf, yg, ym, w_out, norm2_g, router_w, router_b, wg, wu, wd, sg, su, sd):
    t, d = h.shape
    sgu = jnp.concatenate([sg, su], axis=1).astype(BF16)
    tm = min(TOKEN_TILE, t)
    router_b_t = jnp.broadcast_to(router_b.reshape(-1, 1), (N_EXPERTS, tm))
    hs, xw, sel, gates, idx = _post_mix(h, yf, yg, ym, w_out.astype(BF16), norm2_g.reshape(1, d),
                                        router_w.T, router_b_t, sgu, sd.astype(BF16))
    rank, counts = _rank(sel)
    counts = counts[:, 0]
    pcounts = ((counts + EXPERT_BLOCK - 1) // EXPERT_BLOCK) * EXPERT_BLOCK
    pend = jnp.cumsum(pcounts)
    offs = (pend - pcounts).astype(I32)
    tail_start = jnp.where(pcounts > 0, pend - EXPERT_BLOCK, -1).astype(I32)
    n_blk = (t * TOP_K) // EXPERT_BLOCK + N_EXPERTS
    blk_start = jnp.arange(n_blk, dtype=I32) * EXPERT_BLOCK
    blk_e = jnp.minimum(jnp.sum(pend[None, :] <= blk_start[:, None], axis=1), N_EXPERTS - 1).astype(I32)
    n_used = (pend[-1:] // EXPERT_BLOCK).astype(I32)
    blk_i = jnp.arange(n_blk, dtype=I32)
    first = ((blk_i < n_used[0]) & ((blk_i == 0) | (blk_e != jnp.roll(blk_e, 1)))).astype(I32)
    after = pend[blk_e] // EXPERT_BLOCK
    next_e = jnp.where(after < n_used[0], blk_e[jnp.minimum(after, n_blk - 1)], -1).astype(I32)

    dest, g8 = _dest(rank, idx, gates, jnp.broadcast_to(offs.reshape(-1, 1), (N_EXPERTS, LANES)))
    dest_flat = dest.T.reshape(-1)
    g8 = jnp.pad(g8.T, ((0, 0), (0, LANES - TOP_K)))
    xs = _dispatch(dest_flat, tail_start, xw, n_blk * EXPERT_BLOCK)
    ys = _experts(layer, blk_e, n_used, first, next_e, xs, wg, wu, wd)
    return _combine(dest_flat, g8, hs, ys)


def kernel(x, p, norm1_g, w_in, four_w, gmlp_ln_g, gmlp_ln_b, gmlp_ws, gmlp_bs, mlstm_conv_w,
           mlstm_gate_b, mlstm_norm_g, w_out, norm2_g, router_w, router_b, exp_w_gate, exp_w_up,
           exp_w_down, sh_w_gate, sh_w_up, sh_w_down, ple_w_in, ple_w_gate, ple_norm_g, final_norm_g):
    bsz, seq, d = x.shape
    depth = w_in.shape[0]
    t = bsz * seq
    cos_s, sin_s = _dft_tables(seq)
    cos_m, sin_m = _dft_tables(FOUR_HD)
    eye = np.eye(FOUR_HEADS)
    tables = (jnp.asarray(cos_s, BF16), jnp.asarray(sin_s, BF16),
              jnp.asarray(np.kron(eye, cos_m), BF16), jnp.asarray(np.kron(eye, sin_m), BF16))
    h = x.reshape(t, d)
    for i in range(depth):
        yf, yg, ym = _mixers(h, bsz, seq, norm1_g[i], w_in[i], four_w[i], gmlp_ln_g[i], gmlp_ln_b[i],
                             gmlp_ws[i], gmlp_bs[i], mlstm_conv_w[i], mlstm_gate_b[i], mlstm_norm_g[i], tables)
        h2 = _moe(i, h, yf, yg, ym, w_out[i], norm2_g[i], router_w[i], router_b[i],
                  exp_w_gate, exp_w_up, exp_w_down, sh_w_gate[i], sh_w_up[i], sh_w_down[i])
        h = _ple(h2, p[i].reshape(t, -1), ple_w_in[i].astype(BF16), ple_w_gate[i].astype(BF16),
                 ple_norm_g[i].reshape(1, d), final_norm_g.reshape(1, d), i == depth - 1)
    return h.reshape(bsz, seq, d)
```

```python
import functools
import math

import numpy as np
import jax
import jax.numpy as jnp
from jax import lax
from jax.experimental import pallas as pl
from jax.experimental.pallas import tpu as pltpu

F32, BF16, I32, U32 = jnp.float32, jnp.bfloat16, jnp.int32, jnp.uint32

EPS = 1e-6
LANES = 128
VMEM_LIMIT_BYTES = 48 * 1024 * 1024

FOUR_HEADS, FOUR_HD = 4, 64
GMLP_HEADS, GMLP_HD, GMLP_CHUNK = 4, 64, 128
MLSTM_HEADS, MLSTM_DV, MLSTM_DQK, MLSTM_CHUNK = 4, 128, 64, 128
CONV_PAD = 16
MLSTM_N_ROWS = 8
N_GATES = 16
N_EXPERTS, TOP_K = 128, 8
EXPERT_BLOCK = 512
TOKEN_TILE = 1024
DISPATCH_TILE = 512
COMBINE_TILE = 512
ROUTE_COLS = 128
ISSUE_GROUP = 8


def _params(*sem):
    return pltpu.CompilerParams(dimension_semantics=sem, vmem_limit_bytes=VMEM_LIMIT_BYTES)


def _rms(x, g):
    return x * lax.rsqrt(jnp.mean(x * x, axis=-1, keepdims=True) + EPS) * g


def _full(shape):
    nd = len(shape)
    return pl.BlockSpec(shape, lambda *_: (0,) * nd)


def _dot(a, b):
    return jnp.dot(a, b, preferred_element_type=F32)


def _split3(x):
    hi = x.astype(BF16)
    rest = x - hi.astype(F32)
    mid = rest.astype(BF16)
    return hi, mid, (rest - mid.astype(F32)).astype(BF16)


def _pack_row(x):
    n = x.shape[-1] // 2
    lo = lax.bitcast_convert_type(x[:, :n].astype(BF16).astype(F32), U32)
    hi = lax.bitcast_convert_type(x[:, n:].astype(BF16).astype(F32), U32)
    return hi | (lo >> 16)


def _unpack_row(w):
    lo = lax.bitcast_convert_type(w << 16, F32)
    hi = lax.bitcast_convert_type(w & jnp.uint32(0xFFFF0000), F32)
    return lo, hi


ROW_SUB = 4


def _store_rows(ref, w):
    m = w.shape[0]
    for c in range(ROW_SUB):
        ref[pl.ds(c, m, stride=ROW_SUB), :] = w[:, c * LANES:(c + 1) * LANES]


def _load_rows(ref, r0, m):
    return [ref[pl.ds(r0 * ROW_SUB + c, m, stride=ROW_SUB), :] for c in range(ROW_SUB)]


def _in_proj_body(h_ref, g_ref, w_ref, wg_ref, wgt_ref,
                  zf_ref, zu_ref, zv_ref, zqk_ref, zmv_ref, zo_ref, zg_ref, zgt_ref):
    ab = _rms(h_ref[...], g_ref[...]).astype(BF16)
    off = 0
    for o_ref in (zf_ref, zu_ref, zv_ref, zqk_ref, zmv_ref, zo_ref):
        n = o_ref.shape[-1]
        o_ref[...] = _dot(ab, w_ref[:, off:off + n]).astype(o_ref.dtype)
        off += n
    zg_ref[...] = _dot(ab, wg_ref[...])
    zgt_ref[...] = lax.dot_general(wgt_ref[...], ab, (((1,), (1,)), ((), ())),
                                   preferred_element_type=F32)


def _in_proj(h, g, w_main, w_gate, w_gate_t):
    t, d = h.shape
    tm = min(TOKEN_TILE, t)
    widths = (256, 256, 256, 512, 512, 512)
    row = lambda n: pl.BlockSpec((tm, n), lambda i: (i, 0))
    out_shape = [jax.ShapeDtypeStruct((t, n), BF16) for n in widths]
    out_shape += [jax.ShapeDtypeStruct((t, LANES), F32), jax.ShapeDtypeStruct((N_GATES, t), F32)]
    out_specs = [row(n) for n in widths]
    out_specs += [row(LANES), pl.BlockSpec((N_GATES, tm), lambda i: (0, i))]
    return pl.pallas_call(
        _in_proj_body, grid=(t // tm,),
        in_specs=[row(d), _full((1, d)), _full(w_main.shape), _full(w_gate.shape), _full(w_gate_t.shape)],
        out_specs=out_specs, out_shape=out_shape,
        compiler_params=_params("parallel"), name="in_proj",
    )(h, g, w_main, w_gate, w_gate_t)


def _dft_tables(n):
    k = np.arange(n, dtype=np.int64)
    ang = 2.0 * np.pi * ((k[:, None] * k[None, :]) % n).astype(np.float64) / n
    return np.cos(ang), np.sin(ang)


def _block_diag(blocks):
    h, a, b = blocks.shape
    eye = jnp.eye(h, dtype=blocks.dtype)
    return (eye[:, None, :, None] * blocks[:, :, None, :]).reshape(h * a, h * b)


def _fourier_body(scale, row_tile, z_ref, cs_ref, ss_ref, cm_ref, sm_ref, w_ref, o_ref, p_scr, q_scr):
    z = z_ref[0]
    p_scr[...] = _dot(z, cm_ref[...]).astype(BF16)
    q_scr[...] = _dot(z, sm_ref[...]).astype(BF16)
    s = z.shape[0]
    for r in range(s // row_tile):
        rs = slice(r * row_tile, (r + 1) * row_tile)
        re = (_dot(cs_ref[rs, :], p_scr[...]) - _dot(ss_ref[rs, :], q_scr[...])) * scale
        o_ref[0, rs, :] = _dot(re.astype(BF16), w_ref[...]).astype(o_ref.dtype)


def _fourier(zf, cs, ss, cm, sm, wbd):
    b, s, w = zf.shape
    row_tile = min(512, s)
    scale = 1.0 / math.sqrt(s * FOUR_HD)
    blk = pl.BlockSpec((1, s, w), lambda i: (i, 0, 0))
    return pl.pallas_call(
        functools.partial(_fourier_body, scale, row_tile), grid=(b,),
        in_specs=[blk, _full(cs.shape), _full(ss.shape), _full(cm.shape), _full(sm.shape), _full(wbd.shape)],
        out_specs=blk, out_shape=jax.ShapeDtypeStruct((b, s, w), BF16),
        scratch_shapes=[pltpu.VMEM((s, w), BF16), pltpu.VMEM((s, w), BF16)],
        compiler_params=_params("parallel"), name="fourier",
    )(zf, cs, ss, cm, sm, wbd)


def _gmlp_body(zu_ref, zv_ref, lg_ref, lb_ref, ws_ref, bs_ref, o_ref):
    s = zu_ref.shape[1]
    w = zu_ref.shape[2]
    lane = lax.broadcasted_iota(I32, (GMLP_CHUNK, w), 1)

    def chunk(r0):
        rows = pl.ds(r0, GMLP_CHUNK)
        v = jax.nn.gelu(zv_ref[0, rows, :].astype(F32))
        vc = v - jnp.mean(v, axis=-1, keepdims=True)
        vn = vc * lax.rsqrt(jnp.mean(vc * vc, axis=-1, keepdims=True) + EPS) * lg_ref[...] + lb_ref[...]
        acc = bs_ref[...]
        for h in range(GMLP_HEADS):
            vh = jnp.where((lane >= h * GMLP_HD) & (lane < (h + 1) * GMLP_HD), vn, 0.0).astype(BF16)
            acc = acc + _dot(ws_ref[h], vh)
        u = jax.nn.gelu(zu_ref[0, rows, :].astype(F32))
        o_ref[0, rows, :] = (u * acc).astype(o_ref.dtype)

    def pair(c, carry):
        r0 = pl.multiple_of(c * (2 * GMLP_CHUNK), 2 * GMLP_CHUNK)
        chunk(r0)
        chunk(r0 + GMLP_CHUNK)
        return carry

    lax.fori_loop(0, s // (2 * GMLP_CHUNK), pair, 0)


def _gmlp(zu, zv, ln_g, ln_b, ws, bs_full):
    b, s, w = zu.shape
    blk = pl.BlockSpec((1, s, w), lambda i: (i, 0, 0))
    return pl.pallas_call(
        _gmlp_body, grid=(b,),
        in_specs=[blk, blk, _full((1, w)), _full((1, w)), _full(ws.shape), _full(bs_full.shape)],
        out_specs=blk, out_shape=jax.ShapeDtypeStruct((b, s, w), BF16),
        compiler_params=_params("parallel"), name="gmlp",
    )(zu, zv, ln_g, ln_b, ws, bs_full)


def _mlstm_body(qkp_ref, v_ref, zo_ref, g_ref, gt_ref, cw_ref, gb_ref, gbt_ref, ng_ref, o_ref,
                qm_scr, km_scr, vt_scr, hf_scr, hb_scr, c_scr, m_scr, gc_scr, gr_scr, cumc_scr, cumr_scr):
    L = MLSTM_CHUNK
    H = MLSTM_HEADS
    DV = MLSTM_DV
    N_ROWS = MLSTM_N_ROWS
    s = v_ref.shape[1]
    nc = s // L
    qkw = H * MLSTM_DQK
    nt = (((1,), (1,)), ((), ()))

    ext = L + 2 * CONV_PAD
    r_i = lax.broadcasted_iota(I32, (L, ext), 0)
    c_i = lax.broadcasted_iota(I32, (L, ext), 1)
    sh_m1 = jnp.where(c_i == r_i + CONV_PAD - 1, 1.0, 0.0).astype(BF16)
    sh_0 = jnp.where(c_i == r_i + CONV_PAD, 1.0, 0.0).astype(BF16)
    sh_p1 = jnp.where(c_i == r_i + CONV_PAD + 1, 1.0, 0.0).astype(BF16)
    lane = lax.broadcasted_iota(I32, (L, LANES), 1)
    low_half = lane < MLSTM_DQK

    def conv_chunk(c, carry):
        r0 = pl.multiple_of(c * L, L)
        rows = pl.ds(r0, L)
        xe = qkp_ref[0, pl.ds(r0, ext), :]
        y = (cw_ref[0:1, :] * _dot(sh_m1, xe) + cw_ref[1:2, :] * _dot(sh_0, xe)
             + cw_ref[2:3, :] * _dot(sh_p1, xe))
        qk = y * jax.nn.sigmoid(y)
        for h in range(H):
            keep = low_half if h % 2 == 0 else jnp.logical_not(low_half)
            t0 = (h // 2) * LANES
            q_t = qk[:, t0:t0 + LANES] * (MLSTM_DQK ** -0.5)
            k_t = qk[:, qkw + t0:qkw + t0 + LANES]
            qm_scr[h, rows, :] = jnp.where(keep, q_t, 0.0).astype(BF16)
            km_scr[h, rows, :] = jnp.where(keep, k_t, 0.0).astype(BF16)
            vt_scr[h * DV:(h + 1) * DV, rows] = v_ref[0, rows, h * DV:(h + 1) * DV].astype(F32).T.astype(BF16)
        gc = g_ref[0, rows, :] + gb_ref[...]
        gr = gt_ref[:, rows] + gbt_ref[...]
        gc_scr[rows, :] = gc
        gr_scr[:, rows] = gr
        lf_c = jax.nn.log_sigmoid(gc)
        lf_r = jax.nn.log_sigmoid(gr)
        parts_c = _split3(lf_c)
        parts_r = _split3(lf_r)
        cumc_scr[0, rows, :] = sum(_dot(tri_l, p) for p in parts_c)
        cumc_scr[1, rows, :] = sum(_dot(tri_u, p) for p in parts_c)
        cumr_scr[0, :, rows] = sum(_dot(p, tri_u) for p in parts_r)
        cumr_scr[1, :, rows] = sum(_dot(p, tri_l) for p in parts_r)
        return carry

    ri = lax.broadcasted_iota(I32, (L, L), 0)
    ci = lax.broadcasted_iota(I32, (L, L), 1)
    tri_l = jnp.where(ci <= ri, 1.0, 0.0).astype(BF16)
    tri_u = jnp.where(ci >= ri, 1.0, 0.0).astype(BF16)
    lax.fori_loop(0, nc, conv_chunk, 0)

    c_scr[...] = jnp.zeros(c_scr.shape, F32)
    m_scr[...] = jnp.zeros(m_scr.shape, F32)

    def direction(d, chunk):
        r0 = pl.multiple_of(chunk * L, L)
        rows = pl.ds(r0, L)
        gc = gc_scr[rows, :]
        gr = gr_scr[:, rows]
        cum_c = cumc_scr[d, rows, :]
        cum_r = cumr_scr[d, :, rows]
        mask = (ci >= ri) if d == 0 else (ci <= ri)
        i_lane = 2 * d * H
        f_lane = (2 * d + 1) * H
        for h in range(H):
            u = d * H + h
            b_r = cum_r[f_lane + h:f_lane + h + 1, :]
            li_r = gr[i_lane + h:i_lane + h + 1, :]
            key_c = gc[:, i_lane + h:i_lane + h + 1] - cum_c[:, f_lane + h:f_lane + h + 1]
            g_edge = b_r[:, L - 1:L] if d == 0 else b_r[:, 0:1]
            g_tot = jnp.broadcast_to(g_edge, (1, L))
            qh = qm_scr[h, rows, :]
            kh = km_scr[h, rows, :]
            vt = vt_scr[h * DV:(h + 1) * DV, rows]
            cn_prev = c_scr[u]
            m_prev = m_scr[u]
            dmat = jnp.where(mask, b_r + key_c, -jnp.inf)
            m_intra = jnp.max(dmat, axis=0, keepdims=True)
            m_inter = b_r + m_prev
            m_tot = jnp.maximum(m_intra, m_inter)
            kq = lax.dot_general(kh, qh, nt, preferred_element_type=F32)
            s_mat = jnp.exp(dmat - m_tot) * kq
            inter = jnp.exp(m_inter - m_tot)
            state_q = lax.dot_general(cn_prev.astype(BF16), qh, nt, preferred_element_type=F32)
            den = jnp.sum(s_mat, axis=0, keepdims=True) + inter * state_q[DV:DV + 1, :]
            num = _dot(vt, s_mat.astype(BF16)) + inter * state_q[0:DV, :]
            dst = hf_scr if d == 0 else hb_scr
            dst[h * DV:(h + 1) * DV, rows] = num / jnp.maximum(jnp.abs(den), jnp.exp(-m_tot))
            a_r = g_tot - b_r + li_r
            m_loc = jnp.broadcast_to(jnp.max(a_r, axis=-1, keepdims=True), (1, L))
            w_r = jnp.exp(a_r - m_loc)
            weighted = jnp.concatenate([vt.astype(F32) * w_r, jnp.broadcast_to(w_r, (N_ROWS, L))], axis=0)
            cn_loc = _dot(weighted.astype(BF16), kh)
            m_new = jnp.maximum(g_tot + m_prev, m_loc)
            a_old = jnp.exp(g_tot + m_prev - m_new)
            a_new = jnp.exp(m_loc - m_new)
            c_scr[u] = a_old * cn_prev + a_new * cn_loc
            m_scr[u] = m_new

    def step(i, carry):
        direction(0, i)
        direction(1, nc - 1 - i)
        return carry

    lax.fori_loop(0, nc, step, 0)

    def finish(c, carry):
        r0 = pl.multiple_of(c * L, L)
        rows = pl.ds(r0, L)
        og = jax.nn.sigmoid(zo_ref[0, rows, :].astype(F32))
        for h in range(H):
            cols = slice(h * DV, (h + 1) * DV)
            x = og[:, cols] * (hf_scr[cols, rows] + hb_scr[cols, rows]).T
            xc = x - jnp.mean(x, axis=-1, keepdims=True)
            y = xc * lax.rsqrt(jnp.mean(xc * xc, axis=-1, keepdims=True) + EPS) * ng_ref[:, cols]
            o_ref[0, rows, cols] = y.astype(o_ref.dtype)
        return carry

    lax.fori_loop(0, nc, finish, 0)


def _mlstm(zqk_pad, zmv, zo, zg, zgt, conv_w, gate_b, gate_b_t, norm_g):
    b, s, w = zmv.shape
    H = MLSTM_HEADS
    blk = pl.BlockSpec((1, s, w), lambda i: (i, 0, 0))
    units = 2 * H
    return pl.pallas_call(
        _mlstm_body, grid=(b,),
        in_specs=[pl.BlockSpec((1, s + 2 * CONV_PAD, w), lambda i: (i, 0, 0)), blk, blk,
                  pl.BlockSpec((1, s, LANES), lambda i: (i, 0, 0)),
                  pl.BlockSpec((N_GATES, s), lambda i: (0, i)),
                  _full(conv_w.shape), _full(gate_b.shape), _full(gate_b_t.shape), _full(norm_g.shape)],
        out_specs=blk, out_shape=jax.ShapeDtypeStruct((b, s, w), BF16),
        scratch_shapes=[pltpu.VMEM((H, s, LANES), BF16), pltpu.VMEM((H, s, LANES), BF16),
                        pltpu.VMEM((w, s), BF16), pltpu.VMEM((w, s), F32), pltpu.VMEM((w, s), F32),
                        pltpu.VMEM((units, MLSTM_DV + MLSTM_N_ROWS, LANES), F32), pltpu.VMEM((units, 1, LANES), F32),
                        pltpu.VMEM((s, LANES), F32), pltpu.VMEM((N_GATES, s), F32),
                        pltpu.VMEM((2, s, LANES), F32), pltpu.VMEM((2, N_GATES, s), F32)],
        compiler_params=_params("parallel"), name="mlstm",
    )(zqk_pad, zmv, zo, zg, zgt, conv_w, gate_b, gate_b_t, norm_g)


def _post_mix_body(h_ref, yf_ref, yg_ref, ym_ref, wo_ref, g2_ref, rwh_ref, rwl_ref, rbt_ref, sgu_ref, sd_ref,
                   hs_ref, xw_ref, sel_ref, gate_ref, idx_ref, sc_scr):
    wf, wg = yf_ref.shape[-1], yg_ref.shape[-1]
    h1 = (h_ref[...] + _dot(yf_ref[...], wo_ref[0:wf, :]) + _dot(yg_ref[...], wo_ref[wf:wf + wg, :])
          + _dot(ym_ref[...], wo_ref[wf + wg:, :]))
    xn = _rms(h1, g2_ref[...])
    xb = xn.astype(BF16)
    _store_rows(xw_ref, _pack_row(xn))
    gu = _dot(xb, sgu_ref[...])
    de = gu.shape[-1] // 2
    act = (jax.nn.silu(gu[:, :de]) * gu[:, de:]).astype(BF16)
    hs_ref[...] = h1 + _dot(act, sd_ref[...])
    nt = (((1,), (1,)), ((), ()))
    x_lo = (xn - xb.astype(F32)).astype(BF16)
    logits = (lax.dot_general(rwh_ref[...], xb, nt, preferred_element_type=F32)
              + lax.dot_general(rwh_ref[...], x_lo, nt, preferred_element_type=F32)
              + lax.dot_general(rwl_ref[...], xb, nt, preferred_element_type=F32))
    sc_scr[...] = jax.nn.sigmoid(logits)
    for c0 in range(0, sc_scr.shape[1], ROUTE_COLS):
        cols = slice(c0, c0 + ROUTE_COLS)
        scores = sc_scr[:, cols]
        work = scores + rbt_ref[:, cols]
        expert = lax.broadcasted_iota(I32, scores.shape, 0)
        choice = lax.broadcasted_iota(I32, (TOP_K, ROUTE_COLS), 0)
        sel = jnp.zeros(scores.shape, F32)
        idx = jnp.zeros(choice.shape, I32)
        for k in range(TOP_K):
            m = jnp.max(work, axis=0, keepdims=True)
            e = jnp.min(jnp.where(work == m, expert, N_EXPERTS), axis=0, keepdims=True)
            hit = expert == e
            sel = jnp.where(hit, 1.0, sel)
            idx = jnp.where(choice == k, e, idx)
            work = jnp.where(hit, -jnp.inf, work)
        picked = sel * scores
        sel_ref[:, cols] = sel
        gate_ref[:, cols] = picked / jnp.sum(picked, axis=0, keepdims=True)
        idx_ref[:, cols] = idx


def _post_mix(h, yf, yg, ym, w_out, g2, router_w_t, router_b_t, sgu, sd):
    t, d = h.shape
    tm = min(TOKEN_TILE, t)
    row = lambda n: pl.BlockSpec((tm, n), lambda i: (i, 0))
    col = lambda n: pl.BlockSpec((n, tm), lambda i: (0, i))
    rw_hi = router_w_t.astype(BF16)
    rw_lo = (router_w_t - rw_hi.astype(F32)).astype(BF16)
    return pl.pallas_call(
        _post_mix_body, grid=(t // tm,),
        in_specs=[row(d), row(yf.shape[1]), row(yg.shape[1]), row(ym.shape[1]), _full(w_out.shape),
                  _full((1, d)), _full(rw_hi.shape), _full(rw_lo.shape), _full(router_b_t.shape),
                  _full(sgu.shape), _full(sd.shape)],
        out_specs=[row(d), pl.BlockSpec((tm * ROW_SUB, LANES), lambda i: (i, 0)),
                   col(N_EXPERTS), col(N_EXPERTS), col(TOP_K)],
        out_shape=[jax.ShapeDtypeStruct((t, d), F32), jax.ShapeDtypeStruct((t * ROW_SUB, LANES), U32),
                   jax.ShapeDtypeStruct((N_EXPERTS, t), F32), jax.ShapeDtypeStruct((N_EXPERTS, t), F32),
                   jax.ShapeDtypeStruct((TOP_K, t), I32)],
        scratch_shapes=[pltpu.VMEM((N_EXPERTS, tm), F32)],
        compiler_params=_params("parallel"), name="post_mix",
    )(h, yf, yg, ym, w_out, g2, rw_hi, rw_lo, router_b_t, sgu, sd)


def _rank_body(sel_ref, rank_ref, cnt_ref, carry):
    @pl.when(pl.program_id(0) == 0)
    def _():
        carry[...] = jnp.zeros(carry.shape, F32)

    m = sel_ref[...].astype(BF16)
    tm = m.shape[1]
    ri = lax.broadcasted_iota(I32, (tm, tm), 0)
    ci = lax.broadcasted_iota(I32, (tm, tm), 1)
    earlier = jnp.where(ri < ci, 1.0, 0.0).astype(BF16)
    seen = carry[...]
    rank = _dot(m, earlier) + jnp.concatenate([seen] * (tm // LANES), axis=1)
    rank_ref[...] = rank.astype(I32)
    seen = seen + _dot(m, jnp.ones((tm, LANES), BF16))
    carry[...] = seen
    cnt_ref[...] = seen.astype(I32)


def _rank(sel_t):
    e, t = sel_t.shape
    tm = min(TOKEN_TILE, t)
    return pl.pallas_call(
        _rank_body, grid=(t // tm,),
        in_specs=[pl.BlockSpec((e, tm), lambda i: (0, i))],
        out_specs=[pl.BlockSpec((e, tm), lambda i: (0, i)), _full((e, LANES))],
        out_shape=[jax.ShapeDtypeStruct((e, t), I32), jax.ShapeDtypeStruct((e, LANES), I32)],
        scratch_shapes=[pltpu.VMEM((e, LANES), F32)],
        compiler_params=_params("arbitrary"), name="rank",
    )(sel_t)


def _dest_body(rank_ref, idx_ref, gate_ref, offs_ref, dest_ref, g8_ref):
    tm = rank_ref.shape[1]
    offs = jnp.concatenate([offs_ref[...]] * (tm // LANES), axis=1)
    pos = (rank_ref[...] + offs).astype(F32)
    gates = gate_ref[...]
    expert = lax.broadcasted_iota(I32, pos.shape, 0)
    choice = lax.broadcasted_iota(I32, (TOP_K, tm), 0)
    dest = jnp.zeros(choice.shape, F32)
    g8 = jnp.zeros(choice.shape, F32)
    for k in range(TOP_K):
        hit = expert == idx_ref[k:k + 1, :]
        dk = jnp.sum(jnp.where(hit, pos, 0.0), axis=0, keepdims=True)
        gk = jnp.sum(jnp.where(hit, gates, 0.0), axis=0, keepdims=True)
        dest = jnp.where(choice == k, dk, dest)
        g8 = jnp.where(choice == k, gk, g8)
    dest_ref[...] = dest.astype(I32)
    g8_ref[...] = g8


def _dest(rank_t, idx_t, gates_t, offs_rep):
    e, t = rank_t.shape
    tm = min(TOKEN_TILE, t)
    col = lambda n: pl.BlockSpec((n, tm), lambda i: (0, i))
    return pl.pallas_call(
        _dest_body, grid=(t // tm,),
        in_specs=[col(e), col(TOP_K), col(e), _full((e, LANES))],
        out_specs=[col(TOP_K), col(TOP_K)],
        out_shape=[jax.ShapeDtypeStruct((TOP_K, t), I32), jax.ShapeDtypeStruct((TOP_K, t), F32)],
        compiler_params=_params("parallel"), name="dest",
    )(rank_t, idx_t, gates_t, offs_rep)


def _dispatch_body(dest_ref, tail_ref, x_ref, xs_hbm, zero_scr, sem):
    n_grp = x_ref.shape[0]

    @pl.when(pl.program_id(0) == 0)
    def _():
        zero_scr[...] = jnp.zeros(zero_scr.shape, zero_scr.dtype)

        def tail_copy(e):
            start = pl.multiple_of(jnp.maximum(tail_ref[e], 0) * ROW_SUB, EXPERT_BLOCK * ROW_SUB)
            return pltpu.make_async_copy(zero_scr, xs_hbm.at[pl.ds(start, EXPERT_BLOCK * ROW_SUB), :], sem)

        def clear(e, carry):
            @pl.when(tail_ref[e] >= 0)
            def _():
                tail_copy(e).start()
            return carry

        def clear_done(e, carry):
            @pl.when(tail_ref[e] >= 0)
            def _():
                tail_copy(e).wait()
            return carry

        lax.fori_loop(0, N_EXPERTS, clear, 0)
        lax.fori_loop(0, N_EXPERTS, clear_done, 0)

    def row_copy(group, g, dst_row):
        dst = pl.ds(pl.multiple_of(dst_row * ROW_SUB, ROW_SUB), ROW_SUB)
        return pltpu.make_async_copy(x_ref.at[group, g], xs_hbm.at[dst, :], sem)

    def for_group(op):
        def body(gi, carry):
            p0 = gi * (ISSUE_GROUP * TOP_K)
            for g in range(ISSUE_GROUP):
                for k in range(TOP_K):
                    op(row_copy(gi, g, dest_ref[p0 + g * TOP_K + k]), k)
            return carry
        lax.fori_loop(0, n_grp, body, 0)

    for_group(lambda cp, k: cp.start(priority=k % 2))
    for_group(lambda cp, k: cp.wait())


def _dispatch(dest_flat, tail_start, xw, n_rows):
    t = xw.shape[0] // ROW_SUB
    tt = min(DISPATCH_TILE, t)
    x4 = xw.reshape(t // ISSUE_GROUP, ISSUE_GROUP, ROW_SUB, LANES)
    return pl.pallas_call(
        _dispatch_body, grid=(t // tt,),
        in_specs=[pl.BlockSpec((tt * TOP_K,), lambda i: (i,), memory_space=pltpu.SMEM),
                  pl.BlockSpec(memory_space=pltpu.SMEM),
                  pl.BlockSpec((tt // ISSUE_GROUP, ISSUE_GROUP, ROW_SUB, LANES), lambda i: (i, 0, 0, 0))],
        out_specs=pl.BlockSpec(memory_space=pl.ANY),
        out_shape=jax.ShapeDtypeStruct((n_rows * ROW_SUB, LANES), xw.dtype),
        scratch_shapes=[pltpu.VMEM((EXPERT_BLOCK * ROW_SUB, LANES), xw.dtype), pltpu.SemaphoreType.DMA(())],
        compiler_params=_params("arbitrary"), name="dispatch",
    )(dest_flat, tail_start, x4)


def _experts_body(layer, blk_e_ref, n_used_ref, first_ref, next_e_ref, xs_ref, wg_hbm, wu_hbm, wd_hbm, ys_ref,
                  wg_buf, wu_buf, wd_buf, wg_scr, wu_scr, wd_scr, slot_ref, sems):
    i = pl.program_id(0)
    used = i < n_used_ref[0]

    def weight_copies(e, slot):
        return [pltpu.make_async_copy(hbm.at[layer, e], buf.at[slot], sems.at[slot])
                for hbm, buf in ((wg_hbm, wg_buf), (wu_hbm, wu_buf), (wd_hbm, wd_buf))]

    @pl.when(i == 0)
    def _():
        slot_ref[0] = 0
        for cp in weight_copies(blk_e_ref[0], 0):
            cp.start()

    @pl.when(first_ref[i] == 1)
    def _():
        slot = slot_ref[0]
        for cp in weight_copies(blk_e_ref[i], slot):
            cp.wait()
        wg_scr[...] = wg_buf[slot].astype(BF16)
        wu_scr[...] = wu_buf[slot].astype(BF16)
        wd_scr[...] = wd_buf[slot].astype(BF16)

        @pl.when(next_e_ref[i] >= 0)
        def _():
            for cp in weight_copies(next_e_ref[i], 1 - slot):
                cp.start(priority=1)

        slot_ref[0] = 1 - slot

    @pl.when(used)
    def _():
        halves = [_unpack_row(w) for w in _load_rows(xs_ref, 0, EXPERT_BLOCK)]
        x = jnp.concatenate([lo for lo, _ in halves] + [hi for _, hi in halves], axis=-1).astype(BF16)
        a = (jax.nn.silu(_dot(x, wg_scr[...])) * _dot(x, wu_scr[...])).astype(BF16)
        _store_rows(ys_ref, _pack_row(_dot(a, wd_scr[...])))


def _experts(layer, blk_e, n_used, first, next_e, xs, wg, wu, wd):
    blk = (EXPERT_BLOCK * ROW_SUB, LANES)
    nb = xs.shape[0] // blk[0]
    d, de = wg.shape[-2], wg.shape[-1]
    last_used = lambda i, nu: jnp.minimum(i, nu[0] - 1)
    hbm = pl.BlockSpec(memory_space=pl.ANY)
    grid_spec = pltpu.PrefetchScalarGridSpec(
        num_scalar_prefetch=4, grid=(nb,),
        in_specs=[pl.BlockSpec(blk, lambda i, be, nu, fi, ne: (last_used(i, nu), 0)), hbm, hbm, hbm],
        out_specs=pl.BlockSpec(blk, lambda i, be, nu, fi, ne: (last_used(i, nu), 0)),
        scratch_shapes=[pltpu.VMEM((2, d, de), F32), pltpu.VMEM((2, d, de), F32), pltpu.VMEM((2, de, d), F32),
                        pltpu.VMEM((d, de), BF16), pltpu.VMEM((d, de), BF16), pltpu.VMEM((de, d), BF16),
                        pltpu.SMEM((1,), I32), pltpu.SemaphoreType.DMA((2,))])
    return pl.pallas_call(
        functools.partial(_experts_body, layer), grid_spec=grid_spec,
        out_shape=jax.ShapeDtypeStruct(xs.shape, xs.dtype),
        compiler_params=_params("arbitrary"), name="experts",
    )(blk_e, n_used, first, next_e, xs, wg, wu, wd)


def _combine_body(dest_ref, dest_next_ref, g8_ref, hs_ref, ys_hbm, o_ref, buf, sems):
    n_tok = hs_ref.shape[0]
    n_grp = n_tok // ISSUE_GROUP
    step = pl.program_id(0)
    cur = step % 2
    half_rows = TOP_K * n_tok

    def for_group(idx_ref, slot, op):
        base = slot * (half_rows * ROW_SUB)

        def row_copy(src_row, k, gi, g):
            src = pl.ds(pl.multiple_of(src_row * ROW_SUB, ROW_SUB), ROW_SUB)
            dst = base + (k * n_grp + gi) * (ISSUE_GROUP * ROW_SUB) + g * ROW_SUB
            return pltpu.make_async_copy(ys_hbm.at[src, :], buf.at[pl.ds(pl.multiple_of(dst, ROW_SUB), ROW_SUB), :],
                                         sems.at[slot])

        def body(gi, carry):
            p0 = gi * (ISSUE_GROUP * TOP_K)
            for g in range(ISSUE_GROUP):
                for k in range(TOP_K):
                    op(row_copy(idx_ref[p0 + g * TOP_K + k], k, gi, g), k)
            return carry
        lax.fori_loop(0, n_grp, body, 0)

    start = lambda cp, k: cp.start(priority=k % 2)

    @pl.when(step == 0)
    def _():
        for_group(dest_ref, 0, start)

    @pl.when(step + 1 < pl.num_programs(0))
    def _():
        for_group(dest_next_ref, 1 - cur, start)

    for_group(dest_ref, cur, lambda cp, k: cp.wait())

    half = ROW_SUB * LANES

    def reduce_group(r0):
        rows = pl.ds(r0, ISSUE_GROUP)
        gates = [g8_ref[rows, k:k + 1] for k in range(TOP_K)]
        picked = [_load_rows(buf, cur * half_rows + k * n_tok + r0, ISSUE_GROUP) for k in range(TOP_K)]
        for c in range(ROW_SUB):
            lo_cols = slice(c * LANES, (c + 1) * LANES)
            hi_cols = slice(half + c * LANES, half + (c + 1) * LANES)
            acc_lo = hs_ref[rows, lo_cols]
            acc_hi = hs_ref[rows, hi_cols]
            for k in range(TOP_K):
                lo, hi = _unpack_row(picked[k][c])
                acc_lo = acc_lo + gates[k] * lo
                acc_hi = acc_hi + gates[k] * hi
            o_ref[rows, lo_cols] = acc_lo
            o_ref[rows, hi_cols] = acc_hi

    def reduce(gi, carry):
        r0 = pl.multiple_of(gi * (2 * ISSUE_GROUP), 2 * ISSUE_GROUP)
        reduce_group(r0)
        reduce_group(r0 + ISSUE_GROUP)
        return carry

    lax.fori_loop(0, n_grp // 2, reduce, 0)


def _combine(dest_flat, g8, hs, ys):
    t, d = hs.shape
    tt = min(COMBINE_TILE, t)
    n_steps = t // tt
    return pl.pallas_call(
        _combine_body, grid=(n_steps,),
        in_specs=[pl.BlockSpec((tt * TOP_K,), lambda i: (i,), memory_space=pltpu.SMEM),
                  pl.BlockSpec((tt * TOP_K,), lambda i: (jnp.minimum(i + 1, n_steps - 1),), memory_space=pltpu.SMEM),
                  pl.BlockSpec((tt, g8.shape[1]), lambda i: (i, 0)),
                  pl.BlockSpec((tt, d), lambda i: (i, 0)),
                  pl.BlockSpec(memory_space=pl.ANY)],
        out_specs=pl.BlockSpec((tt, d), lambda i: (i, 0)),
        out_shape=jax.ShapeDtypeStruct((t, d), F32),
        scratch_shapes=[pltpu.VMEM((2 * TOP_K * tt * ROW_SUB, LANES), ys.dtype), pltpu.SemaphoreType.DMA((2,))],
        compiler_params=_params("arbitrary"), name="combine",
    )(dest_flat, dest_flat, g8, hs, ys)


def _ple_body(final, h2_ref, p_ref, win_ref, wgate_ref, g_ref, gf_ref, o_ref):
    h2 = h2_ref[...]
    e = _dot(p_ref[...].astype(BF16), win_ref[...])
    gate = jax.nn.sigmoid(_dot(h2.astype(BF16), wgate_ref[...]))
    h3 = h2 + _rms(gate * e, g_ref[...])
    o_ref[...] = _rms(h3, gf_ref[...]) if final else h3


def _ple(h2, p, w_in, w_gate, g, g_final, final):
    t, d = h2.shape
    tm = min(TOKEN_TILE, t)
    row = lambda n: pl.BlockSpec((tm, n), lambda i: (i, 0))
    return pl.pallas_call(
        functools.partial(_ple_body, final), grid=(t // tm,),
        in_specs=[row(d), row(p.shape[1]), _full(w_in.shape), _full(w_gate.shape), _full((1, d)), _full((1, d))],
        out_specs=row(d), out_shape=jax.ShapeDtypeStruct((t, d), F32),
        compiler_params=_params("parallel"), name="ple",
    )(h2, p, w_in, w_gate, g, g_final)


def _mixers(h, bsz, seq, norm1_g, w_in, four_w, ln_g, ln_b, ws, bs, conv_w, gate_b, norm_g, tables):
    d = h.shape[1]
    n_main = w_in.shape[1] - N_GATES
    w_main = w_in[:, :n_main].astype(BF16)
    w_g = w_in[:, n_main:]
    w_gate = jnp.pad(w_g, ((0, 0), (0, LANES - N_GATES))).astype(BF16)
    w_gate_t = w_g.T.astype(BF16)
    zf, zu, zv, zqk, zmv, zo, zg, zgt = _in_proj(h, norm1_g.reshape(1, d), w_main, w_gate, w_gate_t)
    b3 = lambda a: a.reshape(bsz, seq, a.shape[-1])

    cs, ss, cm, sm = tables
    yf = _fourier(b3(zf), cs, ss, cm, sm, _block_diag(four_w).astype(BF16))

    bs_full = jnp.repeat(bs.T, GMLP_HD, axis=1)
    yg = _gmlp(b3(zu), b3(zv), ln_g.reshape(1, -1), ln_b.reshape(1, -1), ws.astype(BF16), bs_full)

    zqk_pad = jnp.pad(b3(zqk), ((0, 0), (CONV_PAD, CONV_PAD), (0, 0)))
    gate_b_c = jnp.pad(gate_b, (0, LANES - N_GATES)).reshape(1, LANES)
    gate_b_t = jnp.broadcast_to(gate_b.reshape(N_GATES, 1), (N_GATES, LANES))
    ym = _mlstm(zqk_pad, b3(zmv), b3(zo), b3(zg), zgt, conv_w, gate_b_c, gate_b_t, norm_g.reshape(1, -1))
    t = bsz * seq
    return yf.reshape(t, -1), yg.reshape(t, -1), ym.reshape(t, -1)


def _moe(layer, h, yf, yg, ym, w_out, norm2_g, router_w, router_b, wg, wu, wd, sg, su, sd):
    t, d = h.shape
    sgu = jnp.concatenate([sg, su], axis=1).astype(BF16)
    tm = min(TOKEN_TILE, t)
    router_b_t = jnp.broadcast_to(router_b.reshape(-1, 1), (N_EXPERTS, tm))
    hs, xw, sel, gates, idx = _post_mix(h, yf, yg, ym, w_out.astype(BF16), norm2_g.reshape(1, d),
                                        router_w.T, router_b_t, sgu, sd.astype(BF16))
    rank, counts = _rank(sel)
    counts = counts[:, 0]
    pcounts = ((counts + EXPERT_BLOCK - 1) // EXPERT_BLOCK) * EXPERT_BLOCK
    pend = jnp.cumsum(pcounts)
    offs = (pend - pcounts).astype(I32)
    tail_start = jnp.where(pcounts > 0, pend - EXPERT_BLOCK, -1).astype(I32)
    n_blk = (t * TOP_K) // EXPERT_BLOCK + N_EXPERTS
    blk_start = jnp.arange(n_blk, dtype=I32) * EXPERT_BLOCK
    blk_e = jnp.minimum(jnp.sum(pend[None, :] <= blk_start[:, None], axis=1), N_EXPERTS - 1).astype(I32)
    n_used = (pend[-1:] // EXPERT_BLOCK).astype(I32)
    blk_i = jnp.arange(n_blk, dtype=I32)
    first = ((blk_i < n_used[0]) & ((blk_i == 0) | (blk_e != jnp.roll(blk_e, 1)))).astype(I32)
    after = pend[blk_e] // EXPERT_BLOCK
    next_e = jnp.where(after < n_used[0], blk_e[jnp.minimum(after, n_blk - 1)], -1).astype(I32)

    dest, g8 = _dest(rank, idx, gates, jnp.broadcast_to(offs.reshape(-1, 1), (N_EXPERTS, LANES)))
    dest_flat = dest.T.reshape(-1)
    g8 = jnp.pad(g8.T, ((0, 0), (0, LANES - TOP_K)))
    xs = _dispatch(dest_flat, tail_start, xw, n_blk * EXPERT_BLOCK)
    ys = _experts(layer, blk_e, n_used, first, next_e, xs, wg, wu, wd)
    return _combine(dest_flat, g8, hs, ys)


def kernel(x, p, norm1_g, w_in, four_w, gmlp_ln_g, gmlp_ln_b, gmlp_ws, gmlp_bs, mlstm_conv_w,
           mlstm_gate_b, mlstm_norm_g, w_out, norm2_g, router_w, router_b, exp_w_gate, exp_w_up,
           exp_w_down, sh_w_gate, sh_w_up, sh_w_down, ple_w_in, ple_w_gate, ple_norm_g, final_norm_g):
    bsz, seq, d = x.shape
    depth = w_in.shape[0]
    t = bsz * seq
    cos_s, sin_s = _dft_tables(seq)
    cos_m, sin_m = _dft_tables(FOUR_HD)
    eye = np.eye(FOUR_HEADS)
    tables = (jnp.asarray(cos_s, BF16), jnp.asarray(sin_s, BF16),
              jnp.asarray(np.kron(eye, cos_m), BF16), jnp.asarray(np.kron(eye, sin_m), BF16))
    h = x.reshape(t, d)
    for i in range(depth):
        yf, yg, ym = _mixers(h, bsz, seq, norm1_g[i], w_in[i], four_w[i], gmlp_ln_g[i], gmlp_ln_b[i],
                             gmlp_ws[i], gmlp_bs[i], mlstm_conv_w[i], mlstm_gate_b[i], mlstm_norm_g[i], tables)
        h2 = _moe(i, h, yf, yg, ym, w_out[i], norm2_g[i], router_w[i], router_b[i],
                  exp_w_gate, exp_w_up, exp_w_down, sh_w_gate[i], sh_w_up[i], sh_w_down[i])
        h = _ple(h2, p[i].reshape(t, -1), ple_w_in[i].astype(BF16), ple_w_gate[i].astype(BF16),
                 ple_norm_g[i].reshape(1, d), final_norm_g.reshape(1, d), i == depth - 1)
    return h.reshape(bsz, seq, d)
```

```python
import functools
import math

import numpy as np
import jax
import jax.numpy as jnp
from jax import lax
from jax.experimental import pallas as pl
from jax.experimental.pallas import tpu as pltpu

F32, BF16, I32, U32 = jnp.float32, jnp.bfloat16, jnp.int32, jnp.uint32

EPS = 1e-6
LANES = 128
VMEM_LIMIT_BYTES = 48 * 1024 * 1024

FOUR_HEADS, FOUR_HD = 4, 64
GMLP_HEADS, GMLP_HD, GMLP_CHUNK = 4, 64, 128
MLSTM_HEADS, MLSTM_DV, MLSTM_DQK, MLSTM_CHUNK = 4, 128, 64, 128
CONV_PAD = 16
MLSTM_N_ROWS = 8
N_GATES = 16
N_EXPERTS, TOP_K = 128, 8
EXPERT_BLOCK = 512
TOKEN_TILE = 1024
DISPATCH_TILE = 512
COMBINE_TILE = 512
ROUTE_COLS = 128
ISSUE_GROUP = 8
REDUCE_GROUPS = 4


def _params(*sem):
    return pltpu.CompilerParams(dimension_semantics=sem, vmem_limit_bytes=VMEM_LIMIT_BYTES)


def _rms(x, g):
    return x * lax.rsqrt(jnp.mean(x * x, axis=-1, keepdims=True) + EPS) * g


def _full(shape):
    nd = len(shape)
    return pl.BlockSpec(shape, lambda *_: (0,) * nd)


def _dot(a, b):
    return jnp.dot(a, b, preferred_element_type=F32)


def _split3(x):
    hi = x.astype(BF16)
    rest = x - hi.astype(F32)
    mid = rest.astype(BF16)
    return hi, mid, (rest - mid.astype(F32)).astype(BF16)


def _pack_row(x):
    n = x.shape[-1] // 2
    lo = lax.bitcast_convert_type(x[:, :n].astype(BF16).astype(F32), U32)
    hi = lax.bitcast_convert_type(x[:, n:].astype(BF16).astype(F32), U32)
    return hi | (lo >> 16)


def _unpack_row(w):
    lo = lax.bitcast_convert_type(w << 16, F32)
    hi = lax.bitcast_convert_type(w & jnp.uint32(0xFFFF0000), F32)
    return lo, hi


ROW_SUB = 4


def _store_rows(ref, w):
    m = w.shape[0]
    for c in range(ROW_SUB):
        ref[pl.ds(c, m, stride=ROW_SUB), :] = w[:, c * LANES:(c + 1) * LANES]


def _load_rows(ref, r0, m):
    return [ref[pl.ds(r0 * ROW_SUB + c, m, stride=ROW_SUB), :] for c in range(ROW_SUB)]


def _in_proj_body(h_ref, g_ref, w_ref, wg_ref, wgt_ref,
                  zf_ref, zu_ref, zv_ref, zqk_ref, zmv_ref, zo_ref, zg_ref, zgt_ref):
    ab = _rms(h_ref[...], g_ref[...]).astype(BF16)
    off = 0
    for o_ref in (zf_ref, zu_ref, zv_ref, zqk_ref, zmv_ref, zo_ref):
        n = o_ref.shape[-1]
        o_ref[...] = _dot(ab, w_ref[:, off:off + n]).astype(o_ref.dtype)
        off += n
    zg_ref[...] = _dot(ab, wg_ref[...])
    zgt_ref[...] = lax.dot_general(wgt_ref[...], ab, (((1,), (1,)), ((), ())),
                                   preferred_element_type=F32)


def _in_proj(h, g, w_main, w_gate, w_gate_t):
    t, d = h.shape
    tm = min(TOKEN_TILE, t)
    widths = (256, 256, 256, 512, 512, 512)
    row = lambda n: pl.BlockSpec((tm, n), lambda i: (i, 0))
    out_shape = [jax.ShapeDtypeStruct((t, n), BF16) for n in widths]
    out_shape += [jax.ShapeDtypeStruct((t, LANES), F32), jax.ShapeDtypeStruct((N_GATES, t), F32)]
    out_specs = [row(n) for n in widths]
    out_specs += [row(LANES), pl.BlockSpec((N_GATES, tm), lambda i: (0, i))]
    return pl.pallas_call(
        _in_proj_body, grid=(t // tm,),
        in_specs=[row(d), _full((1, d)), _full(w_main.shape), _full(w_gate.shape), _full(w_gate_t.shape)],
        out_specs=out_specs, out_shape=out_shape,
        compiler_params=_params("parallel"), name="in_proj",
    )(h, g, w_main, w_gate, w_gate_t)


def _dft_tables(n):
    k = np.arange(n, dtype=np.int64)
    ang = 2.0 * np.pi * ((k[:, None] * k[None, :]) % n).astype(np.float64) / n
    return np.cos(ang), np.sin(ang)


def _block_diag(blocks):
    h, a, b = blocks.shape
    eye = jnp.eye(h, dtype=blocks.dtype)
    return (eye[:, None, :, None] * blocks[:, :, None, :]).reshape(h * a, h * b)


def _fourier_body(scale, row_tile, z_ref, cs_ref, ss_ref, cm_ref, sm_ref, w_ref, o_ref, p_scr, q_scr):
    z = z_ref[0]
    p_scr[...] = _dot(z, cm_ref[...]).astype(BF16)
    q_scr[...] = _dot(z, sm_ref[...]).astype(BF16)
    s = z.shape[0]
    for r in range(s // row_tile):
        rs = slice(r * row_tile, (r + 1) * row_tile)
        re = (_dot(cs_ref[rs, :], p_scr[...]) - _dot(ss_ref[rs, :], q_scr[...])) * scale
        o_ref[0, rs, :] = _dot(re.astype(BF16), w_ref[...]).astype(o_ref.dtype)


def _fourier(zf, cs, ss, cm, sm, wbd):
    b, s, w = zf.shape
    row_tile = min(512, s)
    scale = 1.0 / math.sqrt(s * FOUR_HD)
    blk = pl.BlockSpec((1, s, w), lambda i: (i, 0, 0))
    return pl.pallas_call(
        functools.partial(_fourier_body, scale, row_tile), grid=(b,),
        in_specs=[blk, _full(cs.shape), _full(ss.shape), _full(cm.shape), _full(sm.shape), _full(wbd.shape)],
        out_specs=blk, out_shape=jax.ShapeDtypeStruct((b, s, w), BF16),
        scratch_shapes=[pltpu.VMEM((s, w), BF16), pltpu.VMEM((s, w), BF16)],
        compiler_params=_params("parallel"), name="fourier",
    )(zf, cs, ss, cm, sm, wbd)


def _gmlp_body(zu_ref, zv_ref, lg_ref, lb_ref, ws_ref, bs_ref, o_ref):
    s = zu_ref.shape[1]
    w = zu_ref.shape[2]
    lane = lax.broadcasted_iota(I32, (GMLP_CHUNK, w), 1)

    def chunk(r0):
        rows = pl.ds(r0, GMLP_CHUNK)
        v = jax.nn.gelu(zv_ref[0, rows, :].astype(F32))
        vc = v - jnp.mean(v, axis=-1, keepdims=True)
        vn = vc * lax.rsqrt(jnp.mean(vc * vc, axis=-1, keepdims=True) + EPS) * lg_ref[...] + lb_ref[...]
        acc = bs_ref[...]
        for h in range(GMLP_HEADS):
            vh = jnp.where((lane >= h * GMLP_HD) & (lane < (h + 1) * GMLP_HD), vn, 0.0).astype(BF16)
            acc = acc + _dot(ws_ref[h], vh)
        u = jax.nn.gelu(zu_ref[0, rows, :].astype(F32))
        o_ref[0, rows, :] = (u * acc).astype(o_ref.dtype)

    def pair(c, carry):
        r0 = pl.multiple_of(c * (2 * GMLP_CHUNK), 2 * GMLP_CHUNK)
        chunk(r0)
        chunk(r0 + GMLP_CHUNK)
        return carry

    lax.fori_loop(0, s // (2 * GMLP_CHUNK), pair, 0)


def _gmlp(zu, zv, ln_g, ln_b, ws, bs_full):
    b, s, w = zu.shape
    blk = pl.BlockSpec((1, s, w), lambda i: (i, 0, 0))
    return pl.pallas_call(
        _gmlp_body, grid=(b,),
        in_specs=[blk, blk, _full((1, w)), _full((1, w)), _full(ws.shape), _full(bs_full.shape)],
        out_specs=blk, out_shape=jax.ShapeDtypeStruct((b, s, w), BF16),
        compiler_params=_params("parallel"), name="gmlp",
    )(zu, zv, ln_g, ln_b, ws, bs_full)


def _mlstm_body(qkp_ref, v_ref, zo_ref, g_ref, gt_ref, cw_ref, gb_ref, gbt_ref, ng_ref, o_ref,
                qm_scr, km_scr, vt_scr, hf_scr, hb_scr, c_scr, m_scr, gc_scr, gr_scr, cumc_scr, cumr_scr):
    L = MLSTM_CHUNK
    H = MLSTM_HEADS
    DV = MLSTM_DV
    N_ROWS = MLSTM_N_ROWS
    s = v_ref.shape[1]
    nc = s // L
    qkw = H * MLSTM_DQK
    nt = (((1,), (1,)), ((), ()))

    ext = L + 2 * CONV_PAD
    r_i = lax.broadcasted_iota(I32, (L, ext), 0)
    c_i = lax.broadcasted_iota(I32, (L, ext), 1)
    sh_m1 = jnp.where(c_i == r_i + CONV_PAD - 1, 1.0, 0.0).astype(BF16)
    sh_0 = jnp.where(c_i == r_i + CONV_PAD, 1.0, 0.0).astype(BF16)
    sh_p1 = jnp.where(c_i == r_i + CONV_PAD + 1, 1.0, 0.0).astype(BF16)
    lane = lax.broadcasted_iota(I32, (L, LANES), 1)
    low_half = lane < MLSTM_DQK

    def conv_one(r0):
        rows = pl.ds(r0, L)
        xe = qkp_ref[0, pl.ds(r0, ext), :]
        y = (cw_ref[0:1, :] * _dot(sh_m1, xe) + cw_ref[1:2, :] * _dot(sh_0, xe)
             + cw_ref[2:3, :] * _dot(sh_p1, xe))
        qk = y * jax.nn.sigmoid(y)
        for h in range(H):
            keep = low_half if h % 2 == 0 else jnp.logical_not(low_half)
            t0 = (h // 2) * LANES
            q_t = qk[:, t0:t0 + LANES] * (MLSTM_DQK ** -0.5)
            k_t = qk[:, qkw + t0:qkw + t0 + LANES]
            qm_scr[h, rows, :] = jnp.where(keep, q_t, 0.0).astype(BF16)
            km_scr[h, rows, :] = jnp.where(keep, k_t, 0.0).astype(BF16)
            vt_scr[h * DV:(h + 1) * DV, rows] = v_ref[0, rows, h * DV:(h + 1) * DV].astype(F32).T.astype(BF16)
        gc = g_ref[0, rows, :] + gb_ref[...]
        gr = gt_ref[:, rows] + gbt_ref[...]
        gc_scr[rows, :] = gc
        gr_scr[:, rows] = gr
        lf_c = jax.nn.log_sigmoid(gc)
        lf_r = jax.nn.log_sigmoid(gr)
        parts_c = _split3(lf_c)
        parts_r = _split3(lf_r)
        cumc_scr[0, rows, :] = sum(_dot(tri_l, p) for p in parts_c)
        cumc_scr[1, rows, :] = sum(_dot(tri_u, p) for p in parts_c)
        cumr_scr[0, :, rows] = sum(_dot(p, tri_u) for p in parts_r)
        cumr_scr[1, :, rows] = sum(_dot(p, tri_l) for p in parts_r)

    def conv_chunk(c, carry):
        r0 = pl.multiple_of(c * (2 * L), 2 * L)
        conv_one(r0)
        conv_one(r0 + L)
        return carry

    ri = lax.broadcasted_iota(I32, (L, L), 0)
    ci = lax.broadcasted_iota(I32, (L, L), 1)
    tri_l = jnp.where(ci <= ri, 1.0, 0.0).astype(BF16)
    tri_u = jnp.where(ci >= ri, 1.0, 0.0).astype(BF16)
    lax.fori_loop(0, nc // 2, conv_chunk, 0)

    c_scr[...] = jnp.zeros(c_scr.shape, F32)
    m_scr[...] = jnp.zeros(m_scr.shape, F32)

    def direction(d, chunk):
        r0 = pl.multiple_of(chunk * L, L)
        rows = pl.ds(r0, L)
        gc = gc_scr[rows, :]
        gr = gr_scr[:, rows]
        cum_c = cumc_scr[d, rows, :]
        cum_r = cumr_scr[d, :, rows]
        mask = (ci >= ri) if d == 0 else (ci <= ri)
        i_lane = 2 * d * H
        f_lane = (2 * d + 1) * H
        for h in range(H):
            u = d * H + h
            b_r = cum_r[f_lane + h:f_lane + h + 1, :]
            li_r = gr[i_lane + h:i_lane + h + 1, :]
            key_c = gc[:, i_lane + h:i_lane + h + 1] - cum_c[:, f_lane + h:f_lane + h + 1]
            g_edge = b_r[:, L - 1:L] if d == 0 else b_r[:, 0:1]
            g_tot = jnp.broadcast_to(g_edge, (1, L))
            qh = qm_scr[h, rows, :]
            kh = km_scr[h, rows, :]
            vt = vt_scr[h * DV:(h + 1) * DV, rows]
            cn_prev = c_scr[u]
            m_prev = m_scr[u]
            dmat = jnp.where(mask, b_r + key_c, -jnp.inf)
            m_intra = jnp.max(dmat, axis=0, keepdims=True)
            m_inter = b_r + m_prev
            m_tot = jnp.maximum(m_intra, m_inter)
            kq = lax.dot_general(kh, qh, nt, preferred_element_type=F32)
            s_mat = jnp.exp(dmat - m_tot) * kq
            inter = jnp.exp(m_inter - m_tot)
            state_q = lax.dot_general(cn_prev.astype(BF16), qh, nt, preferred_element_type=F32)
            den = jnp.sum(s_mat, axis=0, keepdims=True) + inter * state_q[DV:DV + 1, :]
            num = _dot(vt, s_mat.astype(BF16)) + inter * state_q[0:DV, :]
            dst = hf_scr if d == 0 else hb_scr
            dst[h * DV:(h + 1) * DV, rows] = num / jnp.maximum(jnp.abs(den), jnp.exp(-m_tot))
            a_r = g_tot - b_r + li_r
            m_loc = jnp.broadcast_to(jnp.max(a_r, axis=-1, keepdims=True), (1, L))
            w_r = jnp.exp(a_r - m_loc)
            weighted = jnp.concatenate([vt.astype(F32) * w_r, jnp.broadcast_to(w_r, (N_ROWS, L))], axis=0)
            cn_loc = _dot(weighted.astype(BF16), kh)
            m_new = jnp.maximum(g_tot + m_prev, m_loc)
            a_old = jnp.exp(g_tot + m_prev - m_new)
            a_new = jnp.exp(m_loc - m_new)
            c_scr[u] = a_old * cn_prev + a_new * cn_loc
            m_scr[u] = m_new

    def step(i, carry):
        direction(0, i)
        direction(1, nc - 1 - i)
        return carry

    lax.fori_loop(0, nc, step, 0)

    def finish(c, carry):
        r0 = pl.multiple_of(c * L, L)
        rows = pl.ds(r0, L)
        og = jax.nn.sigmoid(zo_ref[0, rows, :].astype(F32))
        for h in range(H):
            cols = slice(h * DV, (h + 1) * DV)
            x = og[:, cols] * (hf_scr[cols, rows] + hb_scr[cols, rows]).T
            xc = x - jnp.mean(x, axis=-1, keepdims=True)
            y = xc * lax.rsqrt(jnp.mean(xc * xc, axis=-1, keepdims=True) + EPS) * ng_ref[:, cols]
            o_ref[0, rows, cols] = y.astype(o_ref.dtype)
        return carry

    lax.fori_loop(0, nc, finish, 0)


def _mlstm(zqk_pad, zmv, zo, zg, zgt, conv_w, gate_b, gate_b_t, norm_g):
    b, s, w = zmv.shape
    H = MLSTM_HEADS
    blk = pl.BlockSpec((1, s, w), lambda i: (i, 0, 0))
    units = 2 * H
    return pl.pallas_call(
        _mlstm_body, grid=(b,),
        in_specs=[pl.BlockSpec((1, s + 2 * CONV_PAD, w), lambda i: (i, 0, 0)), blk, blk,
                  pl.BlockSpec((1, s, LANES), lambda i: (i, 0, 0)),
                  pl.BlockSpec((N_GATES, s), lambda i: (0, i)),
                  _full(conv_w.shape), _full(gate_b.shape), _full(gate_b_t.shape), _full(norm_g.shape)],
        out_specs=blk, out_shape=jax.ShapeDtypeStruct((b, s, w), BF16),
        scratch_shapes=[pltpu.VMEM((H, s, LANES), BF16), pltpu.VMEM((H, s, LANES), BF16),
                        pltpu.VMEM((w, s), BF16), pltpu.VMEM((w, s), F32), pltpu.VMEM((w, s), F32),
                        pltpu.VMEM((units, MLSTM_DV + MLSTM_N_ROWS, LANES), F32), pltpu.VMEM((units, 1, LANES), F32),
                        pltpu.VMEM((s, LANES), F32), pltpu.VMEM((N_GATES, s), F32),
                        pltpu.VMEM((2, s, LANES), F32), pltpu.VMEM((2, N_GATES, s), F32)],
        compiler_params=_params("parallel"), name="mlstm",
    )(zqk_pad, zmv, zo, zg, zgt, conv_w, gate_b, gate_b_t, norm_g)


def _post_mix_body(h_ref, yf_ref, yg_ref, ym_ref, wo_ref, g2_ref, rwh_ref, rwl_ref, rbt_ref, sgu_ref, sd_ref,
                   hs_ref, xw_ref, sel_ref, gate_ref, idx_ref, sc_scr):
    wf, wg = yf_ref.shape[-1], yg_ref.shape[-1]
    h1 = (h_ref[...] + _dot(yf_ref[...], wo_ref[0:wf, :]) + _dot(yg_ref[...], wo_ref[wf:wf + wg, :])
          + _dot(ym_ref[...], wo_ref[wf + wg:, :]))
    xn = _rms(h1, g2_ref[...])
    xb = xn.astype(BF16)
    _store_rows(xw_ref, _pack_row(xn))
    gu = _dot(xb, sgu_ref[...])
    de = gu.shape[-1] // 2
    act = (jax.nn.silu(gu[:, :de]) * gu[:, de:]).astype(BF16)
    hs_ref[...] = h1 + _dot(act, sd_ref[...])
    nt = (((1,), (1,)), ((), ()))
    x_lo = (xn - xb.astype(F32)).astype(BF16)
    logits = (lax.dot_general(rwh_ref[...], xb, nt, preferred_element_type=F32)
              + lax.dot_general(rwh_ref[...], x_lo, nt, preferred_element_type=F32)
              + lax.dot_general(rwl_ref[...], xb, nt, preferred_element_type=F32))
    sc_scr[...] = jax.nn.sigmoid(logits)
    for c0 in range(0, sc_scr.shape[1], ROUTE_COLS):
        cols = slice(c0, c0 + ROUTE_COLS)
        scores = sc_scr[:, cols]
        work = scores + rbt_ref[:, cols]
        expert = lax.broadcasted_iota(I32, scores.shape, 0)
        choice = lax.broadcasted_iota(I32, (TOP_K, ROUTE_COLS), 0)
        sel = jnp.zeros(scores.shape, F32)
        idx = jnp.zeros(choice.shape, I32)
        for k in range(TOP_K):
            m = jnp.max(work, axis=0, keepdims=True)
            e = jnp.min(jnp.where(work == m, expert, N_EXPERTS), axis=0, keepdims=True)
            hit = expert == e
            sel = jnp.where(hit, 1.0, sel)
            idx = jnp.where(choice == k, e, idx)
            work = jnp.where(hit, -jnp.inf, work)
        picked = sel * scores
        sel_ref[:, cols] = sel
        gate_ref[:, cols] = picked / jnp.sum(picked, axis=0, keepdims=True)
        idx_ref[:, cols] = idx


def _post_mix(h, yf, yg, ym, w_out, g2, router_w_t, router_b_t, sgu, sd):
    t, d = h.shape
    tm = min(TOKEN_TILE, t)
    row = lambda n: pl.BlockSpec((tm, n), lambda i: (i, 0))
    col = lambda n: pl.BlockSpec((n, tm), lambda i: (0, i))
    rw_hi = router_w_t.astype(BF16)
    rw_lo = (router_w_t - rw_hi.astype(F32)).astype(BF16)
    return pl.pallas_call(
        _post_mix_body, grid=(t // tm,),
        in_specs=[row(d), row(yf.shape[1]), row(yg.shape[1]), row(ym.shape[1]), _full(w_out.shape),
                  _full((1, d)), _full(rw_hi.shape), _full(rw_lo.shape), _full(router_b_t.shape),
                  _full(sgu.shape), _full(sd.shape)],
        out_specs=[row(d), pl.BlockSpec((tm * ROW_SUB, LANES), lambda i: (i, 0)),
                   col(N_EXPERTS), col(N_EXPERTS), col(TOP_K)],
        out_shape=[jax.ShapeDtypeStruct((t, d), F32), jax.ShapeDtypeStruct((t * ROW_SUB, LANES), U32),
                   jax.ShapeDtypeStruct((N_EXPERTS, t), F32), jax.ShapeDtypeStruct((N_EXPERTS, t), F32),
                   jax.ShapeDtypeStruct((TOP_K, t), I32)],
        scratch_shapes=[pltpu.VMEM((N_EXPERTS, tm), F32)],
        compiler_params=_params("parallel"), name="post_mix",
    )(h, yf, yg, ym, w_out, g2, rw_hi, rw_lo, router_b_t, sgu, sd)


def _rank_body(sel_ref, rank_ref, cnt_ref, carry):
    @pl.when(pl.program_id(0) == 0)
    def _():
        carry[...] = jnp.zeros(carry.shape, F32)

    m = sel_ref[...].astype(BF16)
    tm = m.shape[1]
    ri = lax.broadcasted_iota(I32, (tm, tm), 0)
    ci = lax.broadcasted_iota(I32, (tm, tm), 1)
    earlier = jnp.where(ri < ci, 1.0, 0.0).astype(BF16)
    seen = carry[...]
    rank = _dot(m, earlier) + jnp.concatenate([seen] * (tm // LANES), axis=1)
    rank_ref[...] = rank.astype(I32)
    seen = seen + _dot(m, jnp.ones((tm, LANES), BF16))
    carry[...] = seen
    cnt_ref[...] = seen.astype(I32)


def _rank(sel_t):
    e, t = sel_t.shape
    tm = min(TOKEN_TILE, t)
    return pl.pallas_call(
        _rank_body, grid=(t // tm,),
        in_specs=[pl.BlockSpec((e, tm), lambda i: (0, i))],
        out_specs=[pl.BlockSpec((e, tm), lambda i: (0, i)), _full((e, LANES))],
        out_shape=[jax.ShapeDtypeStruct((e, t), I32), jax.ShapeDtypeStruct((e, LANES), I32)],
        scratch_shapes=[pltpu.VMEM((e, LANES), F32)],
        compiler_params=_params("arbitrary"), name="rank",
    )(sel_t)


def _dest_body(rank_ref, idx_ref, gate_ref, offs_ref, dest_ref, g8_ref):
    tm = rank_ref.shape[1]
    offs = jnp.concatenate([offs_ref[...]] * (tm // LANES), axis=1)
    pos = (rank_ref[...] + offs).astype(F32)
    gates = gate_ref[...]
    expert = lax.broadcasted_iota(I32, pos.shape, 0)
    choice = lax.broadcasted_iota(I32, (TOP_K, tm), 0)
    dest = jnp.zeros(choice.shape, F32)
    g8 = jnp.zeros(choice.shape, F32)
    for k in range(TOP_K):
        hit = expert == idx_ref[k:k + 1, :]
        dk = jnp.sum(jnp.where(hit, pos, 0.0), axis=0, keepdims=True)
        gk = jnp.sum(jnp.where(hit, gates, 0.0), axis=0, keepdims=True)
        dest = jnp.where(choice == k, dk, dest)
        g8 = jnp.where(choice == k, gk, g8)
    dest_ref[...] = dest.astype(I32)
    g8_ref[...] = g8


def _dest(rank_t, idx_t, gates_t, offs_rep):
    e, t = rank_t.shape
    tm = min(TOKEN_TILE, t)
    col = lambda n: pl.BlockSpec((n, tm), lambda i: (0, i))
    return pl.pallas_call(
        _dest_body, grid=(t // tm,),
        in_specs=[col(e), col(TOP_K), col(e), _full((e, LANES))],
        out_specs=[col(TOP_K), col(TOP_K)],
        out_shape=[jax.ShapeDtypeStruct((TOP_K, t), I32), jax.ShapeDtypeStruct((TOP_K, t), F32)],
        compiler_params=_params("parallel"), name="dest",
    )(rank_t, idx_t, gates_t, offs_rep)


def _dispatch_body(dest_ref, tail_ref, x_ref, xs_hbm, zero_scr, sem):
    n_grp = x_ref.shape[0]

    @pl.when(pl.program_id(0) == 0)
    def _():
        zero_scr[...] = jnp.zeros(zero_scr.shape, zero_scr.dtype)

        def tail_copy(e):
            start = pl.multiple_of(jnp.maximum(tail_ref[e], 0) * ROW_SUB, EXPERT_BLOCK * ROW_SUB)
            return pltpu.make_async_copy(zero_scr, xs_hbm.at[pl.ds(start, EXPERT_BLOCK * ROW_SUB), :], sem)

        def clear(e, carry):
            @pl.when(tail_ref[e] >= 0)
            def _():
                tail_copy(e).start()
            return carry

        def clear_done(e, carry):
            @pl.when(tail_ref[e] >= 0)
            def _():
                tail_copy(e).wait()
            return carry

        lax.fori_loop(0, N_EXPERTS, clear, 0)
        lax.fori_loop(0, N_EXPERTS, clear_done, 0)

    def row_copy(group, g, dst_row):
        dst = pl.ds(pl.multiple_of(dst_row * ROW_SUB, ROW_SUB), ROW_SUB)
        return pltpu.make_async_copy(x_ref.at[group, g], xs_hbm.at[dst, :], sem)

    def for_group(op):
        def body(gi, carry):
            p0 = gi * (ISSUE_GROUP * TOP_K)
            for g in range(ISSUE_GROUP):
                for k in range(TOP_K):
                    op(row_copy(gi, g, dest_ref[p0 + g * TOP_K + k]), k)
            return carry
        lax.fori_loop(0, n_grp, body, 0)

    for_group(lambda cp, k: cp.start(priority=k % 2))
    for_group(lambda cp, k: cp.wait())


def _dispatch(dest_flat, tail_start, xw, n_rows):
    t = xw.shape[0] // ROW_SUB
    tt = min(DISPATCH_TILE, t)
    x4 = xw.reshape(t // ISSUE_GROUP, ISSUE_GROUP, ROW_SUB, LANES)
    return pl.pallas_call(
        _dispatch_body, grid=(t // tt,),
        in_specs=[pl.BlockSpec((tt * TOP_K,), lambda i: (i,), memory_space=pltpu.SMEM),
                  pl.BlockSpec(memory_space=pltpu.SMEM),
                  pl.BlockSpec((tt // ISSUE_GROUP, ISSUE_GROUP, ROW_SUB, LANES), lambda i: (i, 0, 0, 0))],
        out_specs=pl.BlockSpec(memory_space=pl.ANY),
        out_shape=jax.ShapeDtypeStruct((n_rows * ROW_SUB, LANES), xw.dtype),
        scratch_shapes=[pltpu.VMEM((EXPERT_BLOCK * ROW_SUB, LANES), xw.dtype), pltpu.SemaphoreType.DMA(())],
        compiler_params=_params("arbitrary"), name="dispatch",
    )(dest_flat, tail_start, x4)


def _experts_body(layer, blk_e_ref, n_used_ref, first_ref, next_e_ref, xs_ref, wg_hbm, wu_hbm, wd_hbm, ys_ref,
                  wg_buf, wu_buf, wd_buf, wg_scr, wu_scr, wd_scr, slot_ref, sems):
    i = pl.program_id(0)
    used = i < n_used_ref[0]

    def weight_copies(e, slot):
        return [pltpu.make_async_copy(hbm.at[layer, e], buf.at[slot], sems.at[slot])
                for hbm, buf in ((wg_hbm, wg_buf), (wu_hbm, wu_buf), (wd_hbm, wd_buf))]

    @pl.when(i == 0)
    def _():
        slot_ref[0] = 0
        for cp in weight_copies(blk_e_ref[0], 0):
            cp.start()

    @pl.when(first_ref[i] == 1)
    def _():
        slot = slot_ref[0]
        for cp in weight_copies(blk_e_ref[i], slot):
            cp.wait()
        wg_scr[...] = wg_buf[slot].astype(BF16)
        wu_scr[...] = wu_buf[slot].astype(BF16)
        wd_scr[...] = wd_buf[slot].astype(BF16)

        @pl.when(next_e_ref[i] >= 0)
        def _():
            for cp in weight_copies(next_e_ref[i], 1 - slot):
                cp.start(priority=1)

        slot_ref[0] = 1 - slot

    @pl.when(used)
    def _():
        halves = [_unpack_row(w) for w in _load_rows(xs_ref, 0, EXPERT_BLOCK)]
        x = jnp.concatenate([lo for lo, _ in halves] + [hi for _, hi in halves], axis=-1).astype(BF16)
        a = (jax.nn.silu(_dot(x, wg_scr[...])) * _dot(x, wu_scr[...])).astype(BF16)
        _store_rows(ys_ref, _pack_row(_dot(a, wd_scr[...])))


def _experts(layer, blk_e, n_used, first, next_e, xs, wg, wu, wd):
    blk = (EXPERT_BLOCK * ROW_SUB, LANES)
    nb = xs.shape[0] // blk[0]
    d, de = wg.shape[-2], wg.shape[-1]
    last_used = lambda i, nu: jnp.minimum(i, nu[0] - 1)
    hbm = pl.BlockSpec(memory_space=pl.ANY)
    grid_spec = pltpu.PrefetchScalarGridSpec(
        num_scalar_prefetch=4, grid=(nb,),
        in_specs=[pl.BlockSpec(blk, lambda i, be, nu, fi, ne: (last_used(i, nu), 0)), hbm, hbm, hbm],
        out_specs=pl.BlockSpec(blk, lambda i, be, nu, fi, ne: (last_used(i, nu), 0)),
        scratch_shapes=[pltpu.VMEM((2, d, de), F32), pltpu.VMEM((2, d, de), F32), pltpu.VMEM((2, de, d), F32),
                        pltpu.VMEM((d, de), BF16), pltpu.VMEM((d, de), BF16), pltpu.VMEM((de, d), BF16),
                        pltpu.SMEM((1,), I32), pltpu.SemaphoreType.DMA((2,))])
    return pl.pallas_call(
        functools.partial(_experts_body, layer), grid_spec=grid_spec,
        out_shape=jax.ShapeDtypeStruct(xs.shape, xs.dtype),
        compiler_params=_params("arbitrary"), name="experts",
    )(blk_e, n_used, first, next_e, xs, wg, wu, wd)


def _combine_body(dest_ref, dest_next_ref, g8_ref, hs_ref, ys_hbm, o_ref, buf, sems):
    n_tok = hs_ref.shape[0]
    n_grp = n_tok // ISSUE_GROUP
    step = pl.program_id(0)
    cur = step % 2
    half_rows = TOP_K * n_tok

    def for_group(idx_ref, slot, op):
        base = slot * (half_rows * ROW_SUB)

        def row_copy(src_row, k, gi, g):
            src = pl.ds(pl.multiple_of(src_row * ROW_SUB, ROW_SUB), ROW_SUB)
            dst = base + (k * n_grp + gi) * (ISSUE_GROUP * ROW_SUB) + g * ROW_SUB
            return pltpu.make_async_copy(ys_hbm.at[src, :], buf.at[pl.ds(pl.multiple_of(dst, ROW_SUB), ROW_SUB), :],
                                         sems.at[slot])

        def body(gi, carry):
            p0 = gi * (ISSUE_GROUP * TOP_K)
            for g in range(ISSUE_GROUP):
                for k in range(TOP_K):
                    op(row_copy(idx_ref[p0 + g * TOP_K + k], k, gi, g), k)
            return carry
        lax.fori_loop(0, n_grp, body, 0)

    start = lambda cp, k: cp.start(priority=k % 2)

    @pl.when(step == 0)
    def _():
        for_group(dest_ref, 0, start)

    @pl.when(step + 1 < pl.num_programs(0))
    def _():
        for_group(dest_next_ref, 1 - cur, start)

    for_group(dest_ref, cur, lambda cp, k: cp.wait())

    half = ROW_SUB * LANES

    def reduce_group(r0):
        rows = pl.ds(r0, ISSUE_GROUP)
        gates = [g8_ref[rows, k:k + 1] for k in range(TOP_K)]
        picked = [_load_rows(buf, cur * half_rows + k * n_tok + r0, ISSUE_GROUP) for k in range(TOP_K)]
        for c in range(ROW_SUB):
            lo_cols = slice(c * LANES, (c + 1) * LANES)
            hi_cols = slice(half + c * LANES, half + (c + 1) * LANES)
            acc_lo = hs_ref[rows, lo_cols]
            acc_hi = hs_ref[rows, hi_cols]
            for k in range(TOP_K):
                lo, hi = _unpack_row(picked[k][c])
                acc_lo = acc_lo + gates[k] * lo
                acc_hi = acc_hi + gates[k] * hi
            o_ref[rows, lo_cols] = acc_lo
            o_ref[rows, hi_cols] = acc_hi

    def reduce(gi, carry):
        r0 = pl.multiple_of(gi * (REDUCE_GROUPS * ISSUE_GROUP), REDUCE_GROUPS * ISSUE_GROUP)
        for q in range(REDUCE_GROUPS):
            reduce_group(r0 + q * ISSUE_GROUP)
        return carry

    lax.fori_loop(0, n_grp // REDUCE_GROUPS, reduce, 0)


def _combine(dest_flat, g8, hs, ys):
    t, d = hs.shape
    tt = min(COMBINE_TILE, t)
    n_steps = t // tt
    return pl.pallas_call(
        _combine_body, grid=(n_steps,),
        in_specs=[pl.BlockSpec((tt * TOP_K,), lambda i: (i,), memory_space=pltpu.SMEM),
                  pl.BlockSpec((tt * TOP_K,), lambda i: (jnp.minimum(i + 1, n_steps - 1),), memory_space=pltpu.SMEM),
                  pl.BlockSpec((tt, g8.shape[1]), lambda i: (i, 0)),
                  pl.BlockSpec((tt, d), lambda i: (i, 0)),
                  pl.BlockSpec(memory_space=pl.ANY)],
        out_specs=pl.BlockSpec((tt, d), lambda i: (i, 0)),
        out_shape=jax.ShapeDtypeStruct((t, d), F32),
        scratch_shapes=[pltpu.VMEM((2 * TOP_K * tt * ROW_SUB, LANES), ys.dtype), pltpu.SemaphoreType.DMA((2,))],
        compiler_params=_params("arbitrary"), name="combine",
    )(dest_flat, dest_flat, g8, hs, ys)


def _ple_body(final, h2_ref, p_ref, win_ref, wgate_ref, g_ref, gf_ref, o_ref):
    h2 = h2_ref[...]
    e = _dot(p_ref[...].astype(BF16), win_ref[...])
    gate = jax.nn.sigmoid(_dot(h2.astype(BF16), wgate_ref[...]))
    h3 = h2 + _rms(gate * e, g_ref[...])
    o_ref[...] = _rms(h3, gf_ref[...]) if final else h3


def _ple(h2, p, w_in, w_gate, g, g_final, final):
    t, d = h2.shape
    tm = min(TOKEN_TILE, t)
    row = lambda n: pl.BlockSpec((tm, n), lambda i: (i, 0))
    return pl.pallas_call(
        functools.partial(_ple_body, final), grid=(t // tm,),
        in_specs=[row(d), row(p.shape[1]), _full(w_in.shape), _full(w_gate.shape), _full((1, d)), _full((1, d))],
        out_specs=row(d), out_shape=jax.ShapeDtypeStruct((t, d), F32),
        compiler_params=_params("parallel"), name="ple",
    )(h2, p, w_in, w_gate, g, g_final)


def _mixers(h, bsz, seq, norm1_g, w_in, four_w, ln_g, ln_b, ws, bs, conv_w, gate_b, norm_g, tables):
    d = h.shape[1]
    n_main = w_in.shape[1] - N_GATES
    w_main = w_in[:, :n_main].astype(BF16)
    w_g = w_in[:, n_main:]
    w_gate = jnp.pad(w_g, ((0, 0), (0, LANES - N_GATES))).astype(BF16)
    w_gate_t = w_g.T.astype(BF16)
    zf, zu, zv, zqk, zmv, zo, zg, zgt = _in_proj(h, norm1_g.reshape(1, d), w_main, w_gate, w_gate_t)
    b3 = lambda a: a.reshape(bsz, seq, a.shape[-1])

    cs, ss, cm, sm = tables
    yf = _fourier(b3(zf), cs, ss, cm, sm, _block_diag(four_w).astype(BF16))

    bs_full = jnp.repeat(bs.T, GMLP_HD, axis=1)
    yg = _gmlp(b3(zu), b3(zv), ln_g.reshape(1, -1), ln_b.reshape(1, -1), ws.astype(BF16), bs_full)

    zqk_pad = jnp.pad(b3(zqk), ((0, 0), (CONV_PAD, CONV_PAD), (0, 0)))
    gate_b_c = jnp.pad(gate_b, (0, LANES - N_GATES)).reshape(1, LANES)
    gate_b_t = jnp.broadcast_to(gate_b.reshape(N_GATES, 1), (N_GATES, LANES))
    ym = _mlstm(zqk_pad, b3(zmv), b3(zo), b3(zg), zgt, conv_w, gate_b_c, gate_b_t, norm_g.reshape(1, -1))
    t = bsz * seq
    return yf.reshape(t, -1), yg.reshape(t, -1), ym.reshape(t, -1)


def _moe(layer, h, yf, yg, ym, w_out, norm2_g, router_w, router_b, wg, wu, wd, sg, su, sd):
    t, d = h.shape
    sgu = jnp.concatenate([sg, su], axis=1).astype(BF16)
    tm = min(TOKEN_TILE, t)
    router_b_t = jnp.broadcast_to(router_b.reshape(-1, 1), (N_EXPERTS, tm))
    hs, xw, sel, gates, idx = _post_mix(h, yf, yg, ym, w_out.astype(BF16), norm2_g.reshape(1, d),
                                        router_w.T, router_b_t, sgu, sd.astype(BF16))
    rank, counts = _rank(sel)
    counts = counts[:, 0]
    pcounts = ((counts + EXPERT_BLOCK - 1) // EXPERT_BLOCK) * EXPERT_BLOCK
    pend = jnp.cumsum(pcounts)
    offs = (pend - pcounts).astype(I32)
    tail_start = jnp.where(pcounts > 0, pend - EXPERT_BLOCK, -1).astype(I32)
    n_blk = (t * TOP_K) // EXPERT_BLOCK + N_EXPERTS
    blk_start = jnp.arange(n_blk, dtype=I32) * EXPERT_BLOCK
    blk_e = jnp.minimum(jnp.sum(pend[None, :] <= blk_start[:, None], axis=1), N_EXPERTS - 1).astype(I32)
    n_used = (pend[-1:] // EXPERT_BLOCK).astype(I32)
    blk_i = jnp.arange(n_blk, dtype=I32)
    first = ((blk_i < n_used[0]) & ((blk_i == 0) | (blk_e != jnp.roll(blk_e, 1)))).astype(I32)
    after = pend[blk_e] // EXPERT_BLOCK
    next_e = jnp.where(after < n_used[0], blk_e[jnp.minimum(after, n_blk - 1)], -1).astype(I32)

    dest, g8 = _dest(rank, idx, gates, jnp.broadcast_to(offs.reshape(-1, 1), (N_EXPERTS, LANES)))
    dest_flat = dest.T.reshape(-1)
    g8 = jnp.pad(g8.T, ((0, 0), (0, LANES - TOP_K)))
    xs = _dispatch(dest_flat, tail_start, xw, n_blk * EXPERT_BLOCK)
    ys = _experts(layer, blk_e, n_used, first, next_e, xs, wg, wu, wd)
    return _combine(dest_flat, g8, hs, ys)


def kernel(x, p, norm1_g, w_in, four_w, gmlp_ln_g, gmlp_ln_b, gmlp_ws, gmlp_bs, mlstm_conv_w,
           mlstm_gate_b, mlstm_norm_g, w_out, norm2_g, router_w, router_b, exp_w_gate, exp_w_up,
           exp_w_down, sh_w_gate, sh_w_up, sh_w_down, ple_w_in, ple_w_gate, ple_norm_g, final_norm_g):
    bsz, seq, d = x.shape
    depth = w_in.shape[0]
    t = bsz * seq
    cos_s, sin_s = _dft_tables(seq)
    cos_m, sin_m = _dft_tables(FOUR_HD)
    eye = np.eye(FOUR_HEADS)
    tables = (jnp.asarray(cos_s, BF16), jnp.asarray(sin_s, BF16),
              jnp.asarray(np.kron(eye, cos_m), BF16), jnp.asarray(np.kron(eye, sin_m), BF16))
    h = x.reshape(t, d)
    for i in range(depth):
        yf, yg, ym = _mixers(h, bsz, seq, norm1_g[i], w_in[i], four_w[i], gmlp_ln_g[i], gmlp_ln_b[i],
                             gmlp_ws[i], gmlp_bs[i], mlstm_conv_w[i], mlstm_gate_b[i], mlstm_norm_g[i], tables)
        h2 = _moe(i, h, yf, yg, ym, w_out[i], norm2_g[i], router_w[i], router_b[i],
                  exp_w_gate, exp_w_up, exp_w_down, sh_w_gate[i], sh_w_up[i], sh_w_down[i])
        h = _ple(h2, p[i].reshape(t, -1), ple_w_in[i].astype(BF16), ple_w_gate[i].astype(BF16),
                 ple_norm_g[i].reshape(1, d), final_norm_g.reshape(1, d), i == depth - 1)
    return h.reshape(bsz, seq, d)
```

```python
import functools
import math

import numpy as np
import jax
import jax.numpy as jnp
from jax import lax
from jax.experimental import pallas as pl
from jax.experimental.pallas import tpu as pltpu

F32, BF16, I32, U32 = jnp.float32, jnp.bfloat16, jnp.int32, jnp.uint32

EPS = 1e-6
LANES = 128
VMEM_LIMIT_BYTES = 48 * 1024 * 1024

FOUR_HEADS, FOUR_HD = 4, 64
GMLP_HEADS, GMLP_HD, GMLP_CHUNK = 4, 64, 128
MLSTM_HEADS, MLSTM_DV, MLSTM_DQK, MLSTM_CHUNK = 4, 128, 64, 128
CONV_PAD = 16
MLSTM_N_ROWS = 8
N_GATES = 16
N_EXPERTS, TOP_K = 128, 8
EXPERT_BLOCK = 512
TOKEN_TILE = 1024
DISPATCH_TILE = 1024
COMBINE_TILE = 512
ROUTE_COLS = 128
ISSUE_GROUP = 8
REDUCE_GROUPS = 8


def _params(*sem):
    return pltpu.CompilerParams(dimension_semantics=sem, vmem_limit_bytes=VMEM_LIMIT_BYTES)


def _rms(x, g):
    return x * lax.rsqrt(jnp.mean(x * x, axis=-1, keepdims=True) + EPS) * g


def _full(shape):
    nd = len(shape)
    return pl.BlockSpec(shape, lambda *_: (0,) * nd)


def _dot(a, b):
    return jnp.dot(a, b, preferred_element_type=F32)


def _split3(x):
    hi = x.astype(BF16)
    rest = x - hi.astype(F32)
    mid = rest.astype(BF16)
    return hi, mid, (rest - mid.astype(F32)).astype(BF16)


def _pack_row(x):
    n = x.shape[-1] // 2
    lo = lax.bitcast_convert_type(x[:, :n].astype(BF16).astype(F32), U32)
    hi = lax.bitcast_convert_type(x[:, n:].astype(BF16).astype(F32), U32)
    return hi | (lo >> 16)


def _unpack_row(w):
    lo = lax.bitcast_convert_type(w << 16, F32)
    hi = lax.bitcast_convert_type(w & jnp.uint32(0xFFFF0000), F32)
    return lo, hi


ROW_SUB = 4


def _store_rows(ref, w):
    m = w.shape[0]
    for c in range(ROW_SUB):
        ref[pl.ds(c, m, stride=ROW_SUB), :] = w[:, c * LANES:(c + 1) * LANES]


def _load_rows(ref, r0, m):
    return [ref[pl.ds(r0 * ROW_SUB + c, m, stride=ROW_SUB), :] for c in range(ROW_SUB)]


def _in_proj_body(h_ref, g_ref, w_ref, wg_ref, wgt_ref,
                  zf_ref, zu_ref, zv_ref, zqk_ref, zmv_ref, zo_ref, zg_ref, zgt_ref):
    ab = _rms(h_ref[...], g_ref[...]).astype(BF16)
    off = 0
    for o_ref in (zf_ref, zu_ref, zv_ref, zqk_ref, zmv_ref, zo_ref):
        n = o_ref.shape[-1]
        o_ref[...] = _dot(ab, w_ref[:, off:off + n]).astype(o_ref.dtype)
        off += n
    zg_ref[...] = _dot(ab, wg_ref[...])
    zgt_ref[...] = lax.dot_general(wgt_ref[...], ab, (((1,), (1,)), ((), ())),
                                   preferred_element_type=F32)


def _in_proj(h, g, w_main, w_gate, w_gate_t):
    t, d = h.shape
    tm = min(TOKEN_TILE, t)
    widths = (256, 256, 256, 512, 512, 512)
    row = lambda n: pl.BlockSpec((tm, n), lambda i: (i, 0))
    out_shape = [jax.ShapeDtypeStruct((t, n), BF16) for n in widths]
    out_shape += [jax.ShapeDtypeStruct((t, LANES), F32), jax.ShapeDtypeStruct((N_GATES, t), F32)]
    out_specs = [row(n) for n in widths]
    out_specs += [row(LANES), pl.BlockSpec((N_GATES, tm), lambda i: (0, i))]
    return pl.pallas_call(
        _in_proj_body, grid=(t // tm,),
        in_specs=[row(d), _full((1, d)), _full(w_main.shape), _full(w_gate.shape), _full(w_gate_t.shape)],
        out_specs=out_specs, out_shape=out_shape,
        compiler_params=_params("parallel"), name="in_proj",
    )(h, g, w_main, w_gate, w_gate_t)


def _dft_tables(n):
    k = np.arange(n, dtype=np.int64)
    ang = 2.0 * np.pi * ((k[:, None] * k[None, :]) % n).astype(np.float64) / n
    return np.cos(ang), np.sin(ang)


def _block_diag(blocks):
    h, a, b = blocks.shape
    eye = jnp.eye(h, dtype=blocks.dtype)
    return (eye[:, None, :, None] * blocks[:, :, None, :]).reshape(h * a, h * b)


def _fourier_body(scale, row_tile, z_ref, cs_ref, ss_ref, cm_ref, sm_ref, w_ref, o_ref, p_scr, q_scr):
    z = z_ref[0]
    p_scr[...] = _dot(z, cm_ref[...]).astype(BF16)
    q_scr[...] = _dot(z, sm_ref[...]).astype(BF16)
    s = z.shape[0]
    for r in range(s // row_tile):
        rs = slice(r * row_tile, (r + 1) * row_tile)
        re = (_dot(cs_ref[rs, :], p_scr[...]) - _dot(ss_ref[rs, :], q_scr[...])) * scale
        o_ref[0, rs, :] = _dot(re.astype(BF16), w_ref[...]).astype(o_ref.dtype)


def _fourier(zf, cs, ss, cm, sm, wbd):
    b, s, w = zf.shape
    row_tile = min(512, s)
    scale = 1.0 / math.sqrt(s * FOUR_HD)
    blk = pl.BlockSpec((1, s, w), lambda i: (i, 0, 0))
    return pl.pallas_call(
        functools.partial(_fourier_body, scale, row_tile), grid=(b,),
        in_specs=[blk, _full(cs.shape), _full(ss.shape), _full(cm.shape), _full(sm.shape), _full(wbd.shape)],
        out_specs=blk, out_shape=jax.ShapeDtypeStruct((b, s, w), BF16),
        scratch_shapes=[pltpu.VMEM((s, w), BF16), pltpu.VMEM((s, w), BF16)],
        compiler_params=_params("parallel"), name="fourier",
    )(zf, cs, ss, cm, sm, wbd)


def _gmlp_body(zu_ref, zv_ref, lg_ref, lb_ref, ws_ref, bs_ref, o_ref):
    s = zu_ref.shape[1]
    w = zu_ref.shape[2]
    lane = lax.broadcasted_iota(I32, (GMLP_CHUNK, w), 1)

    def chunk(r0):
        rows = pl.ds(r0, GMLP_CHUNK)
        v = jax.nn.gelu(zv_ref[0, rows, :].astype(F32))
        vc = v - jnp.mean(v, axis=-1, keepdims=True)
        vn = vc * lax.rsqrt(jnp.mean(vc * vc, axis=-1, keepdims=True) + EPS) * lg_ref[...] + lb_ref[...]
        acc = bs_ref[...]
        for h in range(GMLP_HEADS):
            vh = jnp.where((lane >= h * GMLP_HD) & (lane < (h + 1) * GMLP_HD), vn, 0.0).astype(BF16)
            acc = acc + _dot(ws_ref[h], vh)
        u = jax.nn.gelu(zu_ref[0, rows, :].astype(F32))
        o_ref[0, rows, :] = (u * acc).astype(o_ref.dtype)

    def pair(c, carry):
        r0 = pl.multiple_of(c * (2 * GMLP_CHUNK), 2 * GMLP_CHUNK)
        chunk(r0)
        chunk(r0 + GMLP_CHUNK)
        return carry

    lax.fori_loop(0, s // (2 * GMLP_CHUNK), pair, 0)


def _gmlp(zu, zv, ln_g, ln_b, ws, bs_full):
    b, s, w = zu.shape
    blk = pl.BlockSpec((1, s, w), lambda i: (i, 0, 0))
    return pl.pallas_call(
        _gmlp_body, grid=(b,),
        in_specs=[blk, blk, _full((1, w)), _full((1, w)), _full(ws.shape), _full(bs_full.shape)],
        out_specs=blk, out_shape=jax.ShapeDtypeStruct((b, s, w), BF16),
        compiler_params=_params("parallel"), name="gmlp",
    )(zu, zv, ln_g, ln_b, ws, bs_full)


def _mlstm_body(qkp_ref, v_ref, zo_ref, g_ref, gt_ref, cw_ref, gb_ref, gbt_ref, ng_ref, o_ref,
                qm_scr, km_scr, vt_scr, hf_scr, hb_scr, c_scr, m_scr, gc_scr, gr_scr, cumc_scr, cumr_scr):
    L = MLSTM_CHUNK
    H = MLSTM_HEADS
    DV = MLSTM_DV
    N_ROWS = MLSTM_N_ROWS
    s = v_ref.shape[1]
    nc = s // L
    qkw = H * MLSTM_DQK
    nt = (((1,), (1,)), ((), ()))

    ext = L + 2 * CONV_PAD
    r_i = lax.broadcasted_iota(I32, (L, ext), 0)
    c_i = lax.broadcasted_iota(I32, (L, ext), 1)
    sh_m1 = jnp.where(c_i == r_i + CONV_PAD - 1, 1.0, 0.0).astype(BF16)
    sh_0 = jnp.where(c_i == r_i + CONV_PAD, 1.0, 0.0).astype(BF16)
    sh_p1 = jnp.where(c_i == r_i + CONV_PAD + 1, 1.0, 0.0).astype(BF16)
    lane = lax.broadcasted_iota(I32, (L, LANES), 1)
    low_half = lane < MLSTM_DQK

    def conv_one(r0):
        rows = pl.ds(r0, L)
        xe = qkp_ref[0, pl.ds(r0, ext), :]
        y = (cw_ref[0:1, :] * _dot(sh_m1, xe) + cw_ref[1:2, :] * _dot(sh_0, xe)
             + cw_ref[2:3, :] * _dot(sh_p1, xe))
        qk = y * jax.nn.sigmoid(y)
        for h in range(H):
            keep = low_half if h % 2 == 0 else jnp.logical_not(low_half)
            t0 = (h // 2) * LANES
            q_t = qk[:, t0:t0 + LANES] * (MLSTM_DQK ** -0.5)
            k_t = qk[:, qkw + t0:qkw + t0 + LANES]
            qm_scr[h, rows, :] = jnp.where(keep, q_t, 0.0).astype(BF16)
            km_scr[h, rows, :] = jnp.where(keep, k_t, 0.0).astype(BF16)
            vt_scr[h * DV:(h + 1) * DV, rows] = v_ref[0, rows, h * DV:(h + 1) * DV].astype(F32).T.astype(BF16)
        gc = g_ref[0, rows, :] + gb_ref[...]
        gr = gt_ref[:, rows] + gbt_ref[...]
        gc_scr[rows, :] = gc
        gr_scr[:, rows] = gr
        lf_c = jax.nn.log_sigmoid(gc)
        lf_r = jax.nn.log_sigmoid(gr)
        parts_c = _split3(lf_c)
        parts_r = _split3(lf_r)
        cumc_scr[0, rows, :] = sum(_dot(tri_l, p) for p in parts_c)
        cumc_scr[1, rows, :] = sum(_dot(tri_u, p) for p in parts_c)
        cumr_scr[0, :, rows] = sum(_dot(p, tri_u) for p in parts_r)
        cumr_scr[1, :, rows] = sum(_dot(p, tri_l) for p in parts_r)

    def conv_chunk(c, carry):
        r0 = pl.multiple_of(c * (2 * L), 2 * L)
        conv_one(r0)
        conv_one(r0 + L)
        return carry

    ri = lax.broadcasted_iota(I32, (L, L), 0)
    ci = lax.broadcasted_iota(I32, (L, L), 1)
    tri_l = jnp.where(ci <= ri, 1.0, 0.0).astype(BF16)
    tri_u = jnp.where(ci >= ri, 1.0, 0.0).astype(BF16)
    lax.fori_loop(0, nc // 2, conv_chunk, 0)

    c_scr[...] = jnp.zeros(c_scr.shape, F32)
    m_scr[...] = jnp.zeros(m_scr.shape, F32)

    def direction(d, chunk):
        r0 = pl.multiple_of(chunk * L, L)
        rows = pl.ds(r0, L)
        gc = gc_scr[rows, :]
        gr = gr_scr[:, rows]
        cum_c = cumc_scr[d, rows, :]
        cum_r = cumr_scr[d, :, rows]
        mask = (ci >= ri) if d == 0 else (ci <= ri)
        i_lane = 2 * d * H
        f_lane = (2 * d + 1) * H
        for h in range(H):
            u = d * H + h
            b_r = cum_r[f_lane + h:f_lane + h + 1, :]
            li_r = gr[i_lane + h:i_lane + h + 1, :]
            key_c = gc[:, i_lane + h:i_lane + h + 1] - cum_c[:, f_lane + h:f_lane + h + 1]
            g_edge = b_r[:, L - 1:L] if d == 0 else b_r[:, 0:1]
            g_tot = jnp.broadcast_to(g_edge, (1, L))
            qh = qm_scr[h, rows, :]
            kh = km_scr[h, rows, :]
            vt = vt_scr[h * DV:(h + 1) * DV, rows]
            cn_prev = c_scr[u]
            m_prev = m_scr[u]
            dmat = jnp.where(mask, b_r + key_c, -jnp.inf)
            m_intra = jnp.max(dmat, axis=0, keepdims=True)
            m_inter = b_r + m_prev
            m_tot = jnp.maximum(m_intra, m_inter)
            kq = lax.dot_general(kh, qh, nt, preferred_element_type=F32)
            s_mat = jnp.exp(dmat - m_tot) * kq
            inter = jnp.exp(m_inter - m_tot)
            state_q = lax.dot_general(cn_prev.astype(BF16), qh, nt, preferred_element_type=F32)
            den = jnp.sum(s_mat, axis=0, keepdims=True) + inter * state_q[DV:DV + 1, :]
            num = _dot(vt, s_mat.astype(BF16)) + inter * state_q[0:DV, :]
            dst = hf_scr if d == 0 else hb_scr
            dst[h * DV:(h + 1) * DV, rows] = num / jnp.maximum(jnp.abs(den), jnp.exp(-m_tot))
            a_r = g_tot - b_r + li_r
            m_loc = jnp.broadcast_to(jnp.max(a_r, axis=-1, keepdims=True), (1, L))
            w_r = jnp.exp(a_r - m_loc)
            weighted = jnp.concatenate([vt.astype(F32) * w_r, jnp.broadcast_to(w_r, (N_ROWS, L))], axis=0)
            cn_loc = _dot(weighted.astype(BF16), kh)
            m_new = jnp.maximum(g_tot + m_prev, m_loc)
            a_old = jnp.exp(g_tot + m_prev - m_new)
            a_new = jnp.exp(m_loc - m_new)
            c_scr[u] = a_old * cn_prev + a_new * cn_loc
            m_scr[u] = m_new

    def step(i, carry):
        direction(0, i)
        direction(1, nc - 1 - i)
        return carry

    lax.fori_loop(0, nc, step, 0)

    def finish(c, carry):
        r0 = pl.multiple_of(c * L, L)
        rows = pl.ds(r0, L)
        og = jax.nn.sigmoid(zo_ref[0, rows, :].astype(F32))
        for h in range(H):
            cols = slice(h * DV, (h + 1) * DV)
            x = og[:, cols] * (hf_scr[cols, rows] + hb_scr[cols, rows]).T
            xc = x - jnp.mean(x, axis=-1, keepdims=True)
            y = xc * lax.rsqrt(jnp.mean(xc * xc, axis=-1, keepdims=True) + EPS) * ng_ref[:, cols]
            o_ref[0, rows, cols] = y.astype(o_ref.dtype)
        return carry

    lax.fori_loop(0, nc, finish, 0)


def _mlstm(zqk_pad, zmv, zo, zg, zgt, conv_w, gate_b, gate_b_t, norm_g):
    b, s, w = zmv.shape
    H = MLSTM_HEADS
    blk = pl.BlockSpec((1, s, w), lambda i: (i, 0, 0))
    units = 2 * H
    return pl.pallas_call(
        _mlstm_body, grid=(b,),
        in_specs=[pl.BlockSpec((1, s + 2 * CONV_PAD, w), lambda i: (i, 0, 0)), blk, blk,
                  pl.BlockSpec((1, s, LANES), lambda i: (i, 0, 0)),
                  pl.BlockSpec((N_GATES, s), lambda i: (0, i)),
                  _full(conv_w.shape), _full(gate_b.shape), _full(gate_b_t.shape), _full(norm_g.shape)],
        out_specs=blk, out_shape=jax.ShapeDtypeStruct((b, s, w), BF16),
        scratch_shapes=[pltpu.VMEM((H, s, LANES), BF16), pltpu.VMEM((H, s, LANES), BF16),
                        pltpu.VMEM((w, s), BF16), pltpu.VMEM((w, s), F32), pltpu.VMEM((w, s), F32),
                        pltpu.VMEM((units, MLSTM_DV + MLSTM_N_ROWS, LANES), F32), pltpu.VMEM((units, 1, LANES), F32),
                        pltpu.VMEM((s, LANES), F32), pltpu.VMEM((N_GATES, s), F32),
                        pltpu.VMEM((2, s, LANES), F32), pltpu.VMEM((2, N_GATES, s), F32)],
        compiler_params=_params("parallel"), name="mlstm",
    )(zqk_pad, zmv, zo, zg, zgt, conv_w, gate_b, gate_b_t, norm_g)


def _post_mix_body(h_ref, yf_ref, yg_ref, ym_ref, wo_ref, g2_ref, rwh_ref, rwl_ref, rbt_ref, sgu_ref, sd_ref,
                   hs_ref, xw_ref, sel_ref, gate_ref, idx_ref, sc_scr):
    wf, wg = yf_ref.shape[-1], yg_ref.shape[-1]
    h1 = (h_ref[...] + _dot(yf_ref[...], wo_ref[0:wf, :]) + _dot(yg_ref[...], wo_ref[wf:wf + wg, :])
          + _dot(ym_ref[...], wo_ref[wf + wg:, :]))
    xn = _rms(h1, g2_ref[...])
    xb = xn.astype(BF16)
    _store_rows(xw_ref, _pack_row(xn))
    gu = _dot(xb, sgu_ref[...])
    de = gu.shape[-1] // 2
    act = (jax.nn.silu(gu[:, :de]) * gu[:, de:]).astype(BF16)
    hs_ref[...] = h1 + _dot(act, sd_ref[...])
    nt = (((1,), (1,)), ((), ()))
    x_lo = (xn - xb.astype(F32)).astype(BF16)
    logits = (lax.dot_general(rwh_ref[...], xb, nt, preferred_element_type=F32)
              + lax.dot_general(rwh_ref[...], x_lo, nt, preferred_element_type=F32)
              + lax.dot_general(rwl_ref[...], xb, nt, preferred_element_type=F32))
    sc_scr[...] = jax.nn.sigmoid(logits)
    for c0 in range(0, sc_scr.shape[1], ROUTE_COLS):
        cols = slice(c0, c0 + ROUTE_COLS)
        scores = sc_scr[:, cols]
        work = scores + rbt_ref[:, cols]
        expert = lax.broadcasted_iota(I32, scores.shape, 0)
        choice = lax.broadcasted_iota(I32, (TOP_K, ROUTE_COLS), 0)
        sel = jnp.zeros(scores.shape, F32)
        idx = jnp.zeros(choice.shape, I32)
        for k in range(TOP_K):
            m = jnp.max(work, axis=0, keepdims=True)
            e = jnp.min(jnp.where(work == m, expert, N_EXPERTS), axis=0, keepdims=True)
            hit = expert == e
            sel = jnp.where(hit, 1.0, sel)
            idx = jnp.where(choice == k, e, idx)
            work = jnp.where(hit, -jnp.inf, work)
        picked = sel * scores
        sel_ref[:, cols] = sel
        gate_ref[:, cols] = picked / jnp.sum(picked, axis=0, keepdims=True)
        idx_ref[:, cols] = idx


def _post_mix(h, yf, yg, ym, w_out, g2, router_w_t, router_b_t, sgu, sd):
    t, d = h.shape
    tm = min(TOKEN_TILE, t)
    row = lambda n: pl.BlockSpec((tm, n), lambda i: (i, 0))
    col = lambda n: pl.BlockSpec((n, tm), lambda i: (0, i))
    rw_hi = router_w_t.astype(BF16)
    rw_lo = (router_w_t - rw_hi.astype(F32)).astype(BF16)
    return pl.pallas_call(
        _post_mix_body, grid=(t // tm,),
        in_specs=[row(d), row(yf.shape[1]), row(yg.shape[1]), row(ym.shape[1]), _full(w_out.shape),
                  _full((1, d)), _full(rw_hi.shape), _full(rw_lo.shape), _full(router_b_t.shape),
                  _full(sgu.shape), _full(sd.shape)],
        out_specs=[row(d), pl.BlockSpec((tm * ROW_SUB, LANES), lambda i: (i, 0)),
                   col(N_EXPERTS), col(N_EXPERTS), col(TOP_K)],
        out_shape=[jax.ShapeDtypeStruct((t, d), F32), jax.ShapeDtypeStruct((t * ROW_SUB, LANES), U32),
                   jax.ShapeDtypeStruct((N_EXPERTS, t), F32), jax.ShapeDtypeStruct((N_EXPERTS, t), F32),
                   jax.ShapeDtypeStruct((TOP_K, t), I32)],
        scratch_shapes=[pltpu.VMEM((N_EXPERTS, tm), F32)],
        compiler_params=_params("parallel"), name="post_mix",
    )(h, yf, yg, ym, w_out, g2, rw_hi, rw_lo, router_b_t, sgu, sd)


def _rank_body(sel_ref, rank_ref, cnt_ref, carry):
    @pl.when(pl.program_id(0) == 0)
    def _():
        carry[...] = jnp.zeros(carry.shape, F32)

    m = sel_ref[...].astype(BF16)
    tm = m.shape[1]
    ri = lax.broadcasted_iota(I32, (tm, tm), 0)
    ci = lax.broadcasted_iota(I32, (tm, tm), 1)
    earlier = jnp.where(ri < ci, 1.0, 0.0).astype(BF16)
    seen = carry[...]
    rank = _dot(m, earlier) + jnp.concatenate([seen] * (tm // LANES), axis=1)
    rank_ref[...] = rank.astype(I32)
    seen = seen + _dot(m, jnp.ones((tm, LANES), BF16))
    carry[...] = seen
    cnt_ref[...] = seen.astype(I32)


def _rank(sel_t):
    e, t = sel_t.shape
    tm = min(TOKEN_TILE, t)
    return pl.pallas_call(
        _rank_body, grid=(t // tm,),
        in_specs=[pl.BlockSpec((e, tm), lambda i: (0, i))],
        out_specs=[pl.BlockSpec((e, tm), lambda i: (0, i)), _full((e, LANES))],
        out_shape=[jax.ShapeDtypeStruct((e, t), I32), jax.ShapeDtypeStruct((e, LANES), I32)],
        scratch_shapes=[pltpu.VMEM((e, LANES), F32)],
        compiler_params=_params("arbitrary"), name="rank",
    )(sel_t)


def _dest_body(rank_ref, idx_ref, gate_ref, offs_ref, dest_ref, g8_ref):
    tm = rank_ref.shape[1]
    offs = jnp.concatenate([offs_ref[...]] * (tm // LANES), axis=1)
    pos = (rank_ref[...] + offs).astype(F32)
    gates = gate_ref[...]
    expert = lax.broadcasted_iota(I32, pos.shape, 0)
    choice = lax.broadcasted_iota(I32, (TOP_K, tm), 0)
    dest = jnp.zeros(choice.shape, F32)
    g8 = jnp.zeros(choice.shape, F32)
    for k in range(TOP_K):
        hit = expert == idx_ref[k:k + 1, :]
        dk = jnp.sum(jnp.where(hit, pos, 0.0), axis=0, keepdims=True)
        gk = jnp.sum(jnp.where(hit, gates, 0.0), axis=0, keepdims=True)
        dest = jnp.where(choice == k, dk, dest)
        g8 = jnp.where(choice == k, gk, g8)
    dest_ref[...] = dest.astype(I32)
    g8_ref[...] = g8


def _dest(rank_t, idx_t, gates_t, offs_rep):
    e, t = rank_t.shape
    tm = min(TOKEN_TILE, t)
    col = lambda n: pl.BlockSpec((n, tm), lambda i: (0, i))
    return pl.pallas_call(
        _dest_body, grid=(t // tm,),
        in_specs=[col(e), col(TOP_K), col(e), _full((e, LANES))],
        out_specs=[col(TOP_K), col(TOP_K)],
        out_shape=[jax.ShapeDtypeStruct((TOP_K, t), I32), jax.ShapeDtypeStruct((TOP_K, t), F32)],
        compiler_params=_params("parallel"), name="dest",
    )(rank_t, idx_t, gates_t, offs_rep)


def _dispatch_body(dest_ref, tail_ref, x_ref, xs_hbm, zero_scr, sem):
    n_grp = x_ref.shape[0]

    @pl.when(pl.program_id(0) == 0)
    def _():
        zero_scr[...] = jnp.zeros(zero_scr.shape, zero_scr.dtype)

        def tail_copy(e):
            start = pl.multiple_of(jnp.maximum(tail_ref[e], 0) * ROW_SUB, EXPERT_BLOCK * ROW_SUB)
            return pltpu.make_async_copy(zero_scr, xs_hbm.at[pl.ds(start, EXPERT_BLOCK * ROW_SUB), :], sem)

        def clear(e, carry):
            @pl.when(tail_ref[e] >= 0)
            def _():
                tail_copy(e).start()
            return carry

        def clear_done(e, carry):
            @pl.when(tail_ref[e] >= 0)
            def _():
                tail_copy(e).wait()
            return carry

        lax.fori_loop(0, N_EXPERTS, clear, 0)
        lax.fori_loop(0, N_EXPERTS, clear_done, 0)

    def row_copy(group, g, dst_row):
        dst = pl.ds(pl.multiple_of(dst_row * ROW_SUB, ROW_SUB), ROW_SUB)
        return pltpu.make_async_copy(x_ref.at[group, g], xs_hbm.at[dst, :], sem)

    def for_group(op):
        def body(gi, carry):
            p0 = gi * (ISSUE_GROUP * TOP_K)
            for g in range(ISSUE_GROUP):
                for k in range(TOP_K):
                    op(row_copy(gi, g, dest_ref[p0 + g * TOP_K + k]), k)
            return carry
        lax.fori_loop(0, n_grp, body, 0)

    for_group(lambda cp, k: cp.start(priority=k % 2))
    for_group(lambda cp, k: cp.wait())


def _dispatch(dest_flat, tail_start, xw, n_rows):
    t = xw.shape[0] // ROW_SUB
    tt = min(DISPATCH_TILE, t)
    x4 = xw.reshape(t // ISSUE_GROUP, ISSUE_GROUP, ROW_SUB, LANES)
    return pl.pallas_call(
        _dispatch_body, grid=(t // tt,),
        in_specs=[pl.BlockSpec((tt * TOP_K,), lambda i: (i,), memory_space=pltpu.SMEM),
                  pl.BlockSpec(memory_space=pltpu.SMEM),
                  pl.BlockSpec((tt // ISSUE_GROUP, ISSUE_GROUP, ROW_SUB, LANES), lambda i: (i, 0, 0, 0))],
        out_specs=pl.BlockSpec(memory_space=pl.ANY),
        out_shape=jax.ShapeDtypeStruct((n_rows * ROW_SUB, LANES), xw.dtype),
        scratch_shapes=[pltpu.VMEM((EXPERT_BLOCK * ROW_SUB, LANES), xw.dtype), pltpu.SemaphoreType.DMA(())],
        compiler_params=_params("arbitrary"), name="dispatch",
    )(dest_flat, tail_start, x4)


def _experts_body(layer, blk_e_ref, n_used_ref, first_ref, next_e_ref, xs_ref, wg_hbm, wu_hbm, wd_hbm, ys_ref,
                  wg_buf, wu_buf, wd_buf, wg_scr, wu_scr, wd_scr, slot_ref, sems):
    i = pl.program_id(0)
    used = i < n_used_ref[0]

    def weight_copies(e, slot):
        return [pltpu.make_async_copy(hbm.at[layer, e], buf.at[slot], sems.at[slot])
                for hbm, buf in ((wg_hbm, wg_buf), (wu_hbm, wu_buf), (wd_hbm, wd_buf))]

    @pl.when(i == 0)
    def _():
        slot_ref[0] = 0
        for cp in weight_copies(blk_e_ref[0], 0):
            cp.start()

    @pl.when(first_ref[i] == 1)
    def _():
        slot = slot_ref[0]
        for cp in weight_copies(blk_e_ref[i], slot):
            cp.wait()
        wg_scr[...] = wg_buf[slot].astype(BF16)
        wu_scr[...] = wu_buf[slot].astype(BF16)
        wd_scr[...] = wd_buf[slot].astype(BF16)

        @pl.when(next_e_ref[i] >= 0)
        def _():
            for cp in weight_copies(next_e_ref[i], 1 - slot):
                cp.start(priority=1)

        slot_ref[0] = 1 - slot

    @pl.when(used)
    def _():
        halves = [_unpack_row(w) for w in _load_rows(xs_ref, 0, EXPERT_BLOCK)]
        x = jnp.concatenate([lo for lo, _ in halves] + [hi for _, hi in halves], axis=-1).astype(BF16)
        a = (jax.nn.silu(_dot(x, wg_scr[...])) * _dot(x, wu_scr[...])).astype(BF16)
        _store_rows(ys_ref, _pack_row(_dot(a, wd_scr[...])))


def _experts(layer, blk_e, n_used, first, next_e, xs, wg, wu, wd):
    blk = (EXPERT_BLOCK * ROW_SUB, LANES)
    nb = xs.shape[0] // blk[0]
    d, de = wg.shape[-2], wg.shape[-1]
    last_used = lambda i, nu: jnp.minimum(i, nu[0] - 1)
    hbm = pl.BlockSpec(memory_space=pl.ANY)
    grid_spec = pltpu.PrefetchScalarGridSpec(
        num_scalar_prefetch=4, grid=(nb,),
        in_specs=[pl.BlockSpec(blk, lambda i, be, nu, fi, ne: (last_used(i, nu), 0)), hbm, hbm, hbm],
        out_specs=pl.BlockSpec(blk, lambda i, be, nu, fi, ne: (last_used(i, nu), 0)),
        scratch_shapes=[pltpu.VMEM((2, d, de), F32), pltpu.VMEM((2, d, de), F32), pltpu.VMEM((2, de, d), F32),
                        pltpu.VMEM((d, de), BF16), pltpu.VMEM((d, de), BF16), pltpu.VMEM((de, d), BF16),
                        pltpu.SMEM((1,), I32), pltpu.SemaphoreType.DMA((2,))])
    return pl.pallas_call(
        functools.partial(_experts_body, layer), grid_spec=grid_spec,
        out_shape=jax.ShapeDtypeStruct(xs.shape, xs.dtype),
        compiler_params=_params("arbitrary"), name="experts",
    )(blk_e, n_used, first, next_e, xs, wg, wu, wd)


def _combine_body(dest_ref, dest_next_ref, g8_ref, hs_ref, ys_hbm, o_ref, buf, sems):
    n_tok = hs_ref.shape[0]
    n_grp = n_tok // ISSUE_GROUP
    step = pl.program_id(0)
    cur = step % 2
    half_rows = TOP_K * n_tok

    def for_group(idx_ref, slot, op):
        base = slot * (half_rows * ROW_SUB)

        def row_copy(src_row, k, gi, g):
            src = pl.ds(pl.multiple_of(src_row * ROW_SUB, ROW_SUB), ROW_SUB)
            dst = base + (k * n_grp + gi) * (ISSUE_GROUP * ROW_SUB) + g * ROW_SUB
            return pltpu.make_async_copy(ys_hbm.at[src, :], buf.at[pl.ds(pl.multiple_of(dst, ROW_SUB), ROW_SUB), :],
                                         sems.at[slot])

        def body(gi, carry):
            p0 = gi * (ISSUE_GROUP * TOP_K)
            for g in range(ISSUE_GROUP):
                for k in range(TOP_K):
                    op(row_copy(idx_ref[p0 + g * TOP_K + k], k, gi, g), k)
            return carry
        lax.fori_loop(0, n_grp, body, 0)

    start = lambda cp, k: cp.start(priority=k % 2)

    @pl.when(step == 0)
    def _():
        for_group(dest_ref, 0, start)

    @pl.when(step + 1 < pl.num_programs(0))
    def _():
        for_group(dest_next_ref, 1 - cur, start)

    for_group(dest_ref, cur, lambda cp, k: cp.wait())

    half = ROW_SUB * LANES

    def reduce_group(r0):
        rows = pl.ds(r0, ISSUE_GROUP)
        gates = [g8_ref[rows, k:k + 1] for k in range(TOP_K)]
        picked = [_load_rows(buf, cur * half_rows + k * n_tok + r0, ISSUE_GROUP) for k in range(TOP_K)]
        for c in range(ROW_SUB):
            lo_cols = slice(c * LANES, (c + 1) * LANES)
            hi_cols = slice(half + c * LANES, half + (c + 1) * LANES)
            acc_lo = hs_ref[rows, lo_cols]
            acc_hi = hs_ref[rows, hi_cols]
            for k in range(TOP_K):
                lo, hi = _unpack_row(picked[k][c])
                acc_lo = acc_lo + gates[k] * lo
                acc_hi = acc_hi + gates[k] * hi
            o_ref[rows, lo_cols] = acc_lo
            o_ref[rows, hi_cols] = acc_hi

    def reduce(gi, carry):
        r0 = pl.multiple_of(gi * (REDUCE_GROUPS * ISSUE_GROUP), REDUCE_GROUPS * ISSUE_GROUP)
        for q in range(REDUCE_GROUPS):
            reduce_group(r0 + q * ISSUE_GROUP)
        return carry

    lax.fori_loop(0, n_grp // REDUCE_GROUPS, reduce, 0)


def _combine(dest_flat, g8, hs, ys):
    t, d = hs.shape
    tt = min(COMBINE_TILE, t)
    n_steps = t // tt
    return pl.pallas_call(
        _combine_body, grid=(n_steps,),
        in_specs=[pl.BlockSpec((tt * TOP_K,), lambda i: (i,), memory_space=pltpu.SMEM),
                  pl.BlockSpec((tt * TOP_K,), lambda i: (jnp.minimum(i + 1, n_steps - 1),), memory_space=pltpu.SMEM),
                  pl.BlockSpec((tt, g8.shape[1]), lambda i: (i, 0)),
                  pl.BlockSpec((tt, d), lambda i: (i, 0)),
                  pl.BlockSpec(memory_space=pl.ANY)],
        out_specs=pl.BlockSpec((tt, d), lambda i: (i, 0)),
        out_shape=jax.ShapeDtypeStruct((t, d), F32),
        scratch_shapes=[pltpu.VMEM((2 * TOP_K * tt * ROW_SUB, LANES), ys.dtype), pltpu.SemaphoreType.DMA((2,))],
        compiler_params=_params("arbitrary"), name="combine",
    )(dest_flat, dest_flat, g8, hs, ys)


def _ple_body(final, h2_ref, p_ref, win_ref, wgate_ref, g_ref, gf_ref, o_ref):
    h2 = h2_ref[...]
    e = _dot(p_ref[...].astype(BF16), win_ref[...])
    gate = jax.nn.sigmoid(_dot(h2.astype(BF16), wgate_ref[...]))
    h3 = h2 + _rms(gate * e, g_ref[...])
    o_ref[...] = _rms(h3, gf_ref[...]) if final else h3


def _ple(h2, p, w_in, w_gate, g, g_final, final):
    t, d = h2.shape
    tm = min(TOKEN_TILE, t)
    row = lambda n: pl.BlockSpec((tm, n), lambda i: (i, 0))
    return pl.pallas_call(
        functools.partial(_ple_body, final), grid=(t // tm,),
        in_specs=[row(d), row(p.shape[1]), _full(w_in.shape), _full(w_gate.shape), _full((1, d)), _full((1, d))],
        out_specs=row(d), out_shape=jax.ShapeDtypeStruct((t, d), F32),
        compiler_params=_params("parallel"), name="ple",
    )(h2, p, w_in, w_gate, g, g_final)


def _mixers(h, bsz, seq, norm1_g, w_in, four_w, ln_g, ln_b, ws, bs, conv_w, gate_b, norm_g, tables):
    d = h.shape[1]
    n_main = w_in.shape[1] - N_GATES
    w_main = w_in[:, :n_main].astype(BF16)
    w_g = w_in[:, n_main:]
    w_gate = jnp.pad(w_g, ((0, 0), (0, LANES - N_GATES))).astype(BF16)
    w_gate_t = w_g.T.astype(BF16)
    zf, zu, zv, zqk, zmv, zo, zg, zgt = _in_proj(h, norm1_g.reshape(1, d), w_main, w_gate, w_gate_t)
    b3 = lambda a: a.reshape(bsz, seq, a.shape[-1])

    cs, ss, cm, sm = tables
    yf = _fourier(b3(zf), cs, ss, cm, sm, _block_diag(four_w).astype(BF16))

    bs_full = jnp.repeat(bs.T, GMLP_HD, axis=1)
    yg = _gmlp(b3(zu), b3(zv), ln_g.reshape(1, -1), ln_b.reshape(1, -1), ws.astype(BF16), bs_full)

    zqk_pad = jnp.pad(b3(zqk), ((0, 0), (CONV_PAD, CONV_PAD), (0, 0)))
    gate_b_c = jnp.pad(gate_b, (0, LANES - N_GATES)).reshape(1, LANES)
    gate_b_t = jnp.broadcast_to(gate_b.reshape(N_GATES, 1), (N_GATES, LANES))
    ym = _mlstm(zqk_pad, b3(zmv), b3(zo), b3(zg), zgt, conv_w, gate_b_c, gate_b_t, norm_g.reshape(1, -1))
    t = bsz * seq
    return yf.reshape(t, -1), yg.reshape(t, -1), ym.reshape(t, -1)


def _moe(layer, h, yf, yg, ym, w_out, norm2_g, router_w, router_b, wg, wu, wd, sg, su, sd):
    t, d = h.shape
    sgu = jnp.concatenate([sg, su], axis=1).astype(BF16)
    tm = min(TOKEN_TILE, t)
    router_b_t = jnp.broadcast_to(router_b.reshape(-1, 1), (N_EXPERTS, tm))
    hs, xw, sel, gates, idx = _post_mix(h, yf, yg, ym, w_out.astype(BF16), norm2_g.reshape(1, d),
                                        router_w.T, router_b_t, sgu, sd.astype(BF16))
    rank, counts = _rank(sel)
    counts = counts[:, 0]
    pcounts = ((counts + EXPERT_BLOCK - 1) // EXPERT_BLOCK) * EXPERT_BLOCK
    pend = jnp.cumsum(pcounts)
    offs = (pend - pcounts).astype(I32)
    tail_start = jnp.where(pcounts > 0, pend - EXPERT_BLOCK, -1).astype(I32)
    n_blk = (t * TOP_K) // EXPERT_BLOCK + N_EXPERTS
    blk_start = jnp.arange(n_blk, dtype=I32) * EXPERT_BLOCK
    blk_e = jnp.minimum(jnp.sum(pend[None, :] <= blk_start[:, None], axis=1), N_EXPERTS - 1).astype(I32)
    n_used = (pend[-1:] // EXPERT_BLOCK).astype(I32)
    blk_i = jnp.arange(n_blk, dtype=I32)
    first = ((blk_i < n_used[0]) & ((blk_i == 0) | (blk_e != jnp.roll(blk_e, 1)))).astype(I32)
    after = pend[blk_e] // EXPERT_BLOCK
    next_e = jnp.where(after < n_used[0], blk_e[jnp.minimum(after, n_blk - 1)], -1).astype(I32)

    dest, g8 = _dest(rank, idx, gates, jnp.broadcast_to(offs.reshape(-1, 1), (N_EXPERTS, LANES)))
    dest_flat = dest.T.reshape(-1)
    g8 = jnp.pad(g8.T, ((0, 0), (0, LANES - TOP_K)))
    xs = _dispatch(dest_flat, tail_start, xw, n_blk * EXPERT_BLOCK)
    ys = _experts(layer, blk_e, n_used, first, next_e, xs, wg, wu, wd)
    return _combine(dest_flat, g8, hs, ys)


def kernel(x, p, norm1_g, w_in, four_w, gmlp_ln_g, gmlp_ln_b, gmlp_ws, gmlp_bs, mlstm_conv_w,
           mlstm_gate_b, mlstm_norm_g, w_out, norm2_g, router_w, router_b, exp_w_gate, exp_w_up,
           exp_w_down, sh_w_gate, sh_w_up, sh_w_down, ple_w_in, ple_w_gate, ple_norm_g, final_norm_g):
    bsz, seq, d = x.shape
    depth = w_in.shape[0]
    t = bsz * seq
    cos_s, sin_s = _dft_tables(seq)
    cos_m, sin_m = _dft_tables(FOUR_HD)
    eye = np.eye(FOUR_HEADS)
    tables = (jnp.asarray(cos_s, BF16), jnp.asarray(sin_s, BF16),
              jnp.asarray(np.kron(eye, cos_m), BF16), jnp.asarray(np.kron(eye, sin_m), BF16))
    h = x.reshape(t, d)
    for i in range(depth):
        yf, yg, ym = _mixers(h, bsz, seq, norm1_g[i], w_in[i], four_w[i], gmlp_ln_g[i], gmlp_ln_b[i],
                             gmlp_ws[i], gmlp_bs[i], mlstm_conv_w[i], mlstm_gate_b[i], mlstm_norm_g[i], tables)
        h2 = _moe(i, h, yf, yg, ym, w_out[i], norm2_g[i], router_w[i], router_b[i],
                  exp_w_gate, exp_w_up, exp_w_down, sh_w_gate[i], sh_w_up[i], sh_w_down[i])
        h = _ple(h2, p[i].reshape(t, -1), ple_w_in[i].astype(BF16), ple_w_gate[i].astype(BF16),
                 ple_norm_g[i].reshape(1, d), final_norm_g.reshape(1, d), i == depth - 1)
    return h.reshape(bsz, seq, d)
```
